```python
import jax, jax.numpy as jnp
from jax import lax
import numpy as np

D_MODEL = 1024
BATCH = 2
SEQ = 8192
DEPTH = 4

HEAD_DIM = 64
ATT_Q_HEADS = 6
ATT_KV_HEADS = 2
ATT_GROUP = ATT_Q_HEADS // ATT_KV_HEADS
RET_HEADS = 6
POOL_WIDTH = D_MODEL - (ATT_Q_HEADS + RET_HEADS) * HEAD_DIM
POOL_WINDOWS = (2, 4, 8, 16)
POOL_GROUPS = len(POOL_WINDOWS)
POOL_GROUP_WIDTH = POOL_WIDTH // POOL_GROUPS
WINDOW = 128
BLOCK = 128
RET_CHUNK = 128
N_BUCKETS = 32
MAX_DISTANCE = 128
D_FF = 4 * D_MODEL
RMS_EPS = 1e-6
ROPE_BASE = 10000.0
NEG_INF = -1e30

ATT_Q_W = ATT_Q_HEADS * HEAD_DIM
ATT_KV_W = ATT_KV_HEADS * HEAD_DIM
RET_W = RET_HEADS * HEAD_DIM
IN_SIZES = (ATT_Q_W, ATT_KV_W, ATT_KV_W, RET_W, RET_W, RET_W, RET_W, POOL_WIDTH)
IN_WIDTH = sum(IN_SIZES)
SPLIT_POINTS = tuple(int(v) for v in np.cumsum(IN_SIZES)[:-1])
MIX_WIDTH = ATT_Q_W + RET_W + POOL_WIDTH

kernel_name = "hybrid_swa_retention_pool_block"


def rms_norm(x, g):
    xf = x.astype(jnp.float32)
    y = xf * lax.rsqrt(jnp.mean(xf * xf, axis=-1, keepdims=True) + RMS_EPS)
    return (y * g.astype(jnp.float32)).astype(x.dtype)


def t5_bucket(dist):
    max_exact = N_BUCKETS // 2
    n = np.maximum(dist, 0)
    large = max_exact + (np.log(np.maximum(n, 1) / max_exact)
                         / np.log(MAX_DISTANCE / max_exact)
                         * (N_BUCKETS - max_exact)).astype(np.int64)
    large = np.minimum(large, N_BUCKETS - 1)
    return np.where(n < max_exact, n, large).astype(np.int32)


def sliding_window_attention(q, k, v, sinks, rel_bias):
    B, S = q.shape[0], q.shape[1]
    N = S // BLOCK
    qb = q.reshape(B, N, BLOCK, ATT_KV_HEADS, ATT_GROUP, HEAD_DIM)
    kb = k.reshape(B, N, BLOCK, ATT_KV_HEADS, HEAD_DIM)
    vb = v.reshape(B, N, BLOCK, ATT_KV_HEADS, HEAD_DIM)

    def with_prev(t):
        prev = jnp.pad(t, ((0, 0), (1, 0), (0, 0), (0, 0), (0, 0)))[:, :-1]
        return jnp.concatenate([prev, t], axis=2)

    kk, vv = with_prev(kb), with_prev(vb)

    i = np.arange(BLOCK)[:, None]
    j = np.arange(2 * BLOCK)[None, :]
    dist = BLOCK + i - j
    band = (dist >= 0) & (dist < WINDOW)
    in_cur = j >= BLOCK
    blk = jnp.arange(N)[:, None, None]
    valid = jnp.asarray(band)[None] & ((blk > 0) | jnp.asarray(in_cur)[None])

    bias = rel_bias.astype(jnp.float32)[t5_bucket(dist)]
    bias = jnp.transpose(bias, (2, 0, 1)).reshape(ATT_KV_HEADS, ATT_GROUP, BLOCK, 2 * BLOCK)

    scores = jnp.einsum('bnihgd,bnjhd->bnhgij', qb, kk).astype(jnp.float32)
    scores = scores * (HEAD_DIM ** -0.5) + bias
    scores = jnp.where(valid[None, :, None, None], scores, NEG_INF)

    sink = sinks.astype(jnp.float32).reshape(ATT_KV_HEADS, ATT_GROUP)[None, None, :, :, None]
    m = jnp.maximum(scores.max(axis=-1), sink)
    p = jnp.exp(scores - m[..., None])
    denom = p.sum(axis=-1) + jnp.exp(sink - m)
    o = jnp.einsum('bnhgij,bnjhd->bnihgd', p, vv.astype(jnp.float32))
    o = o / jnp.transpose(denom, (0, 1, 4, 2, 3))[..., None]
    return o.reshape(B, S, ATT_Q_W).astype(q.dtype)


def rotary(x, pos):
    inv = 1.0 / (ROPE_BASE ** jnp.linspace(0.0, 1.0, HEAD_DIM // 2, dtype=jnp.float32))
    ang = pos.astype(jnp.float32)[:, None] * inv[None, :]
    cos = jnp.cos(ang)[None, :, None, :]
    sin = jnp.sin(ang)[None, :, None, :]
    xf = x.astype(jnp.float32).reshape(x.shape[:-1] + (HEAD_DIM // 2, 2))
    x1, x2 = xf[..., 0], xf[..., 1]
    return jnp.stack([x1 * cos - x2 * sin, x1 * sin + x2 * cos], axis=-1).reshape(x.shape)


def retention(q, k, v, g):
    B, S = q.shape[0], q.shape[1]
    C = RET_CHUNK
    N = S // C
    H, d = RET_HEADS, HEAD_DIM
    pos = jnp.arange(S)
    qf = rotary(q, pos)
    kf = rotary(k, pos) * (d ** -0.5)
    vf = v.astype(jnp.float32)

    lg = jnp.log(1.0 - 2.0 ** (-5.0 - jnp.arange(H, dtype=jnp.float32)))
    idx = jnp.arange(C, dtype=jnp.float32)
    diff = idx[:, None] - idx[None, :]
    dmask = jnp.where(diff >= 0, jnp.exp(lg[:, None, None] * jnp.maximum(diff, 0.0)), 0.0)

    qc = qf.reshape(B, N, C, H, d)
    kc = kf.reshape(B, N, C, H, d)
    vc = vf.reshape(B, N, C, H, d)

    inner = jnp.einsum('bnihd,bnjhd->bnhij', qc, kc) * dmask
    inner = jnp.einsum('bnhij,bnjhe->bnihe', inner, vc)

    zeta = jnp.exp(lg[:, None] * (C - 1.0 - idx)[None, :])
    U = jnp.einsum('bnjhd,bnjhe,hj->nbhde', kc, vc, zeta)
    g_chunk = jnp.exp(lg * C)[None, :, None, None]

    def step(R, u):
        return R * g_chunk + u, R

    _, R_prev = lax.scan(step, jnp.zeros((B, H, d, d), jnp.float32), U)
    xi = jnp.exp(lg[None, :] * (idx[:, None] + 1.0))
    cross = jnp.einsum('bnihd,nbhde->bnihe', qc, R_prev) * xi[None, None, :, :, None]

    o = inner + cross
    o = o * lax.rsqrt(jnp.mean(o * o, axis=-1, keepdims=True) + RMS_EPS)
    o = o.reshape(B, S, H, d)
    out = jax.nn.silu(g.astype(jnp.float32)) * o
    return out.reshape(B, S, RET_W).astype(q.dtype)


def pool_mixer(u, pool_w, pool_scale):
    B, S = u.shape[0], u.shape[1]
    uf = u.astype(jnp.float32)
    csp = jnp.pad(jnp.cumsum(uf, axis=1), ((0, 0), (1, 0), (0, 0)))
    t = jnp.arange(S)
    groups = []
    for gi, w in enumerate(POOL_WINDOWS):
        sl = slice(gi * POOL_GROUP_WIDTH, (gi + 1) * POOL_GROUP_WIDTH)
        lo = jnp.maximum(t + 1 - w, 0)
        win_sum = csp[:, 1:, sl] - csp[:, lo, sl]
        cnt = (t + 1 - lo).astype(jnp.float32)[None, :, None]
        groups.append(win_sum / cnt - uf[..., sl])
    pooled = jnp.stack(groups, axis=2)
    mixed = jnp.einsum('bsgc,gcd->bsgd', pooled, pool_w.astype(jnp.float32))
    mixed = mixed.reshape(B, S, POOL_WIDTH) * pool_scale.astype(jnp.float32)
    return mixed.astype(u.dtype)


def mixer_sublayer(x, norm_g, w_in, sinks, rel_bias, pool_w, pool_scale, w_out):
    B, S = x.shape[0], x.shape[1]
    h = rms_norm(x, norm_g)
    z = h @ w_in
    qa, ka, va, qr, kr, vr, gr, up = jnp.split(z, SPLIT_POINTS, axis=-1)
    ya = sliding_window_attention(
        qa.reshape(B, S, ATT_Q_HEADS, HEAD_DIM),
        ka.reshape(B, S, ATT_KV_HEADS, HEAD_DIM),
        va.reshape(B, S, ATT_KV_HEADS, HEAD_DIM), sinks, rel_bias)
    yr = retention(qr.reshape(B, S, RET_HEADS, HEAD_DIM),
                   kr.reshape(B, S, RET_HEADS, HEAD_DIM),
                   vr.reshape(B, S, RET_HEADS, HEAD_DIM),
                   gr.reshape(B, S, RET_HEADS, HEAD_DIM))
    yp = pool_mixer(up, pool_w, pool_scale)
    y = jnp.concatenate([ya, yr, yp], axis=-1)
    return x + y @ w_out


def mlp_sublayer(x, norm_g, w_up, w_down):
    h = rms_norm(x, norm_g)
    return x + jnp.square(jax.nn.relu(h @ w_up)) @ w_down


def setup_inputs(seed: int = 0) -> dict:
    key = jax.random.key(seed)
    ks = jax.random.split(key, 12)
    f32 = jnp.float32
    nrm = lambda k, s: jax.random.normal(k, s, dtype=f32)
    return {
        "x": nrm(ks[0], (BATCH, SEQ, D_MODEL)),
        "attn_norm_g": 1.0 + 0.1 * nrm(ks[1], (DEPTH, D_MODEL)),
        "w_in": nrm(ks[2], (DEPTH, D_MODEL, IN_WIDTH)) * D_MODEL ** -0.5,
        "attn_sinks": 0.5 * nrm(ks[3], (DEPTH, ATT_Q_HEADS)),
        "rel_bias": 0.5 * nrm(ks[4], (N_BUCKETS, ATT_Q_HEADS)),
        "pool_w": nrm(ks[5], (DEPTH, POOL_GROUPS, POOL_GROUP_WIDTH, POOL_GROUP_WIDTH)) * POOL_GROUP_WIDTH ** -0.5,
        "pool_scale": 1.0 + 0.1 * nrm(ks[6], (DEPTH, POOL_WIDTH)),
        "w_out": nrm(ks[7], (DEPTH, MIX_WIDTH, D_MODEL)) * MIX_WIDTH ** -0.5,
        "mlp_norm_g": 1.0 + 0.1 * nrm(ks[8], (DEPTH, D_MODEL)),
        "w_up": nrm(ks[9], (DEPTH, D_MODEL, D_FF)) * D_MODEL ** -0.5,
        "w_down": nrm(ks[10], (DEPTH, D_FF, D_MODEL)) * D_FF ** -0.5,
        "final_norm_g": 1.0 + 0.1 * nrm(ks[11], (D_MODEL,)),
    }


def reference(x, attn_norm_g, w_in, attn_sinks, rel_bias, pool_w, pool_scale, w_out,
              mlp_norm_g, w_up, w_down, final_norm_g):
    for layer in range(DEPTH):
        x = mixer_sublayer(x, attn_norm_g[layer], w_in[layer], attn_sinks[layer], rel_bias,
                           pool_w[layer], pool_scale[layer], w_out[layer])
        x = mlp_sublayer(x, mlp_norm_g[layer], w_up[layer], w_down[layer])
    return rms_norm(x, final_norm_g)
```

```python
import functools

import jax
import jax.numpy as jnp
import numpy as np
from jax import lax
from jax.experimental import pallas as pl
from jax.experimental.pallas import tpu as pltpu

D_MODEL = 1024
HEAD_DIM = 64
ATT_Q_HEADS = 6
ATT_KV_HEADS = 2
ATT_GROUP = ATT_Q_HEADS // ATT_KV_HEADS
RET_HEADS = 6
POOL_WINDOWS = (2, 4, 8, 16)
POOL_GROUP_WIDTH = 64
POOL_WIDTH = 256
BLOCK = 128
N_BUCKETS = 32
MAX_DISTANCE = 128
D_FF = 4 * D_MODEL
RMS_EPS = 1e-6
ROPE_BASE = 10000.0
NEG_INF = -1e30

LANES = 128
HALF = LANES // 2
ATT_Q_W = ATT_Q_HEADS * HEAD_DIM
ATT_KV_W = ATT_KV_HEADS * HEAD_DIM
RET_W = RET_HEADS * HEAD_DIM
RET_SLABS = RET_W // LANES
ATT_SLABS = ATT_Q_W // LANES
POOL_TAIL = 16

C_QA = 0
C_KA = C_QA + ATT_Q_W
C_VA = C_KA + ATT_KV_W
C_QR = C_VA + ATT_KV_W
C_KR = C_QR + RET_W
C_VR = C_KR + RET_W
C_GR = C_VR + RET_W
C_UP = C_GR + RET_W
C_QS = C_UP + POOL_WIDTH
C_KS = C_QS + RET_W
Z_WIDTH = C_KS + RET_W

Y_A = 0
Y_R = ATT_Q_W
Y_P = ATT_Q_W + RET_W

MIXER_TILE = 256
MLP_TILE = 512
FF_CHUNK = 1024
VMEM_LIMIT_BYTES = 56 * 1024 * 1024


def _rms_norm(x, g):
    ms = jnp.mean(x * x, axis=-1, keepdims=True)
    return x * lax.rsqrt(ms + RMS_EPS) * g


def _t5_bucket(dist):
    max_exact = N_BUCKETS // 2
    n = np.maximum(dist, 0)
    large = max_exact + (np.log(np.maximum(n, 1) / max_exact)
                         / np.log(MAX_DISTANCE / max_exact)
                         * (N_BUCKETS - max_exact)).astype(np.int64)
    large = np.minimum(large, N_BUCKETS - 1)
    return np.where(n < max_exact, n, large).astype(np.int32)


def _mixer_kernel(sinks_ref, x_ref, g_ref, win_ref, bias_ref, cos_ref, sin_ref, dm_ref, zeta_ref,
                  xi_ref, gtab_ref, wpool_ref, pscale_ref, wout_ref, o_ref,
                  z_ref, y_ref, kbuf, vbuf, ubuf, r_ref, *, tile):
    t = pl.program_id(1)
    nblk = tile // BLOCK
    f32, bf16 = jnp.float32, jnp.bfloat16

    @pl.when(t == 0)
    def _():
        kbuf[0:BLOCK, :] = jnp.zeros((BLOCK, ATT_KV_W), bf16)
        vbuf[0:BLOCK, :] = jnp.zeros((BLOCK, ATT_KV_W), bf16)
        ubuf[0:POOL_TAIL, :] = jnp.zeros((POOL_TAIL, POOL_WIDTH), f32)
        r_ref[...] = jnp.zeros(r_ref.shape, f32)

    x = x_ref[0]
    h = _rms_norm(x, g_ref[...]).astype(bf16)
    z_ref[...] = jnp.dot(h, win_ref[...], preferred_element_type=f32)

    lane = lax.broadcasted_iota(jnp.int32, (BLOCK, LANES), 1)
    lo = lane < HALF

    kbuf[BLOCK:BLOCK + tile, :] = z_ref[:, C_KA:C_KA + ATT_KV_W].astype(bf16)
    vbuf[BLOCK:BLOCK + tile, :] = z_ref[:, C_VA:C_VA + ATT_KV_W].astype(bf16)
    for n in range(nblk):
        r0 = n * BLOCK
        qs = [z_ref[r0:r0 + BLOCK, C_QA + j * LANES:C_QA + (j + 1) * LANES] for j in range(ATT_SLABS)]
        parts = [jnp.where(lo, q, 0.0) for q in qs] + [jnp.where(lo, 0.0, q) for q in qs]
        qstack = jnp.concatenate(parts, axis=0).astype(bf16)
        kk = kbuf[r0:r0 + 2 * BLOCK, :]
        vv = vbuf[r0:r0 + 2 * BLOCK, :]
        s = lax.dot_general(qstack, kk, (((1,), (1,)), ((), ())), preferred_element_type=f32)
        if n == 0:
            first = jnp.where(t == 0, 1, 0)
            s = s + bias_ref[first]
        else:
            s = s + bias_ref[0]
        ps, denoms = [], []
        for hd in range(ATT_Q_HEADS):
            sh = s[hd * BLOCK:(hd + 1) * BLOCK]
            sink = sinks_ref[hd]
            m = jnp.maximum(jnp.max(sh, axis=-1, keepdims=True), sink)
            p = jnp.exp(sh - m)
            denoms.append(jnp.sum(p, axis=-1, keepdims=True) + jnp.exp(sink - m))
            ps.append(p.astype(bf16))
        pstack = jnp.concatenate(ps, axis=0)
        o = jnp.dot(pstack, vv, preferred_element_type=f32)
        for j in range(ATT_SLABS):
            o_lo = o[j * BLOCK:(j + 1) * BLOCK] / denoms[j]
            o_hi = o[(j + ATT_GROUP) * BLOCK:(j + ATT_GROUP + 1) * BLOCK] / denoms[j + ATT_GROUP]
            y_ref[r0:r0 + BLOCK, Y_A + j * LANES:Y_A + (j + 1) * LANES] = jnp.where(lo, o_lo, o_hi).astype(bf16)

    for n in range(nblk):
        r0 = n * BLOCK
        cos = cos_ref[r0:r0 + BLOCK, :]
        sin = sin_ref[r0:r0 + BLOCK, :]
        for j in range(RET_SLABS):
            c0 = j * LANES
            q = (z_ref[r0:r0 + BLOCK, C_QR + c0:C_QR + c0 + LANES] * cos
                 + z_ref[r0:r0 + BLOCK, C_QS + c0:C_QS + c0 + LANES] * sin)
            k = (z_ref[r0:r0 + BLOCK, C_KR + c0:C_KR + c0 + LANES] * cos
                 + z_ref[r0:r0 + BLOCK, C_KS + c0:C_KS + c0 + LANES] * sin)
            v = z_ref[r0:r0 + BLOCK, C_VR + c0:C_VR + c0 + LANES]
            gate = z_ref[r0:r0 + BLOCK, C_GR + c0:C_GR + c0 + LANES]
            qb = q.astype(bf16)
            kb = k.astype(bf16)
            qsplit = jnp.concatenate([jnp.where(lo, q, 0.0), jnp.where(lo, 0.0, q)], axis=0).astype(bf16)
            sc = lax.dot_general(qsplit, kb, (((1,), (1,)), ((), ())), preferred_element_type=f32)
            a = sc * dm_ref[j]
            acat = jnp.concatenate([a[:BLOCK], a[BLOCK:]], axis=1).astype(bf16)
            vsplit = jnp.concatenate([jnp.where(lo, v, 0.0), jnp.where(lo, 0.0, v)], axis=0).astype(bf16)
            inner = jnp.dot(acat, vsplit, preferred_element_type=f32)
            r_prev = r_ref[j]
            cross = jnp.dot(qb, r_prev.astype(bf16), preferred_element_type=f32) * xi_ref[j]
            o = inner + cross
            vz = (v * zeta_ref[j]).astype(bf16)
            u = lax.dot_general(kb, vz, (((0,), (0,)), ((), ())), preferred_element_type=f32)
            gt = gtab_ref[j]
            r_ref[j] = r_prev * gt + jnp.where(gt > 0.0, u, 0.0)
            o2 = o * o
            ms_lo = jnp.sum(jnp.where(lo, o2, 0.0), axis=-1, keepdims=True)
            ms_hi = jnp.sum(jnp.where(lo, 0.0, o2), axis=-1, keepdims=True)
            ms = jnp.where(lo, ms_lo, ms_hi) * (1.0 / HEAD_DIM)
            on = o * lax.rsqrt(ms + RMS_EPS)
            y_ref[r0:r0 + BLOCK, Y_R + c0:Y_R + c0 + LANES] = (jax.nn.silu(gate) * on).astype(bf16)

    ubuf[POOL_TAIL:POOL_TAIL + tile, :] = z_ref[:, C_UP:C_UP + POOL_WIDTH]
    lane_t = lax.broadcasted_iota(jnp.int32, (tile, LANES), 1)
    lo_t = lane_t < HALF
    pos1 = lax.broadcasted_iota(jnp.int32, (tile, LANES), 0) + (t * tile + 1)
    pooled = []
    for sl in range(POOL_WIDTH // LANES):
        w_small, w_big = POOL_WINDOWS[2 * sl], POOL_WINDOWS[2 * sl + 1]
        cols = slice(sl * LANES, (sl + 1) * LANES)

        def shifted(d, cols=cols):
            return ubuf[POOL_TAIL - d:POOL_TAIL - d + tile, cols]

        u0 = shifted(0)
        acc = u0
        for d in range(1, w_small):
            acc = acc + shifted(d)
        small = acc
        for d in range(w_small, w_big):
            acc = acc + shifted(d)
        win = jnp.where(lo_t, small, acc)
        cnt = jnp.minimum(pos1, jnp.where(lo_t, w_small, w_big)).astype(f32)
        pooled.append((win / cnt - u0).astype(bf16))
    pooled = jnp.concatenate(pooled, axis=1)
    mixed = jnp.dot(pooled, wpool_ref[...], preferred_element_type=f32) * pscale_ref[...]
    y_ref[:, Y_P:Y_P + POOL_WIDTH] = mixed.astype(bf16)

    o_ref[0] = x + jnp.dot(y_ref[...], wout_ref[...], preferred_element_type=f32)

    kbuf[0:BLOCK, :] = kbuf[tile:tile + BLOCK, :]
    vbuf[0:BLOCK, :] = vbuf[tile:tile + BLOCK, :]
    ubuf[0:POOL_TAIL, :] = ubuf[tile:tile + POOL_TAIL, :]


def _const_spec(shape):
    nd = len(shape)
    return pl.BlockSpec(shape, lambda b, t, _nd=nd: (0,) * _nd)


def _mixer_call(x, sinks, g, win, bias, cos, sin, dm, zeta, xi, gtab, wpool, pscale, wout):
    batch, seq, _ = x.shape
    tile = MIXER_TILE
    grid = (batch, seq // tile)
    f32, bf16 = jnp.float32, jnp.bfloat16
    in_specs = [
        pl.BlockSpec(memory_space=pltpu.SMEM),
        pl.BlockSpec((1, tile, D_MODEL), lambda b, t: (b, t, 0)),
        _const_spec(g.shape), _const_spec(win.shape), _const_spec(bias.shape),
        pl.BlockSpec((tile, LANES), lambda b, t: (t, 0)),
        pl.BlockSpec((tile, LANES), lambda b, t: (t, 0)),
        _const_spec(dm.shape), _const_spec(zeta.shape), _const_spec(xi.shape), _const_spec(gtab.shape),
        _const_spec(wpool.shape), _const_spec(pscale.shape), _const_spec(wout.shape),
    ]
    return pl.pallas_call(
        functools.partial(_mixer_kernel, tile=tile),
        grid=grid,
        in_specs=in_specs,
        out_specs=pl.BlockSpec((1, tile, D_MODEL), lambda b, t: (b, t, 0)),
        out_shape=jax.ShapeDtypeStruct(x.shape, x.dtype),
        scratch_shapes=[
            pltpu.VMEM((tile, Z_WIDTH), f32),
            pltpu.VMEM((tile, D_MODEL), bf16),
            pltpu.VMEM((BLOCK + tile, ATT_KV_W), bf16),
            pltpu.VMEM((BLOCK + tile, ATT_KV_W), bf16),
            pltpu.VMEM((POOL_TAIL + tile, POOL_WIDTH), f32),
            pltpu.VMEM((RET_SLABS, LANES, LANES), f32),
        ],
        compiler_params=pltpu.CompilerParams(
            dimension_semantics=("arbitrary", "arbitrary"),
            vmem_limit_bytes=VMEM_LIMIT_BYTES),
        name="mixer",
    )(sinks, x, g, win, bias, cos, sin, dm, zeta, xi, gtab, wpool, pscale, wout)


def _mlp_kernel(x_ref, g_ref, wup_ref, wdown_ref, gf_ref, o_ref, *, final_norm):
    f32, bf16 = jnp.float32, jnp.bfloat16
    x = x_ref[0]
    h = _rms_norm(x, g_ref[...]).astype(bf16)
    acc = x
    for c in range(D_FF // FF_CHUNK):
        a = jnp.dot(h, wup_ref[:, c * FF_CHUNK:(c + 1) * FF_CHUNK], preferred_element_type=f32)
        a = jnp.square(jnp.maximum(a, 0.0)).astype(bf16)
        acc = acc + jnp.dot(a, wdown_ref[c * FF_CHUNK:(c + 1) * FF_CHUNK, :], preferred_element_type=f32)
    if final_norm:
        acc = _rms_norm(acc, gf_ref[...])
    o_ref[0] = acc


def _mlp_call(x, g, wup, wdown, gf, final_norm):
    batch, seq, _ = x.shape
    tile = MLP_TILE
    grid = (batch, seq // tile)
    return pl.pallas_call(
        functools.partial(_mlp_kernel, final_norm=final_norm),
        grid=grid,
        in_specs=[
            pl.BlockSpec((1, tile, D_MODEL), lambda b, t: (b, t, 0)),
            _const_spec(g.shape), _const_spec(wup.shape), _const_spec(wdown.shape), _const_spec(gf.shape),
        ],
        out_specs=pl.BlockSpec((1, tile, D_MODEL), lambda b, t: (b, t, 0)),
        out_shape=jax.ShapeDtypeStruct(x.shape, x.dtype),
        compiler_params=pltpu.CompilerParams(
            dimension_semantics=("arbitrary", "arbitrary"),
            vmem_limit_bytes=VMEM_LIMIT_BYTES),
        name="mlp",
    )(x, g, wup, wdown, gf)


def _att_head_perm():
    idx = []
    for j in range(ATT_SLABS):
        idx += list(range(j * HEAD_DIM, (j + 1) * HEAD_DIM))
        idx += list(range((j + ATT_GROUP) * HEAD_DIM, (j + ATT_GROUP + 1) * HEAD_DIM))
    return np.asarray(idx, np.int32)


def _extend_w_in(w_in):
    perm = _att_head_perm()
    qa = w_in[..., 0:ATT_Q_W][..., perm] * (HEAD_DIM ** -0.5)
    rest = w_in[..., ATT_Q_W:]
    o_qr = C_QR - ATT_Q_W
    o_kr = C_KR - ATT_Q_W

    def partner(w):
        pairs = w.reshape(w.shape[:-1] + (RET_W // 2, 2))
        return jnp.stack([-pairs[..., 1], pairs[..., 0]], axis=-1).reshape(w.shape)

    qs = partner(rest[..., o_qr:o_qr + RET_W])
    ks = partner(rest[..., o_kr:o_kr + RET_W])
    return jnp.concatenate([qa, rest, qs, ks], axis=-1).astype(jnp.bfloat16)


def _attention_bias(rel_bias):
    i = np.arange(BLOCK)[:, None]
    j = np.arange(2 * BLOCK)[None, :]
    dist = BLOCK + i - j
    band = (dist >= 0) & (dist < BLOCK)
    bias = rel_bias.astype(jnp.float32)[_t5_bucket(dist)]
    bias = jnp.transpose(bias, (2, 0, 1))
    normal = jnp.where(jnp.asarray(band)[None], bias, NEG_INF)
    first = jnp.where(jnp.asarray(band & (j >= BLOCK))[None], bias, NEG_INF)
    return jnp.stack([normal, first]).reshape(2, ATT_Q_HEADS * BLOCK, 2 * BLOCK)


def _rotary_tables(seq):
    inv = 1.0 / (ROPE_BASE ** jnp.linspace(0.0, 1.0, HEAD_DIM // 2, dtype=jnp.float32))
    ang = jnp.arange(seq).astype(jnp.float32)[:, None] * inv[None, :]
    cos = jnp.tile(jnp.repeat(jnp.cos(ang), 2, axis=-1), (1, LANES // HEAD_DIM))
    sin = jnp.tile(jnp.repeat(jnp.sin(ang), 2, axis=-1), (1, LANES // HEAD_DIM))
    return cos, sin


def _retention_tables():
    c = BLOCK
    scale = HEAD_DIM ** -0.5
    lg = jnp.log(1.0 - 2.0 ** (-5.0 - jnp.arange(RET_HEADS, dtype=jnp.float32)))
    idx = jnp.arange(c, dtype=jnp.float32)
    diff = idx[:, None] - idx[None, :]
    dmask = jnp.where(diff >= 0, jnp.exp(lg[:, None, None] * jnp.maximum(diff, 0.0)), 0.0)
    dm = (dmask * scale).reshape(RET_SLABS, 2 * c, c)
    zeta = jnp.exp(lg[:, None] * (c - 1.0 - idx)[None, :]) * scale
    xi = jnp.exp(lg[None, :] * (idx[:, None] + 1.0))
    g_chunk = jnp.exp(lg * c)
    head_of_lane = (np.arange(LANES) >= HALF).astype(np.int32)
    zeta_t = jnp.stack([zeta[2 * j + head_of_lane].T for j in range(RET_SLABS)])
    xi_t = jnp.stack([xi[:, 2 * j + head_of_lane] for j in range(RET_SLABS)])
    same = head_of_lane[:, None] == head_of_lane[None, :]
    gtab = jnp.stack([jnp.where(same, g_chunk[2 * j + head_of_lane][:, None], 0.0)
                      for j in range(RET_SLABS)])
    return dm, zeta_t, xi_t, gtab


def _pool_block_diag(pool_w):
    depth, groups = pool_w.shape[0], pool_w.shape[1]
    out = jnp.zeros((depth, POOL_WIDTH, POOL_WIDTH), pool_w.dtype)
    for gi in range(groups):
        sl = slice(gi * POOL_GROUP_WIDTH, (gi + 1) * POOL_GROUP_WIDTH)
        out = out.at[:, sl, sl].set(pool_w[:, gi])
    return out.astype(jnp.bfloat16)


def kernel(x, attn_norm_g, w_in, attn_sinks, rel_bias, pool_w, pool_scale, w_out, mlp_norm_g, w_up, w_down,
           final_norm_g):
    depth = w_in.shape[0]
    seq = x.shape[1]
    bf16 = jnp.bfloat16
    w_ext = _extend_w_in(w_in)
    perm = _att_head_perm()
    w_out_p = jnp.concatenate([w_out[:, 0:ATT_Q_W][:, perm], w_out[:, ATT_Q_W:]], axis=1).astype(bf16)
    w_pool = _pool_block_diag(pool_w)
    w_up_b = w_up.astype(bf16)
    w_down_b = w_down.astype(bf16)
    bias = _attention_bias(rel_bias)
    cos, sin = _rotary_tables(seq)
    dm, zeta_t, xi_t, gtab = _retention_tables()
    gf = final_norm_g.reshape(1, D_MODEL)
    for layer in range(depth):
        x = _mixer_call(x, attn_sinks[layer], attn_norm_g[layer].reshape(1, D_MODEL), w_ext[layer], bias,
                        cos, sin, dm, zeta_t, xi_t, gtab, w_pool[layer],
                        pool_scale[layer].reshape(1, POOL_WIDTH), w_out_p[layer])
        x = _mlp_call(x, mlp_norm_g[layer].reshape(1, D_MODEL), w_up_b[layer], w_down_b[layer], gf,
                      final_norm=(layer == depth - 1))
    return x
```

```python
import functools

import jax
import jax.numpy as jnp
import numpy as np
from jax import lax
from jax.experimental import pallas as pl
from jax.experimental.pallas import tpu as pltpu

D_MODEL = 1024
HEAD_DIM = 64
ATT_Q_HEADS = 6
ATT_KV_HEADS = 2
ATT_GROUP = ATT_Q_HEADS // ATT_KV_HEADS
RET_HEADS = 6
POOL_WINDOWS = (2, 4, 8, 16)
POOL_GROUP_WIDTH = 64
POOL_WIDTH = 256
BLOCK = 128
N_BUCKETS = 32
MAX_DISTANCE = 128
D_FF = 4 * D_MODEL
RMS_EPS = 1e-6
ROPE_BASE = 10000.0
NEG_INF = -1e30

LANES = 128
HALF = LANES // 2
ATT_Q_W = ATT_Q_HEADS * HEAD_DIM
ATT_KV_W = ATT_KV_HEADS * HEAD_DIM
RET_W = RET_HEADS * HEAD_DIM
RET_SLABS = RET_W // LANES
ATT_SLABS = ATT_Q_W // LANES
POOL_TAIL = 16

C_QA = 0
C_KA = C_QA + ATT_Q_W
C_VA = C_KA + ATT_KV_W
C_QR = C_VA + ATT_KV_W
C_KR = C_QR + RET_W
C_VR = C_KR + RET_W
C_GR = C_VR + RET_W
C_UP = C_GR + RET_W
Z_WIDTH = C_UP + POOL_WIDTH

Y_A = 0
Y_R = ATT_Q_W
Y_P = ATT_Q_W + RET_W

MIXER_TILE = 256
MLP_TILE = 512
FF_CHUNK = 1024
VMEM_LIMIT_BYTES = 56 * 1024 * 1024


def _rms_norm(x, g):
    ms = jnp.mean(x * x, axis=-1, keepdims=True)
    return x * lax.rsqrt(ms + RMS_EPS) * g


def _t5_bucket(dist):
    max_exact = N_BUCKETS // 2
    n = np.maximum(dist, 0)
    large = max_exact + (np.log(np.maximum(n, 1) / max_exact)
                         / np.log(MAX_DISTANCE / max_exact)
                         * (N_BUCKETS - max_exact)).astype(np.int64)
    large = np.minimum(large, N_BUCKETS - 1)
    return np.where(n < max_exact, n, large).astype(np.int32)


def _mixer_kernel(sinks_ref, x_ref, g_ref, win_ref, bias_ref, cos_ref, sina_ref, sinb_ref, dm_ref, zeta_ref,
                  xi_ref, gtab_ref, wpool_ref, pscale_ref, wout_ref, o_ref,
                  z_ref, y_ref, kbuf, vbuf, ubuf, r_ref, *, tile, layer):
    t = pl.program_id(1)
    nblk = tile // BLOCK
    f32, bf16 = jnp.float32, jnp.bfloat16

    @pl.when(t == 0)
    def _():
        kbuf[0:BLOCK, :] = jnp.zeros((BLOCK, ATT_KV_W), bf16)
        vbuf[0:BLOCK, :] = jnp.zeros((BLOCK, ATT_KV_W), bf16)
        ubuf[0:POOL_TAIL, :] = jnp.zeros((POOL_TAIL, POOL_WIDTH), f32)
        r_ref[...] = jnp.zeros(r_ref.shape, f32)

    x = x_ref[0]
    h = _rms_norm(x, g_ref[...]).astype(bf16)
    z_ref[...] = jnp.dot(h, win_ref[...], preferred_element_type=f32)

    lane = lax.broadcasted_iota(jnp.int32, (BLOCK, LANES), 1)
    lo = lane < HALF

    kbuf[BLOCK:BLOCK + tile, :] = z_ref[:, C_KA:C_KA + ATT_KV_W].astype(bf16)
    vbuf[BLOCK:BLOCK + tile, :] = z_ref[:, C_VA:C_VA + ATT_KV_W].astype(bf16)
    for n in range(nblk):
        r0 = n * BLOCK
        qs = [z_ref[r0:r0 + BLOCK, C_QA + j * LANES:C_QA + (j + 1) * LANES] for j in range(ATT_SLABS)]
        parts = [jnp.where(lo, q, 0.0) for q in qs] + [jnp.where(lo, 0.0, q) for q in qs]
        qstack = jnp.concatenate(parts, axis=0).astype(bf16)
        kk = kbuf[r0:r0 + 2 * BLOCK, :]
        vv = vbuf[r0:r0 + 2 * BLOCK, :]
        s = lax.dot_general(qstack, kk, (((1,), (1,)), ((), ())), preferred_element_type=f32)
        if n == 0:
            first = jnp.where(t == 0, 1, 0)
            s = s + bias_ref[first]
        else:
            s = s + bias_ref[0]
        ps, denoms = [], []
        for hd in range(ATT_Q_HEADS):
            sh = s[hd * BLOCK:(hd + 1) * BLOCK]
            sink = sinks_ref[layer, hd]
            m = jnp.maximum(jnp.max(sh, axis=-1, keepdims=True), sink)
            p = jnp.exp(sh - m)
            denoms.append(jnp.sum(p, axis=-1, keepdims=True) + jnp.exp(sink - m))
            ps.append(p.astype(bf16))
        pstack = jnp.concatenate(ps, axis=0)
        o = jnp.dot(pstack, vv, preferred_element_type=f32)
        for j in range(ATT_SLABS):
            o_lo = o[j * BLOCK:(j + 1) * BLOCK] / denoms[j]
            o_hi = o[(j + ATT_GROUP) * BLOCK:(j + ATT_GROUP + 1) * BLOCK] / denoms[j + ATT_GROUP]
            y_ref[r0:r0 + BLOCK, Y_A + j * LANES:Y_A + (j + 1) * LANES] = jnp.where(lo, o_lo, o_hi).astype(bf16)

    def rotary(v, cos, sin_a, sin_b):
        return v * cos + pltpu.roll(v, LANES - 1, 1) * sin_a + pltpu.roll(v, 1, 1) * sin_b

    for n in range(nblk):
        r0 = n * BLOCK
        cos = cos_ref[r0:r0 + BLOCK, :]
        sin_a = sina_ref[r0:r0 + BLOCK, :]
        sin_b = sinb_ref[r0:r0 + BLOCK, :]
        for j in range(RET_SLABS):
            c0 = j * LANES
            q = rotary(z_ref[r0:r0 + BLOCK, C_QR + c0:C_QR + c0 + LANES], cos, sin_a, sin_b)
            k = rotary(z_ref[r0:r0 + BLOCK, C_KR + c0:C_KR + c0 + LANES], cos, sin_a, sin_b)
            v = z_ref[r0:r0 + BLOCK, C_VR + c0:C_VR + c0 + LANES]
            gate = z_ref[r0:r0 + BLOCK, C_GR + c0:C_GR + c0 + LANES]
            qb = q.astype(bf16)
            kb = k.astype(bf16)
            qsplit = jnp.concatenate([jnp.where(lo, q, 0.0), jnp.where(lo, 0.0, q)], axis=0).astype(bf16)
            sc = lax.dot_general(qsplit, kb, (((1,), (1,)), ((), ())), preferred_element_type=f32)
            a = sc * dm_ref[j]
            acat = jnp.concatenate([a[:BLOCK], a[BLOCK:]], axis=1).astype(bf16)
            vsplit = jnp.concatenate([jnp.where(lo, v, 0.0), jnp.where(lo, 0.0, v)], axis=0).astype(bf16)
            inner = jnp.dot(acat, vsplit, preferred_element_type=f32)
            r_prev = r_ref[j]
            cross = jnp.dot(qb, r_prev.astype(bf16), preferred_element_type=f32) * xi_ref[j]
            o = inner + cross
            vz = (v * zeta_ref[j]).astype(bf16)
            u = lax.dot_general(kb, vz, (((0,), (0,)), ((), ())), preferred_element_type=f32)
            gt = gtab_ref[j]
            r_ref[j] = r_prev * gt + jnp.where(gt > 0.0, u, 0.0)
            o2 = o * o
            ms_lo = jnp.sum(jnp.where(lo, o2, 0.0), axis=-1, keepdims=True)
            ms_hi = jnp.sum(jnp.where(lo, 0.0, o2), axis=-1, keepdims=True)
            ms = jnp.where(lo, ms_lo, ms_hi) * (1.0 / HEAD_DIM)
            on = o * lax.rsqrt(ms + RMS_EPS)
            y_ref[r0:r0 + BLOCK, Y_R + c0:Y_R + c0 + LANES] = (jax.nn.silu(gate) * on).astype(bf16)

    ubuf[POOL_TAIL:POOL_TAIL + tile, :] = z_ref[:, C_UP:C_UP + POOL_WIDTH]
    lane_t = lax.broadcasted_iota(jnp.int32, (tile, LANES), 1)
    lo_t = lane_t < HALF
    pos1 = lax.broadcasted_iota(jnp.int32, (tile, LANES), 0) + (t * tile + 1)
    pooled = []
    for sl in range(POOL_WIDTH // LANES):
        w_small, w_big = POOL_WINDOWS[2 * sl], POOL_WINDOWS[2 * sl + 1]
        cols = slice(sl * LANES, (sl + 1) * LANES)

        def shifted(d, cols=cols):
            return ubuf[POOL_TAIL - d:POOL_TAIL - d + tile, cols]

        u0 = shifted(0)
        acc = u0
        for d in range(1, w_small):
            acc = acc + shifted(d)
        small = acc
        for d in range(w_small, w_big):
            acc = acc + shifted(d)
        win = jnp.where(lo_t, small, acc)
        cnt = jnp.minimum(pos1, jnp.where(lo_t, w_small, w_big)).astype(f32)
        pooled.append((win / cnt - u0).astype(bf16))
    pooled = jnp.concatenate(pooled, axis=1)
    mixed = jnp.dot(pooled, wpool_ref[...], preferred_element_type=f32) * pscale_ref[...]
    y_ref[:, Y_P:Y_P + POOL_WIDTH] = mixed.astype(bf16)

    o_ref[0] = x + jnp.dot(y_ref[...], wout_ref[...], preferred_element_type=f32)

    kbuf[0:BLOCK, :] = kbuf[tile:tile + BLOCK, :]
    vbuf[0:BLOCK, :] = vbuf[tile:tile + BLOCK, :]
    ubuf[0:POOL_TAIL, :] = ubuf[tile:tile + POOL_TAIL, :]


def _const_spec(shape):
    nd = len(shape)
    return pl.BlockSpec(shape, lambda b, t, _nd=nd: (0,) * _nd)


def _layer_spec(shape, layer):
    nd = len(shape)
    return pl.BlockSpec((None,) + tuple(shape[1:]), lambda b, t, _nd=nd: (layer,) + (0,) * (_nd - 1))


def _mixer_call(layer, x, sinks, g, win, bias, cos, sin_a, sin_b, dm, zeta, xi, gtab, wpool, pscale, wout):
    batch, seq, _ = x.shape
    tile = MIXER_TILE
    grid = (batch, seq // tile)
    f32, bf16 = jnp.float32, jnp.bfloat16
    rot_spec = pl.BlockSpec((tile, LANES), lambda b, t: (t, 0))
    in_specs = [
        pl.BlockSpec(memory_space=pltpu.SMEM),
        pl.BlockSpec((1, tile, D_MODEL), lambda b, t: (b, t, 0)),
        _layer_spec(g.shape, layer), _layer_spec(win.shape, layer), _const_spec(bias.shape),
        rot_spec, rot_spec, rot_spec,
        _const_spec(dm.shape), _const_spec(zeta.shape), _const_spec(xi.shape), _const_spec(gtab.shape),
        _layer_spec(wpool.shape, layer), _layer_spec(pscale.shape, layer), _layer_spec(wout.shape, layer),
    ]
    return pl.pallas_call(
        functools.partial(_mixer_kernel, tile=tile, layer=layer),
        grid=grid,
        in_specs=in_specs,
        out_specs=pl.BlockSpec((1, tile, D_MODEL), lambda b, t: (b, t, 0)),
        out_shape=jax.ShapeDtypeStruct(x.shape, x.dtype),
        scratch_shapes=[
            pltpu.VMEM((tile, Z_WIDTH), f32),
            pltpu.VMEM((tile, D_MODEL), bf16),
            pltpu.VMEM((BLOCK + tile, ATT_KV_W), bf16),
            pltpu.VMEM((BLOCK + tile, ATT_KV_W), bf16),
            pltpu.VMEM((POOL_TAIL + tile, POOL_WIDTH), f32),
            pltpu.VMEM((RET_SLABS, LANES, LANES), f32),
        ],
        compiler_params=pltpu.CompilerParams(
            dimension_semantics=("arbitrary", "arbitrary"),
            vmem_limit_bytes=VMEM_LIMIT_BYTES),
        name="mixer",
    )(sinks, x, g, win, bias, cos, sin_a, sin_b, dm, zeta, xi, gtab, wpool, pscale, wout)


def _mlp_kernel(x_ref, g_ref, wup_ref, wdown_ref, gf_ref, o_ref, *, final_norm):
    f32, bf16 = jnp.float32, jnp.bfloat16
    x = x_ref[0]
    h = _rms_norm(x, g_ref[...]).astype(bf16)
    acc = x
    for c in range(D_FF // FF_CHUNK):
        a = jnp.dot(h, wup_ref[:, c * FF_CHUNK:(c + 1) * FF_CHUNK], preferred_element_type=f32)
        a = jnp.square(jnp.maximum(a, 0.0)).astype(bf16)
        acc = acc + jnp.dot(a, wdown_ref[c * FF_CHUNK:(c + 1) * FF_CHUNK, :], preferred_element_type=f32)
    if final_norm:
        acc = _rms_norm(acc, gf_ref[...])
    o_ref[0] = acc


def _mlp_call(layer, x, g, wup, wdown, gf, final_norm):
    batch, seq, _ = x.shape
    tile = MLP_TILE
    grid = (batch, seq // tile)
    return pl.pallas_call(
        functools.partial(_mlp_kernel, final_norm=final_norm),
        grid=grid,
        in_specs=[
            pl.BlockSpec((1, tile, D_MODEL), lambda b, t: (b, t, 0)),
            _layer_spec(g.shape, layer), _layer_spec(wup.shape, layer), _layer_spec(wdown.shape, layer),
            _const_spec(gf.shape),
        ],
        out_specs=pl.BlockSpec((1, tile, D_MODEL), lambda b, t: (b, t, 0)),
        out_shape=jax.ShapeDtypeStruct(x.shape, x.dtype),
        compiler_params=pltpu.CompilerParams(
            dimension_semantics=("arbitrary", "arbitrary"),
            vmem_limit_bytes=VMEM_LIMIT_BYTES),
        name="mlp",
    )(x, g, wup, wdown, gf)


def _att_slab_order(w, axis):
    heads = [lax.slice_in_dim(w, hd * HEAD_DIM, (hd + 1) * HEAD_DIM, axis=axis) for hd in range(ATT_Q_HEADS)]
    order = []
    for j in range(ATT_SLABS):
        order += [heads[j], heads[j + ATT_GROUP]]
    return jnp.concatenate(order, axis=axis)


def _prep_w_in(w_in):
    w = w_in.astype(jnp.bfloat16)
    qa = _att_slab_order(w[..., 0:ATT_Q_W], axis=2) * jnp.asarray(HEAD_DIM ** -0.5, jnp.bfloat16)
    return jnp.concatenate([qa, w[..., ATT_Q_W:]], axis=-1)


def _prep_w_out(w_out):
    w = w_out.astype(jnp.bfloat16)
    return jnp.concatenate([_att_slab_order(w[:, 0:ATT_Q_W], axis=1), w[:, ATT_Q_W:]], axis=1)


def _attention_bias(rel_bias):
    i = np.arange(BLOCK)[:, None]
    j = np.arange(2 * BLOCK)[None, :]
    dist = BLOCK + i - j
    band = (dist >= 0) & (dist < BLOCK)
    bias = rel_bias.astype(jnp.float32)[_t5_bucket(dist)]
    bias = jnp.transpose(bias, (2, 0, 1))
    normal = jnp.where(jnp.asarray(band)[None], bias, NEG_INF)
    first = jnp.where(jnp.asarray(band & (j >= BLOCK))[None], bias, NEG_INF)
    return jnp.stack([normal, first]).reshape(2, ATT_Q_HEADS * BLOCK, 2 * BLOCK)


def _rotary_tables(seq):
    inv = 1.0 / (ROPE_BASE ** jnp.linspace(0.0, 1.0, HEAD_DIM // 2, dtype=jnp.float32))
    ang = jnp.arange(seq).astype(jnp.float32)[:, None] * inv[None, :]
    reps = (1, LANES // HEAD_DIM)
    cos = jnp.tile(jnp.repeat(jnp.cos(ang), 2, axis=-1), reps)
    sin = jnp.tile(jnp.repeat(jnp.sin(ang), 2, axis=-1), reps)
    even = (np.arange(LANES) % 2 == 0)[None, :]
    return cos, jnp.where(even, -sin, 0.0), jnp.where(even, 0.0, sin)


def _retention_tables():
    c = BLOCK
    f32 = np.float32
    scale = f32(HEAD_DIM ** -0.5)
    lg = np.log(f32(1.0) - f32(2.0) ** (f32(-5.0) - np.arange(RET_HEADS, dtype=f32))).astype(f32)
    idx = np.arange(c, dtype=f32)
    diff = idx[:, None] - idx[None, :]
    dmask = np.where(diff >= 0, np.exp(lg[:, None, None] * np.maximum(diff, f32(0.0))), f32(0.0)).astype(f32)
    dm = (dmask * scale).reshape(RET_SLABS, 2 * c, c)
    zeta = (np.exp(lg[:, None] * (f32(c) - f32(1.0) - idx)[None, :]) * scale).astype(f32)
    xi = np.exp(lg[None, :] * (idx[:, None] + f32(1.0))).astype(f32)
    g_chunk = np.exp(lg * f32(c)).astype(f32)
    head_of_lane = (np.arange(LANES) >= HALF).astype(np.int32)
    zeta_t = np.stack([zeta[2 * j + head_of_lane].T for j in range(RET_SLABS)])
    xi_t = np.stack([xi[:, 2 * j + head_of_lane] for j in range(RET_SLABS)])
    same = head_of_lane[:, None] == head_of_lane[None, :]
    gtab = np.stack([np.where(same, g_chunk[2 * j + head_of_lane][:, None], f32(0.0))
                     for j in range(RET_SLABS)]).astype(f32)
    return jnp.asarray(dm), jnp.asarray(zeta_t), jnp.asarray(xi_t), jnp.asarray(gtab)


def _pool_block_diag(pool_w):
    depth, groups = pool_w.shape[0], pool_w.shape[1]
    eye = jnp.eye(groups, dtype=jnp.bfloat16)[None, :, None, :, None]
    wide = pool_w.astype(jnp.bfloat16)[:, :, :, None, :] * eye
    return wide.reshape(depth, POOL_WIDTH, POOL_WIDTH)


def kernel(x, attn_norm_g, w_in, attn_sinks, rel_bias, pool_w, pool_scale, w_out, mlp_norm_g, w_up, w_down,
           final_norm_g):
    depth = w_in.shape[0]
    seq = x.shape[1]
    bf16 = jnp.bfloat16
    w_in_b = _prep_w_in(w_in)
    w_out_b = _prep_w_out(w_out)
    w_pool = _pool_block_diag(pool_w)
    w_up_b = w_up.astype(bf16)
    w_down_b = w_down.astype(bf16)
    bias = _attention_bias(rel_bias)
    cos, sin_a, sin_b = _rotary_tables(seq)
    dm, zeta_t, xi_t, gtab = _retention_tables()
    g_attn = attn_norm_g.reshape(depth, 1, D_MODEL)
    g_mlp = mlp_norm_g.reshape(depth, 1, D_MODEL)
    p_scale = pool_scale.reshape(depth, 1, POOL_WIDTH)
    gf = final_norm_g.reshape(1, D_MODEL)
    for layer in range(depth):
        x = _mixer_call(layer, x, attn_sinks, g_attn, w_in_b, bias, cos, sin_a, sin_b, dm, zeta_t, xi_t, gtab,
                        w_pool, p_scale, w_out_b)
        x = _mlp_call(layer, x, g_mlp, w_up_b, w_down_b, gf, final_norm=(layer == depth - 1))
    return x
```

```python
import functools

import jax
import jax.numpy as jnp
import numpy as np
from jax import lax
from jax.experimental import pallas as pl
from jax.experimental.pallas import tpu as pltpu

D_MODEL = 1024
HEAD_DIM = 64
ATT_Q_HEADS = 6
ATT_KV_HEADS = 2
ATT_GROUP = ATT_Q_HEADS // ATT_KV_HEADS
RET_HEADS = 6
POOL_WINDOWS = (2, 4, 8, 16)
POOL_GROUP_WIDTH = 64
POOL_WIDTH = 256
BLOCK = 128
N_BUCKETS = 32
MAX_DISTANCE = 128
D_FF = 4 * D_MODEL
RMS_EPS = 1e-6
ROPE_BASE = 10000.0
NEG_INF = -1e30

LANES = 128
HALF = LANES // 2
ATT_Q_W = ATT_Q_HEADS * HEAD_DIM
ATT_KV_W = ATT_KV_HEADS * HEAD_DIM
RET_W = RET_HEADS * HEAD_DIM
RET_SLABS = RET_W // LANES
ATT_SLABS = ATT_Q_W // LANES
POOL_TAIL = 16

C_QA = 0
C_KA = C_QA + ATT_Q_W
C_VA = C_KA + ATT_KV_W
C_QR = C_VA + ATT_KV_W
C_KR = C_QR + RET_W
C_VR = C_KR + RET_W
C_GR = C_VR + RET_W
C_UP = C_GR + RET_W
Z_WIDTH = C_UP + POOL_WIDTH

Y_A = 0
Y_R = ATT_Q_W
Y_P = ATT_Q_W + RET_W

MIXER_TILE = 256
MLP_TILE = 512
FF_CHUNK = 1024
VMEM_LIMIT_BYTES = 56 * 1024 * 1024


def _rms_norm(x, g):
    ms = jnp.mean(x * x, axis=-1, keepdims=True)
    return x * lax.rsqrt(ms + RMS_EPS) * g


def _t5_bucket(dist):
    max_exact = N_BUCKETS // 2
    n = np.maximum(dist, 0)
    large = max_exact + (np.log(np.maximum(n, 1) / max_exact)
                         / np.log(MAX_DISTANCE / max_exact)
                         * (N_BUCKETS - max_exact)).astype(np.int64)
    large = np.minimum(large, N_BUCKETS - 1)
    return np.where(n < max_exact, n, large).astype(np.int32)


def _mixer_kernel(sinks_ref, x_ref, g_ref, win_ref, bias_ref, cos_ref, sina_ref, sinb_ref, dm_ref, zeta_ref,
                  xi_ref, gtab_ref, wpool_ref, pscale_ref, wout_ref, o_ref,
                  z_ref, y_ref, kbuf, vbuf, ubuf, r_ref, *, tile, layer):
    t = pl.program_id(1)
    nblk = tile // BLOCK
    f32, bf16 = jnp.float32, jnp.bfloat16

    @pl.when(t == 0)
    def _():
        kbuf[0:BLOCK, :] = jnp.zeros((BLOCK, ATT_KV_W), bf16)
        vbuf[0:BLOCK, :] = jnp.zeros((BLOCK, ATT_KV_W), bf16)
        ubuf[0:POOL_TAIL, :] = jnp.zeros((POOL_TAIL, POOL_WIDTH), f32)
        r_ref[...] = jnp.zeros(r_ref.shape, f32)

    x = x_ref[0]
    h = _rms_norm(x, g_ref[...]).astype(bf16)
    z_ref[...] = jnp.dot(h, win_ref[...], preferred_element_type=f32)

    lane = lax.broadcasted_iota(jnp.int32, (BLOCK, LANES), 1)
    lo = lane < HALF

    kbuf[BLOCK:BLOCK + tile, :] = z_ref[:, C_KA:C_KA + ATT_KV_W].astype(bf16)
    vbuf[BLOCK:BLOCK + tile, :] = z_ref[:, C_VA:C_VA + ATT_KV_W].astype(bf16)
    for n in range(nblk):
        r0 = n * BLOCK
        qs = [z_ref[r0:r0 + BLOCK, C_QA + j * LANES:C_QA + (j + 1) * LANES] for j in range(ATT_SLABS)]
        parts = [jnp.where(lo, q, 0.0) for q in qs] + [jnp.where(lo, 0.0, q) for q in qs]
        qstack = jnp.concatenate(parts, axis=0).astype(bf16)
        kk = kbuf[r0:r0 + 2 * BLOCK, :]
        vv = vbuf[r0:r0 + 2 * BLOCK, :]
        s = lax.dot_general(qstack, kk, (((1,), (1,)), ((), ())), preferred_element_type=f32)
        if n == 0:
            first = jnp.where(t == 0, 1, 0)
            s = s + bias_ref[first]
        else:
            s = s + bias_ref[0]
        ps, denoms = [], []
        for hd in range(ATT_Q_HEADS):
            sh = s[hd * BLOCK:(hd + 1) * BLOCK]
            sink = sinks_ref[layer, hd]
            m = jnp.maximum(jnp.max(sh, axis=-1, keepdims=True), sink)
            p = jnp.exp(sh - m)
            denoms.append(jnp.sum(p, axis=-1, keepdims=True) + jnp.exp(sink - m))
            ps.append(p.astype(bf16))
        pstack = jnp.concatenate(ps, axis=0)
        o = jnp.dot(pstack, vv, preferred_element_type=f32)
        for j in range(ATT_SLABS):
            o_lo = o[j * BLOCK:(j + 1) * BLOCK] / denoms[j]
            o_hi = o[(j + ATT_GROUP) * BLOCK:(j + ATT_GROUP + 1) * BLOCK] / denoms[j + ATT_GROUP]
            y_ref[r0:r0 + BLOCK, Y_A + j * LANES:Y_A + (j + 1) * LANES] = jnp.where(lo, o_lo, o_hi).astype(bf16)

    def rotary(v, cos, sin_a, sin_b):
        return v * cos + pltpu.roll(v, LANES - 1, 1) * sin_a + pltpu.roll(v, 1, 1) * sin_b

    for n in range(nblk):
        r0 = n * BLOCK
        cos = cos_ref[r0:r0 + BLOCK, :]
        sin_a = sina_ref[r0:r0 + BLOCK, :]
        sin_b = sinb_ref[r0:r0 + BLOCK, :]
        for j in range(RET_SLABS):
            c0 = j * LANES
            q = rotary(z_ref[r0:r0 + BLOCK, C_QR + c0:C_QR + c0 + LANES], cos, sin_a, sin_b)
            k = rotary(z_ref[r0:r0 + BLOCK, C_KR + c0:C_KR + c0 + LANES], cos, sin_a, sin_b)
            v = z_ref[r0:r0 + BLOCK, C_VR + c0:C_VR + c0 + LANES]
            gate = z_ref[r0:r0 + BLOCK, C_GR + c0:C_GR + c0 + LANES]
            qb = q.astype(bf16)
            kb = k.astype(bf16)
            ksplit = jnp.concatenate([jnp.where(lo, k, 0.0), jnp.where(lo, 0.0, k)], axis=0).astype(bf16)
            sc = lax.dot_general(qb, ksplit, (((1,), (1,)), ((), ())), preferred_element_type=f32)
            acat = (sc * dm_ref[j]).astype(bf16)
            vsplit = jnp.concatenate([jnp.where(lo, v, 0.0), jnp.where(lo, 0.0, v)], axis=0).astype(bf16)
            inner = jnp.dot(acat, vsplit, preferred_element_type=f32)
            r_prev = r_ref[j]
            cross = jnp.dot(qb, r_prev.astype(bf16), preferred_element_type=f32) * xi_ref[j]
            o = inner + cross
            vz = (v * zeta_ref[j]).astype(bf16)
            u = lax.dot_general(kb, vz, (((0,), (0,)), ((), ())), preferred_element_type=f32)
            gt = gtab_ref[j]
            r_ref[j] = r_prev * gt + jnp.where(gt > 0.0, u, 0.0)
            o2 = o * o
            ms_lo = jnp.sum(jnp.where(lo, o2, 0.0), axis=-1, keepdims=True)
            ms_hi = jnp.sum(jnp.where(lo, 0.0, o2), axis=-1, keepdims=True)
            ms = jnp.where(lo, ms_lo, ms_hi) * (1.0 / HEAD_DIM)
            on = o * lax.rsqrt(ms + RMS_EPS)
            y_ref[r0:r0 + BLOCK, Y_R + c0:Y_R + c0 + LANES] = (jax.nn.silu(gate) * on).astype(bf16)

    ubuf[POOL_TAIL:POOL_TAIL + tile, :] = z_ref[:, C_UP:C_UP + POOL_WIDTH]
    lane_t = lax.broadcasted_iota(jnp.int32, (tile, LANES), 1)
    lo_t = lane_t < HALF
    pos1 = lax.broadcasted_iota(jnp.int32, (tile, LANES), 0) + (t * tile + 1)
    pooled = []
    for sl in range(POOL_WIDTH // LANES):
        w_small, w_big = POOL_WINDOWS[2 * sl], POOL_WINDOWS[2 * sl + 1]
        cols = slice(sl * LANES, (sl + 1) * LANES)

        def shifted(d, cols=cols):
            return ubuf[POOL_TAIL - d:POOL_TAIL - d + tile, cols]

        u0 = shifted(0)
        acc = u0
        for d in range(1, w_small):
            acc = acc + shifted(d)
        small = acc
        for d in range(w_small, w_big):
            acc = acc + shifted(d)
        win = jnp.where(lo_t, small, acc)
        cnt = jnp.minimum(pos1, jnp.where(lo_t, w_small, w_big)).astype(f32)
        pooled.append((win / cnt - u0).astype(bf16))
    pooled = jnp.concatenate(pooled, axis=1)
    mixed = jnp.dot(pooled, wpool_ref[...], preferred_element_type=f32) * pscale_ref[...]
    y_ref[:, Y_P:Y_P + POOL_WIDTH] = mixed.astype(bf16)

    o_ref[0] = x + jnp.dot(y_ref[...], wout_ref[...], preferred_element_type=f32)

    kbuf[0:BLOCK, :] = kbuf[tile:tile + BLOCK, :]
    vbuf[0:BLOCK, :] = vbuf[tile:tile + BLOCK, :]
    ubuf[0:POOL_TAIL, :] = ubuf[tile:tile + POOL_TAIL, :]


def _const_spec(shape):
    nd = len(shape)
    return pl.BlockSpec(shape, lambda b, t, _nd=nd: (0,) * _nd)


def _layer_spec(shape, layer):
    nd = len(shape)
    return pl.BlockSpec((None,) + tuple(shape[1:]), lambda b, t, _nd=nd: (layer,) + (0,) * (_nd - 1))


def _mixer_call(layer, x, sinks, g, win, bias, cos, sin_a, sin_b, dm, zeta, xi, gtab, wpool, pscale, wout):
    batch, seq, _ = x.shape
    tile = MIXER_TILE
    grid = (batch, seq // tile)
    f32, bf16 = jnp.float32, jnp.bfloat16
    rot_spec = pl.BlockSpec((tile, LANES), lambda b, t: (t, 0))
    in_specs = [
        pl.BlockSpec(memory_space=pltpu.SMEM),
        pl.BlockSpec((1, tile, D_MODEL), lambda b, t: (b, t, 0)),
        _layer_spec(g.shape, layer), _layer_spec(win.shape, layer), _const_spec(bias.shape),
        rot_spec, rot_spec, rot_spec,
        _const_spec(dm.shape), _const_spec(zeta.shape), _const_spec(xi.shape), _const_spec(gtab.shape),
        _layer_spec(wpool.shape, layer), _layer_spec(pscale.shape, layer), _layer_spec(wout.shape, layer),
    ]
    return pl.pallas_call(
        functools.partial(_mixer_kernel, tile=tile, layer=layer),
        grid=grid,
        in_specs=in_specs,
        out_specs=pl.BlockSpec((1, tile, D_MODEL), lambda b, t: (b, t, 0)),
        out_shape=jax.ShapeDtypeStruct(x.shape, x.dtype),
        scratch_shapes=[
            pltpu.VMEM((tile, Z_WIDTH), f32),
            pltpu.VMEM((tile, D_MODEL), bf16),
            pltpu.VMEM((BLOCK + tile, ATT_KV_W), bf16),
            pltpu.VMEM((BLOCK + tile, ATT_KV_W), bf16),
            pltpu.VMEM((POOL_TAIL + tile, POOL_WIDTH), f32),
            pltpu.VMEM((RET_SLABS, LANES, LANES), f32),
        ],
        compiler_params=pltpu.CompilerParams(
            dimension_semantics=("arbitrary", "arbitrary"),
            vmem_limit_bytes=VMEM_LIMIT_BYTES),
        name="mixer",
    )(sinks, x, g, win, bias, cos, sin_a, sin_b, dm, zeta, xi, gtab, wpool, pscale, wout)


def _mlp_kernel(x_ref, g_ref, wup_ref, wdown_ref, gf_ref, o_ref, *, final_norm):
    f32, bf16 = jnp.float32, jnp.bfloat16
    x = x_ref[0]
    h = _rms_norm(x, g_ref[...]).astype(bf16)
    acc = x
    for c in range(D_FF // FF_CHUNK):
        a = jnp.dot(h, wup_ref[:, c * FF_CHUNK:(c + 1) * FF_CHUNK], preferred_element_type=f32)
        a = jnp.square(jnp.maximum(a, 0.0)).astype(bf16)
        acc = acc + jnp.dot(a, wdown_ref[c * FF_CHUNK:(c + 1) * FF_CHUNK, :], preferred_element_type=f32)
    if final_norm:
        acc = _rms_norm(acc, gf_ref[...])
    o_ref[0] = acc


def _mlp_call(layer, x, g, wup, wdown, gf, final_norm):
    batch, seq, _ = x.shape
    tile = MLP_TILE
    grid = (batch, seq // tile)
    return pl.pallas_call(
        functools.partial(_mlp_kernel, final_norm=final_norm),
        grid=grid,
        in_specs=[
            pl.BlockSpec((1, tile, D_MODEL), lambda b, t: (b, t, 0)),
            _layer_spec(g.shape, layer), _layer_spec(wup.shape, layer), _layer_spec(wdown.shape, layer),
            _const_spec(gf.shape),
        ],
        out_specs=pl.BlockSpec((1, tile, D_MODEL), lambda b, t: (b, t, 0)),
        out_shape=jax.ShapeDtypeStruct(x.shape, x.dtype),
        compiler_params=pltpu.CompilerParams(
            dimension_semantics=("arbitrary", "arbitrary"),
            vmem_limit_bytes=VMEM_LIMIT_BYTES),
        name="mlp",
    )(x, g, wup, wdown, gf)


def _att_slab_order(w, axis):
    heads = [lax.slice_in_dim(w, hd * HEAD_DIM, (hd + 1) * HEAD_DIM, axis=axis) for hd in range(ATT_Q_HEADS)]
    order = []
    for j in range(ATT_SLABS):
        order += [heads[j], heads[j + ATT_GROUP]]
    return jnp.concatenate(order, axis=axis)


def _prep_w_in(w_in):
    w = w_in.astype(jnp.bfloat16)
    qa = _att_slab_order(w[..., 0:ATT_Q_W], axis=2) * jnp.asarray(HEAD_DIM ** -0.5, jnp.bfloat16)
    return jnp.concatenate([qa, w[..., ATT_Q_W:]], axis=-1)


def _prep_w_out(w_out):
    w = w_out.astype(jnp.bfloat16)
    return jnp.concatenate([_att_slab_order(w[:, 0:ATT_Q_W], axis=1), w[:, ATT_Q_W:]], axis=1)


def _attention_bias(rel_bias):
    i = np.arange(BLOCK)[:, None]
    j = np.arange(2 * BLOCK)[None, :]
    dist = BLOCK + i - j
    band = (dist >= 0) & (dist < BLOCK)
    bucket = _t5_bucket(dist)
    table = rel_bias.astype(jnp.float32)
    bias = jnp.zeros((ATT_Q_HEADS, BLOCK, 2 * BLOCK), jnp.float32)
    for b in range(N_BUCKETS):
        bias = jnp.where(jnp.asarray(bucket == b)[None], table[b][:, None, None], bias)
    normal = jnp.where(jnp.asarray(band)[None], bias, NEG_INF)
    first = jnp.where(jnp.asarray(band & (j >= BLOCK))[None], bias, NEG_INF)
    return jnp.stack([normal, first]).reshape(2, ATT_Q_HEADS * BLOCK, 2 * BLOCK)


def _rotary_tables(seq):
    inv = 1.0 / (ROPE_BASE ** jnp.linspace(0.0, 1.0, HEAD_DIM // 2, dtype=jnp.float32))
    ang = jnp.arange(seq).astype(jnp.float32)[:, None] * inv[None, :]
    reps = (1, LANES // HEAD_DIM)
    cos = jnp.tile(jnp.repeat(jnp.cos(ang), 2, axis=-1), reps)
    sin = jnp.tile(jnp.repeat(jnp.sin(ang), 2, axis=-1), reps)
    even = (np.arange(LANES) % 2 == 0)[None, :]
    return cos, jnp.where(even, -sin, 0.0), jnp.where(even, 0.0, sin)


def _retention_tables():
    c = BLOCK
    f32 = np.float32
    scale = f32(HEAD_DIM ** -0.5)
    lg = np.log(f32(1.0) - f32(2.0) ** (f32(-5.0) - np.arange(RET_HEADS, dtype=f32))).astype(f32)
    idx = np.arange(c, dtype=f32)
    diff = idx[:, None] - idx[None, :]
    dmask = np.where(diff >= 0, np.exp(lg[:, None, None] * np.maximum(diff, f32(0.0))), f32(0.0)).astype(f32)
    dm = np.stack([np.concatenate([dmask[2 * j], dmask[2 * j + 1]], axis=1)
                   for j in range(RET_SLABS)]) * scale
    zeta = (np.exp(lg[:, None] * (f32(c) - f32(1.0) - idx)[None, :]) * scale).astype(f32)
    xi = np.exp(lg[None, :] * (idx[:, None] + f32(1.0))).astype(f32)
    g_chunk = np.exp(lg * f32(c)).astype(f32)
    head_of_lane = (np.arange(LANES) >= HALF).astype(np.int32)
    zeta_t = np.stack([zeta[2 * j + head_of_lane].T for j in range(RET_SLABS)])
    xi_t = np.stack([xi[:, 2 * j + head_of_lane] for j in range(RET_SLABS)])
    same = head_of_lane[:, None] == head_of_lane[None, :]
    gtab = np.stack([np.where(same, g_chunk[2 * j + head_of_lane][:, None], f32(0.0))
                     for j in range(RET_SLABS)]).astype(f32)
    return jnp.asarray(dm), jnp.asarray(zeta_t), jnp.asarray(xi_t), jnp.asarray(gtab)


def _pool_block_diag(pool_w):
    depth, groups = pool_w.shape[0], pool_w.shape[1]
    eye = jnp.eye(groups, dtype=jnp.bfloat16)[None, :, None, :, None]
    wide = pool_w.astype(jnp.bfloat16)[:, :, :, None, :] * eye
    return wide.reshape(depth, POOL_WIDTH, POOL_WIDTH)


def kernel(x, attn_norm_g, w_in, attn_sinks, rel_bias, pool_w, pool_scale, w_out, mlp_norm_g, w_up, w_down,
           final_norm_g):
    depth = w_in.shape[0]
    seq = x.shape[1]
    bf16 = jnp.bfloat16
    w_in_b = _prep_w_in(w_in)
    w_out_b = _prep_w_out(w_out)
    w_pool = _pool_block_diag(pool_w)
    w_up_b = w_up.astype(bf16)
    w_down_b = w_down.astype(bf16)
    bias = _attention_bias(rel_bias)
    cos, sin_a, sin_b = _rotary_tables(seq)
    dm, zeta_t, xi_t, gtab = _retention_tables()
    g_attn = attn_norm_g.reshape(depth, 1, D_MODEL)
    g_mlp = mlp_norm_g.reshape(depth, 1, D_MODEL)
    p_scale = pool_scale.reshape(depth, 1, POOL_WIDTH)
    gf = final_norm_g.reshape(1, D_MODEL)
    for layer in range(depth):
        x = _mixer_call(layer, x, attn_sinks, g_attn, w_in_b, bias, cos, sin_a, sin_b, dm, zeta_t, xi_t, gtab,
                        w_pool, p_scale, w_out_b)
        x = _mlp_call(layer, x, g_mlp, w_up_b, w_down_b, gf, final_norm=(layer == depth - 1))
    return x
```

```python
import functools

import jax
import jax.numpy as jnp
import numpy as np
from jax import lax
from jax.experimental import pallas as pl
from jax.experimental.pallas import tpu as pltpu

D_MODEL = 1024
HEAD_DIM = 64
ATT_Q_HEADS = 6
ATT_KV_HEADS = 2
ATT_GROUP = ATT_Q_HEADS // ATT_KV_HEADS
RET_HEADS = 6
POOL_WINDOWS = (2, 4, 8, 16)
POOL_GROUP_WIDTH = 64
POOL_WIDTH = 256
BLOCK = 128
N_BUCKETS = 32
MAX_DISTANCE = 128
D_FF = 4 * D_MODEL
RMS_EPS = 1e-6
ROPE_BASE = 10000.0
NEG_INF = -1e30

LANES = 128
HALF = LANES // 2
ATT_Q_W = ATT_Q_HEADS * HEAD_DIM
ATT_KV_W = ATT_KV_HEADS * HEAD_DIM
RET_W = RET_HEADS * HEAD_DIM
RET_SLABS = RET_W // LANES
ATT_SLABS = ATT_Q_W // LANES
POOL_TAIL = 16

C_QA = 0
C_KA = C_QA + ATT_Q_W
C_VA = C_KA + ATT_KV_W
C_QR = C_VA + ATT_KV_W
C_KR = C_QR + RET_W
C_VR = C_KR + RET_W
C_GR = C_VR + RET_W
C_UP = C_GR + RET_W
Z_WIDTH = C_UP + POOL_WIDTH

Y_A = 0
Y_R = ATT_Q_W
Y_P = ATT_Q_W + RET_W

MIXER_TILE = 512
MIXER_SUB = 256
PROJ_CHUNK = 512
OUT_CHUNK = 512
MLP_TILE = 512
FF_CHUNK = 1024
VMEM_LIMIT_BYTES = 56 * 1024 * 1024


def _rms_norm(x, g):
    ms = jnp.mean(x * x, axis=-1, keepdims=True)
    return x * lax.rsqrt(ms + RMS_EPS) * g


def _t5_bucket(dist):
    max_exact = N_BUCKETS // 2
    n = np.maximum(dist, 0)
    large = max_exact + (np.log(np.maximum(n, 1) / max_exact)
                         / np.log(MAX_DISTANCE / max_exact)
                         * (N_BUCKETS - max_exact)).astype(np.int64)
    large = np.minimum(large, N_BUCKETS - 1)
    return np.where(n < max_exact, n, large).astype(np.int32)


def _spread(first, second):
    keyed = [((k + 0.5) / len(first), 0, task) for k, task in enumerate(first)]
    keyed += [((k + 0.5) / len(second), 1, task) for k, task in enumerate(second)]
    return [task for _, _, task in sorted(keyed, key=lambda item: item[:2])]


def _mixer_kernel(sinks_ref, x_ref, xn_ref, g_ref, win_ref, bias_ref, cos_ref, sina_ref, sinb_ref, dm_ref,
                  zeta_ref, xi_ref, gtab_ref, wpool_ref, pscale_ref, wout_ref, o_ref,
                  z0_ref, z_ref, y_ref, kbuf, vbuf, ubuf, r_ref, *, tile, sub, layer):
    t = pl.program_id(1)
    npiece = tile // sub
    assert npiece >= 2, "piece 0 of the next tile is projected while the last piece is mixed"
    f32, bf16 = jnp.float32, jnp.bfloat16

    lane = lax.broadcasted_iota(jnp.int32, (BLOCK, LANES), 1)
    lo = lane < HALF

    def zrows(r0):
        return (z0_ref, r0) if r0 < sub else (z_ref, r0 - sub)

    def project_tasks(read_x, zdst, zrow, brow):
        cache = []

        def normed():
            if not cache:
                cache.append(_rms_norm(read_x(), g_ref[...]).astype(bf16))
            return cache[0]

        side = ((C_KA, ATT_KV_W, kbuf, BLOCK), (C_VA, ATT_KV_W, vbuf, BLOCK), (C_UP, POOL_WIDTH, ubuf, POOL_TAIL))
        chunks = [(c0, min(c0 + PROJ_CHUNK, Z_WIDTH)) for c0 in range(0, Z_WIDTH, PROJ_CHUNK)]
        for cs, width, _, _ in side:
            assert sum(c0 <= cs and cs + width <= c1 for c0, c1 in chunks) == 1

        def make(c0, c1):
            def task():
                z = jnp.dot(normed(), win_ref[:, c0:c1], preferred_element_type=f32)
                zdst[zrow:zrow + sub, c0:c1] = z
                for cs, width, buf, head in side:
                    if c0 <= cs and cs + width <= c1:
                        buf[head + brow:head + brow + sub, :] = z[:, cs - c0:cs - c0 + width].astype(buf.dtype)
            return task

        return [make(c0, c1) for c0, c1 in chunks]

    @pl.when(t == 0)
    def _():
        kbuf[0:BLOCK, :] = jnp.zeros((BLOCK, ATT_KV_W), bf16)
        vbuf[0:BLOCK, :] = jnp.zeros((BLOCK, ATT_KV_W), bf16)
        ubuf[0:POOL_TAIL, :] = jnp.zeros((POOL_TAIL, POOL_WIDTH), f32)
        r_ref[...] = jnp.zeros(r_ref.shape, f32)
        for task in project_tasks(lambda: x_ref[0, 0:sub, :], z0_ref, 0, 0):
            task()

    def attention(r0):
        zr, zo = zrows(r0)
        qs = [zr[zo:zo + BLOCK, C_QA + j * LANES:C_QA + (j + 1) * LANES] for j in range(ATT_SLABS)]
        parts = [jnp.where(lo, q, 0.0) for q in qs] + [jnp.where(lo, 0.0, q) for q in qs]
        qstack = jnp.concatenate(parts, axis=0).astype(bf16)
        kk = kbuf[r0:r0 + 2 * BLOCK, :]
        vv = vbuf[r0:r0 + 2 * BLOCK, :]
        s = lax.dot_general(qstack, kk, (((1,), (1,)), ((), ())), preferred_element_type=f32)
        if r0 == 0:
            s = s + bias_ref[jnp.where(t == 0, 1, 0)]
        else:
            s = s + bias_ref[0]
        ps, denoms = [], []
        for hd in range(ATT_Q_HEADS):
            sh = s[hd * BLOCK:(hd + 1) * BLOCK]
            sink = sinks_ref[layer, hd]
            m = jnp.maximum(jnp.max(sh, axis=-1, keepdims=True), sink)
            p = jnp.exp(sh - m)
            denoms.append(jnp.sum(p, axis=-1, keepdims=True) + jnp.exp(sink - m))
            ps.append(p.astype(bf16))
        pstack = jnp.concatenate(ps, axis=0)
        o = jnp.dot(pstack, vv, preferred_element_type=f32)
        for j in range(ATT_SLABS):
            o_lo = o[j * BLOCK:(j + 1) * BLOCK] / denoms[j]
            o_hi = o[(j + ATT_GROUP) * BLOCK:(j + ATT_GROUP + 1) * BLOCK] / denoms[j + ATT_GROUP]
            y_ref[r0:r0 + BLOCK, Y_A + j * LANES:Y_A + (j + 1) * LANES] = jnp.where(lo, o_lo, o_hi).astype(bf16)

    def rotary(v, cos, sin_a, sin_b):
        return v * cos + pltpu.roll(v, LANES - 1, 1) * sin_a + pltpu.roll(v, 1, 1) * sin_b

    def retention(r0, j):
        zr, zo = zrows(r0)
        cos = cos_ref[r0:r0 + BLOCK, :]
        sin_a = sina_ref[r0:r0 + BLOCK, :]
        sin_b = sinb_ref[r0:r0 + BLOCK, :]
        c0 = j * LANES
        q = rotary(zr[zo:zo + BLOCK, C_QR + c0:C_QR + c0 + LANES], cos, sin_a, sin_b)
        k = rotary(zr[zo:zo + BLOCK, C_KR + c0:C_KR + c0 + LANES], cos, sin_a, sin_b)
        v = zr[zo:zo + BLOCK, C_VR + c0:C_VR + c0 + LANES]
        gate = zr[zo:zo + BLOCK, C_GR + c0:C_GR + c0 + LANES]
        qb = q.astype(bf16)
        kb = k.astype(bf16)
        ksplit = jnp.concatenate([jnp.where(lo, k, 0.0), jnp.where(lo, 0.0, k)], axis=0).astype(bf16)
        sc = lax.dot_general(qb, ksplit, (((1,), (1,)), ((), ())), preferred_element_type=f32)
        acat = (sc * dm_ref[j]).astype(bf16)
        vsplit = jnp.concatenate([jnp.where(lo, v, 0.0), jnp.where(lo, 0.0, v)], axis=0).astype(bf16)
        inner = jnp.dot(acat, vsplit, preferred_element_type=f32)
        r_prev = r_ref[j]
        cross = jnp.dot(qb, r_prev.astype(bf16), preferred_element_type=f32) * xi_ref[j]
        o = inner + cross
        vz = (v * zeta_ref[j]).astype(bf16)
        u = lax.dot_general(kb, vz, (((0,), (0,)), ((), ())), preferred_element_type=f32)
        gt = gtab_ref[j]
        r_ref[j] = r_prev * gt + jnp.where(gt > 0.0, u, 0.0)
        o2 = o * o
        ms_lo = jnp.sum(jnp.where(lo, o2, 0.0), axis=-1, keepdims=True)
        ms_hi = jnp.sum(jnp.where(lo, 0.0, o2), axis=-1, keepdims=True)
        ms = jnp.where(lo, ms_lo, ms_hi) * (1.0 / HEAD_DIM)
        on = o * lax.rsqrt(ms + RMS_EPS)
        y_ref[r0:r0 + BLOCK, Y_R + c0:Y_R + c0 + LANES] = (jax.nn.silu(gate) * on).astype(bf16)

    def pool(r0):
        lo_t = lax.broadcasted_iota(jnp.int32, (sub, LANES), 1) < HALF
        pos1 = lax.broadcasted_iota(jnp.int32, (sub, LANES), 0) + (t * tile + r0 + 1)
        pooled = []
        for sl in range(POOL_WIDTH // LANES):
            w_small, w_big = POOL_WINDOWS[2 * sl], POOL_WINDOWS[2 * sl + 1]
            cols = slice(sl * LANES, (sl + 1) * LANES)

            def shifted(d, cols=cols):
                return ubuf[POOL_TAIL + r0 - d:POOL_TAIL + r0 - d + sub, cols]

            u0 = shifted(0)
            acc = u0
            for d in range(1, w_small):
                acc = acc + shifted(d)
            small = acc
            for d in range(w_small, w_big):
                acc = acc + shifted(d)
            win = jnp.where(lo_t, small, acc)
            cnt = jnp.minimum(pos1, jnp.where(lo_t, w_small, w_big)).astype(f32)
            pooled.append((win / cnt - u0).astype(bf16))
        pooled = jnp.concatenate(pooled, axis=1)
        mixed = jnp.dot(pooled, wpool_ref[...], preferred_element_type=f32) * pscale_ref[...]
        y_ref[r0:r0 + sub, Y_P:Y_P + POOL_WIDTH] = mixed.astype(bf16)

    def output_tasks(r0):
        def make(c0):
            def task():
                o_ref[0, r0:r0 + sub, c0:c0 + OUT_CHUNK] = x_ref[0, r0:r0 + sub, c0:c0 + OUT_CHUNK] + jnp.dot(
                    y_ref[r0:r0 + sub, :], wout_ref[:, c0:c0 + OUT_CHUNK], preferred_element_type=f32)
            return task
        return [make(c0) for c0 in range(0, D_MODEL, OUT_CHUNK)]

    def mix_tasks(r0):
        tasks = []
        for b0 in range(r0, r0 + sub, BLOCK):
            tasks.append(functools.partial(attention, b0))
            tasks += [functools.partial(retention, b0, j) for j in range(RET_SLABS)]
        tasks.append(functools.partial(pool, r0))
        return tasks

    for p in range(npiece):
        r0 = p * sub
        if p + 1 < npiece:
            matmuls = project_tasks(lambda r=r0 + sub: x_ref[0, r:r + sub, :], z_ref, r0, r0 + sub)
        else:
            matmuls = project_tasks(lambda: xn_ref[0], z0_ref, 0, tile)
        if p >= 1:
            matmuls = _spread(matmuls, output_tasks(r0 - sub))
        for task in _spread(matmuls, mix_tasks(r0)):
            task()
    for task in output_tasks(tile - sub):
        task()

    kbuf[0:BLOCK + sub, :] = kbuf[tile:tile + BLOCK + sub, :]
    vbuf[0:BLOCK + sub, :] = vbuf[tile:tile + BLOCK + sub, :]
    ubuf[0:POOL_TAIL + sub, :] = ubuf[tile:tile + POOL_TAIL + sub, :]


def _const_spec(shape):
    nd = len(shape)
    return pl.BlockSpec(shape, lambda b, t, _nd=nd: (0,) * _nd)


def _layer_spec(shape, layer):
    nd = len(shape)
    return pl.BlockSpec((None,) + tuple(shape[1:]), lambda b, t, _nd=nd: (layer,) + (0,) * (_nd - 1))


def _mixer_call(layer, x, sinks, g, win, bias, cos, sin_a, sin_b, dm, zeta, xi, gtab, wpool, pscale, wout):
    batch, seq, _ = x.shape
    tile, sub = MIXER_TILE, MIXER_SUB
    grid = (batch, seq // tile)
    last_piece = seq // sub - 1
    f32, bf16 = jnp.float32, jnp.bfloat16
    rot_spec = pl.BlockSpec((tile, LANES), lambda b, t: (t, 0))
    in_specs = [
        pl.BlockSpec(memory_space=pltpu.SMEM),
        pl.BlockSpec((1, tile, D_MODEL), lambda b, t: (b, t, 0)),
        pl.BlockSpec((1, sub, D_MODEL), lambda b, t: (b, jnp.minimum((t + 1) * (tile // sub), last_piece), 0)),
        _layer_spec(g.shape, layer), _layer_spec(win.shape, layer), _const_spec(bias.shape),
        rot_spec, rot_spec, rot_spec,
        _const_spec(dm.shape), _const_spec(zeta.shape), _const_spec(xi.shape), _const_spec(gtab.shape),
        _layer_spec(wpool.shape, layer), _layer_spec(pscale.shape, layer), _layer_spec(wout.shape, layer),
    ]
    return pl.pallas_call(
        functools.partial(_mixer_kernel, tile=tile, sub=sub, layer=layer),
        grid=grid,
        in_specs=in_specs,
        out_specs=pl.BlockSpec((1, tile, D_MODEL), lambda b, t: (b, t, 0)),
        out_shape=jax.ShapeDtypeStruct(x.shape, x.dtype),
        scratch_shapes=[
            pltpu.VMEM((sub, Z_WIDTH), f32),
            pltpu.VMEM((tile - sub, Z_WIDTH), f32),
            pltpu.VMEM((tile, D_MODEL), bf16),
            pltpu.VMEM((BLOCK + tile + sub, ATT_KV_W), bf16),
            pltpu.VMEM((BLOCK + tile + sub, ATT_KV_W), bf16),
            pltpu.VMEM((POOL_TAIL + tile + sub, POOL_WIDTH), f32),
            pltpu.VMEM((RET_SLABS, LANES, LANES), f32),
        ],
        compiler_params=pltpu.CompilerParams(
            dimension_semantics=("arbitrary", "arbitrary"),
            vmem_limit_bytes=VMEM_LIMIT_BYTES),
        name="mixer",
    )(sinks, x, x, g, win, bias, cos, sin_a, sin_b, dm, zeta, xi, gtab, wpool, pscale, wout)


def _mlp_kernel(x_ref, g_ref, wup_ref, wdown_ref, gf_ref, o_ref, *, final_norm):
    f32, bf16 = jnp.float32, jnp.bfloat16
    x = x_ref[0]
    h = _rms_norm(x, g_ref[...]).astype(bf16)
    acc = x
    for c in range(D_FF // FF_CHUNK):
        a = jnp.dot(h, wup_ref[:, c * FF_CHUNK:(c + 1) * FF_CHUNK], preferred_element_type=f32)
        a = jnp.square(jnp.maximum(a, 0.0)).astype(bf16)
        acc = acc + jnp.dot(a, wdown_ref[c * FF_CHUNK:(c + 1) * FF_CHUNK, :], preferred_element_type=f32)
    if final_norm:
        acc = _rms_norm(acc, gf_ref[...])
    o_ref[0] = acc


def _mlp_call(layer, x, g, wup, wdown, gf, final_norm):
    batch, seq, _ = x.shape
    tile = MLP_TILE
    grid = (batch, seq // tile)
    return pl.pallas_call(
        functools.partial(_mlp_kernel, final_norm=final_norm),
        grid=grid,
        in_specs=[
            pl.BlockSpec((1, tile, D_MODEL), lambda b, t: (b, t, 0)),
            _layer_spec(g.shape, layer), _layer_spec(wup.shape, layer), _layer_spec(wdown.shape, layer),
            _const_spec(gf.shape),
        ],
        out_specs=pl.BlockSpec((1, tile, D_MODEL), lambda b, t: (b, t, 0)),
        out_shape=jax.ShapeDtypeStruct(x.shape, x.dtype),
        compiler_params=pltpu.CompilerParams(
            dimension_semantics=("arbitrary", "arbitrary"),
            vmem_limit_bytes=VMEM_LIMIT_BYTES),
        name="mlp",
    )(x, g, wup, wdown, gf)


def _att_slab_order(w, axis):
    heads = [lax.slice_in_dim(w, hd * HEAD_DIM, (hd + 1) * HEAD_DIM, axis=axis) for hd in range(ATT_Q_HEADS)]
    order = []
    for j in range(ATT_SLABS):
        order += [heads[j], heads[j + ATT_GROUP]]
    return jnp.concatenate(order, axis=axis)


def _prep_w_in(w_in):
    w = w_in.astype(jnp.bfloat16)
    qa = _att_slab_order(w[..., 0:ATT_Q_W], axis=2) * jnp.asarray(HEAD_DIM ** -0.5, jnp.bfloat16)
    return jnp.concatenate([qa, w[..., ATT_Q_W:]], axis=-1)


def _prep_w_out(w_out):
    w = w_out.astype(jnp.bfloat16)
    return jnp.concatenate([_att_slab_order(w[:, 0:ATT_Q_W], axis=1), w[:, ATT_Q_W:]], axis=1)


def _attention_bias(rel_bias):
    i = np.arange(BLOCK)[:, None]
    j = np.arange(2 * BLOCK)[None, :]
    dist = BLOCK + i - j
    band = (dist >= 0) & (dist < BLOCK)
    bucket = _t5_bucket(dist)
    table = rel_bias.astype(jnp.float32)
    bias = jnp.zeros((ATT_Q_HEADS, BLOCK, 2 * BLOCK), jnp.float32)
    for b in range(N_BUCKETS):
        bias = jnp.where(jnp.asarray(bucket == b)[None], table[b][:, None, None], bias)
    normal = jnp.where(jnp.asarray(band)[None], bias, NEG_INF)
    first = jnp.where(jnp.asarray(band & (j >= BLOCK))[None], bias, NEG_INF)
    return jnp.stack([normal, first]).reshape(2, ATT_Q_HEADS * BLOCK, 2 * BLOCK)


def _rotary_tables(seq):
    inv = 1.0 / (ROPE_BASE ** jnp.linspace(0.0, 1.0, HEAD_DIM // 2, dtype=jnp.float32))
    ang = jnp.arange(seq).astype(jnp.float32)[:, None] * inv[None, :]
    reps = (1, LANES // HEAD_DIM)
    cos = jnp.tile(jnp.repeat(jnp.cos(ang), 2, axis=-1), reps)
    sin = jnp.tile(jnp.repeat(jnp.sin(ang), 2, axis=-1), reps)
    even = (np.arange(LANES) % 2 == 0)[None, :]
    return cos, jnp.where(even, -sin, 0.0), jnp.where(even, 0.0, sin)


def _retention_tables():
    c = BLOCK
    f32 = np.float32
    scale = f32(HEAD_DIM ** -0.5)
    lg = np.log(f32(1.0) - f32(2.0) ** (f32(-5.0) - np.arange(RET_HEADS, dtype=f32))).astype(f32)
    idx = np.arange(c, dtype=f32)
    diff = idx[:, None] - idx[None, :]
    dmask = np.where(diff >= 0, np.exp(lg[:, None, None] * np.maximum(diff, f32(0.0))), f32(0.0)).astype(f32)
    dm = np.stack([np.concatenate([dmask[2 * j], dmask[2 * j + 1]], axis=1)
                   for j in range(RET_SLABS)]) * scale
    zeta = (np.exp(lg[:, None] * (f32(c) - f32(1.0) - idx)[None, :]) * scale).astype(f32)
    xi = np.exp(lg[None, :] * (idx[:, None] + f32(1.0))).astype(f32)
    g_chunk = np.exp(lg * f32(c)).astype(f32)
    head_of_lane = (np.arange(LANES) >= HALF).astype(np.int32)
    zeta_t = np.stack([zeta[2 * j + head_of_lane].T for j in range(RET_SLABS)])
    xi_t = np.stack([xi[:, 2 * j + head_of_lane] for j in range(RET_SLABS)])
    same = head_of_lane[:, None] == head_of_lane[None, :]
    gtab = np.stack([np.where(same, g_chunk[2 * j + head_of_lane][:, None], f32(0.0))
                     for j in range(RET_SLABS)]).astype(f32)
    return jnp.asarray(dm), jnp.asarray(zeta_t), jnp.asarray(xi_t), jnp.asarray(gtab)


def _pool_block_diag(pool_w):
    depth, groups = pool_w.shape[0], pool_w.shape[1]
    eye = jnp.eye(groups, dtype=jnp.bfloat16)[None, :, None, :, None]
    wide = pool_w.astype(jnp.bfloat16)[:, :, :, None, :] * eye
    return wide.reshape(depth, POOL_WIDTH, POOL_WIDTH)


def kernel(x, attn_norm_g, w_in, attn_sinks, rel_bias, pool_w, pool_scale, w_out, mlp_norm_g, w_up, w_down,
           final_norm_g):
    depth = w_in.shape[0]
    seq = x.shape[1]
    bf16 = jnp.bfloat16
    w_in_b = _prep_w_in(w_in)
    w_out_b = _prep_w_out(w_out)
    w_pool = _pool_block_diag(pool_w)
    w_up_b = w_up.astype(bf16)
    w_down_b = w_down.astype(bf16)
    bias = _attention_bias(rel_bias)
    cos, sin_a, sin_b = _rotary_tables(seq)
    dm, zeta_t, xi_t, gtab = _retention_tables()
    g_attn = attn_norm_g.reshape(depth, 1, D_MODEL)
    g_mlp = mlp_norm_g.reshape(depth, 1, D_MODEL)
    p_scale = pool_scale.reshape(depth, 1, POOL_WIDTH)
    gf = final_norm_g.reshape(1, D_MODEL)
    for layer in range(depth):
        x = _mixer_call(layer, x, attn_sinks, g_attn, w_in_b, bias, cos, sin_a, sin_b, dm, zeta_t, xi_t, gtab,
                        w_pool, p_scale, w_out_b)
        x = _mlp_call(layer, x, g_mlp, w_up_b, w_down_b, gf, final_norm=(layer == depth - 1))
    return x
```

```python
import functools

import jax
import jax.numpy as jnp
import numpy as np
from jax import lax
from jax.experimental import pallas as pl
from jax.experimental.pallas import tpu as pltpu

D_MODEL = 1024
HEAD_DIM = 64
ATT_Q_HEADS = 6
ATT_KV_HEADS = 2
ATT_GROUP = ATT_Q_HEADS // ATT_KV_HEADS
RET_HEADS = 6
POOL_WINDOWS = (2, 4, 8, 16)
POOL_GROUP_WIDTH = 64
POOL_WIDTH = 256
BLOCK = 128
N_BUCKETS = 32
MAX_DISTANCE = 128
D_FF = 4 * D_MODEL
RMS_EPS = 1e-6
ROPE_BASE = 10000.0
NEG_INF = -1e30

LANES = 128
HALF = LANES // 2
ATT_Q_W = ATT_Q_HEADS * HEAD_DIM
ATT_KV_W = ATT_KV_HEADS * HEAD_DIM
RET_W = RET_HEADS * HEAD_DIM
RET_SLABS = RET_W // LANES
ATT_SLABS = ATT_Q_W // LANES
POOL_TAIL = 16

C_QA = 0
C_KA = C_QA + ATT_Q_W
C_VA = C_KA + ATT_KV_W
C_QR = C_VA + ATT_KV_W
C_KR = C_QR + RET_W
C_VR = C_KR + RET_W
C_GR = C_VR + RET_W
C_UP = C_GR + RET_W
Z_WIDTH = C_UP + POOL_WIDTH

Y_A = 0
Y_R = ATT_Q_W
Y_P = ATT_Q_W + RET_W

MIXER_TILE = 512
MIXER_SUB = 256
PROJ_CHUNK = 256
OUT_CHUNK = 256
MLP_TILE = 512
FF_CHUNK = 1024
VMEM_LIMIT_BYTES = 56 * 1024 * 1024


def _rms_norm(x, g):
    ms = jnp.mean(x * x, axis=-1, keepdims=True)
    return x * lax.rsqrt(ms + RMS_EPS) * g


def _t5_bucket(dist):
    max_exact = N_BUCKETS // 2
    n = np.maximum(dist, 0)
    large = max_exact + (np.log(np.maximum(n, 1) / max_exact)
                         / np.log(MAX_DISTANCE / max_exact)
                         * (N_BUCKETS - max_exact)).astype(np.int64)
    large = np.minimum(large, N_BUCKETS - 1)
    return np.where(n < max_exact, n, large).astype(np.int32)


def _spread(first, second):
    keyed = [((k + 0.5) / len(first), 0, task) for k, task in enumerate(first)]
    keyed += [((k + 0.5) / len(second), 1, task) for k, task in enumerate(second)]
    return [task for _, _, task in sorted(keyed, key=lambda item: item[:2])]


def _mixer_kernel(sinks_ref, x_ref, xn_ref, g_ref, win_ref, bias_ref, cos_ref, sina_ref, sinb_ref, dm_ref,
                  zeta_ref, xi_ref, gtab_ref, wpool_ref, pscale_ref, wout_ref, o_ref,
                  z0_ref, z_ref, y_ref, kbuf, vbuf, ubuf, r_ref, *, tile, sub, layer):
    t = pl.program_id(1)
    npiece = tile // sub
    assert npiece >= 2, "piece 0 of the next tile is projected while the last piece is mixed"
    f32, bf16 = jnp.float32, jnp.bfloat16

    lane = lax.broadcasted_iota(jnp.int32, (BLOCK, LANES), 1)
    lo = lane < HALF

    def zrows(r0):
        return (z0_ref, r0) if r0 < sub else (z_ref, r0 - sub)

    def project_tasks(read_x, zdst, zrow, brow):
        cache = []

        def normed():
            if not cache:
                cache.append(_rms_norm(read_x(), g_ref[...]).astype(bf16))
            return cache[0]

        side = ((C_KA, ATT_KV_W, kbuf, BLOCK), (C_VA, ATT_KV_W, vbuf, BLOCK), (C_UP, POOL_WIDTH, ubuf, POOL_TAIL))
        chunks = [(c0, min(c0 + PROJ_CHUNK, Z_WIDTH)) for c0 in range(0, Z_WIDTH, PROJ_CHUNK)]

        def make(c0, c1):
            def task():
                z = jnp.dot(normed(), win_ref[:, c0:c1], preferred_element_type=f32)
                zdst[zrow:zrow + sub, c0:c1] = z
                for cs, width, buf, head in side:
                    a, b = max(c0, cs), min(c1, cs + width)
                    if a < b:
                        buf[head + brow:head + brow + sub, a - cs:b - cs] = z[:, a - c0:b - c0].astype(buf.dtype)
            return task

        return [make(c0, c1) for c0, c1 in chunks]

    @pl.when(t == 0)
    def _():
        kbuf[0:BLOCK, :] = jnp.zeros((BLOCK, ATT_KV_W), bf16)
        vbuf[0:BLOCK, :] = jnp.zeros((BLOCK, ATT_KV_W), bf16)
        ubuf[0:POOL_TAIL, :] = jnp.zeros((POOL_TAIL, POOL_WIDTH), f32)
        r_ref[...] = jnp.zeros(r_ref.shape, f32)
        for task in project_tasks(lambda: x_ref[0, 0:sub, :], z0_ref, 0, 0):
            task()

    def attention_tasks(r0):
        live = {"ps": [], "denoms": []}

        def scores():
            zr, zo = zrows(r0)
            qs = [zr[zo:zo + BLOCK, C_QA + j * LANES:C_QA + (j + 1) * LANES] for j in range(ATT_SLABS)]
            parts = [jnp.where(lo, q, 0.0) for q in qs] + [jnp.where(lo, 0.0, q) for q in qs]
            qstack = jnp.concatenate(parts, axis=0).astype(bf16)
            kk = kbuf[r0:r0 + 2 * BLOCK, :]
            s = lax.dot_general(qstack, kk, (((1,), (1,)), ((), ())), preferred_element_type=f32)
            live["s"] = s + (bias_ref[jnp.where(t == 0, 1, 0)] if r0 == 0 else bias_ref[0])

        def softmax(heads):
            for hd in heads:
                sh = live["s"][hd * BLOCK:(hd + 1) * BLOCK]
                sink = sinks_ref[layer, hd]
                m = jnp.maximum(jnp.max(sh, axis=-1, keepdims=True), sink)
                p = jnp.exp(sh - m)
                live["denoms"].append(jnp.sum(p, axis=-1, keepdims=True) + jnp.exp(sink - m))
                live["ps"].append(p.astype(bf16))

        def weighted_values():
            denoms = live["denoms"]
            pstack = jnp.concatenate(live["ps"], axis=0)
            o = jnp.dot(pstack, vbuf[r0:r0 + 2 * BLOCK, :], preferred_element_type=f32)
            for j in range(ATT_SLABS):
                o_lo = o[j * BLOCK:(j + 1) * BLOCK] / denoms[j]
                o_hi = o[(j + ATT_GROUP) * BLOCK:(j + ATT_GROUP + 1) * BLOCK] / denoms[j + ATT_GROUP]
                y_ref[r0:r0 + BLOCK, Y_A + j * LANES:Y_A + (j + 1) * LANES] = (
                    jnp.where(lo, o_lo, o_hi).astype(bf16))

        return [scores, functools.partial(softmax, range(0, ATT_GROUP)),
                functools.partial(softmax, range(ATT_GROUP, ATT_Q_HEADS)), weighted_values]

    def rotary(v, cos, sin_a, sin_b):
        return v * cos + pltpu.roll(v, LANES - 1, 1) * sin_a + pltpu.roll(v, 1, 1) * sin_b

    def retention(r0, j):
        zr, zo = zrows(r0)
        cos = cos_ref[r0:r0 + BLOCK, :]
        sin_a = sina_ref[r0:r0 + BLOCK, :]
        sin_b = sinb_ref[r0:r0 + BLOCK, :]
        c0 = j * LANES
        q = rotary(zr[zo:zo + BLOCK, C_QR + c0:C_QR + c0 + LANES], cos, sin_a, sin_b)
        k = rotary(zr[zo:zo + BLOCK, C_KR + c0:C_KR + c0 + LANES], cos, sin_a, sin_b)
        v = zr[zo:zo + BLOCK, C_VR + c0:C_VR + c0 + LANES]
        gate = zr[zo:zo + BLOCK, C_GR + c0:C_GR + c0 + LANES]
        qb = q.astype(bf16)
        kb = k.astype(bf16)
        ksplit = jnp.concatenate([jnp.where(lo, k, 0.0), jnp.where(lo, 0.0, k)], axis=0).astype(bf16)
        sc = lax.dot_general(qb, ksplit, (((1,), (1,)), ((), ())), preferred_element_type=f32)
        acat = (sc * dm_ref[j]).astype(bf16)
        vsplit = jnp.concatenate([jnp.where(lo, v, 0.0), jnp.where(lo, 0.0, v)], axis=0).astype(bf16)
        inner = jnp.dot(acat, vsplit, preferred_element_type=f32)
        r_prev = r_ref[j]
        cross = jnp.dot(qb, r_prev.astype(bf16), preferred_element_type=f32) * xi_ref[j]
        o = inner + cross
        vz = (v * zeta_ref[j]).astype(bf16)
        u = lax.dot_general(kb, vz, (((0,), (0,)), ((), ())), preferred_element_type=f32)
        gt = gtab_ref[j]
        r_ref[j] = r_prev * gt + jnp.where(gt > 0.0, u, 0.0)
        o2 = o * o
        ms_lo = jnp.sum(jnp.where(lo, o2, 0.0), axis=-1, keepdims=True)
        ms_hi = jnp.sum(jnp.where(lo, 0.0, o2), axis=-1, keepdims=True)
        ms = jnp.where(lo, ms_lo, ms_hi) * (1.0 / HEAD_DIM)
        on = o * lax.rsqrt(ms + RMS_EPS)
        y_ref[r0:r0 + BLOCK, Y_R + c0:Y_R + c0 + LANES] = (jax.nn.silu(gate) * on).astype(bf16)

    def pool(r0):
        lo_t = lax.broadcasted_iota(jnp.int32, (sub, LANES), 1) < HALF
        pos1 = lax.broadcasted_iota(jnp.int32, (sub, LANES), 0) + (t * tile + r0 + 1)
        pooled = []
        for sl in range(POOL_WIDTH // LANES):
            w_small, w_big = POOL_WINDOWS[2 * sl], POOL_WINDOWS[2 * sl + 1]
            cols = slice(sl * LANES, (sl + 1) * LANES)

            def shifted(d, cols=cols):
                return ubuf[POOL_TAIL + r0 - d:POOL_TAIL + r0 - d + sub, cols]

            u0 = shifted(0)
            acc = u0
            for d in range(1, w_small):
                acc = acc + shifted(d)
            small = acc
            for d in range(w_small, w_big):
                acc = acc + shifted(d)
            win = jnp.where(lo_t, small, acc)
            cnt = jnp.minimum(pos1, jnp.where(lo_t, w_small, w_big)).astype(f32)
            pooled.append((win / cnt - u0).astype(bf16))
        pooled = jnp.concatenate(pooled, axis=1)
        mixed = jnp.dot(pooled, wpool_ref[...], preferred_element_type=f32) * pscale_ref[...]
        y_ref[r0:r0 + sub, Y_P:Y_P + POOL_WIDTH] = mixed.astype(bf16)

    def output_tasks(r0):
        def make(c0):
            def task():
                o_ref[0, r0:r0 + sub, c0:c0 + OUT_CHUNK] = x_ref[0, r0:r0 + sub, c0:c0 + OUT_CHUNK] + jnp.dot(
                    y_ref[r0:r0 + sub, :], wout_ref[:, c0:c0 + OUT_CHUNK], preferred_element_type=f32)
            return task
        return [make(c0) for c0 in range(0, D_MODEL, OUT_CHUNK)]

    def mix_tasks(r0):
        tasks = []
        for b0 in range(r0, r0 + sub, BLOCK):
            tasks += attention_tasks(b0)
            tasks += [functools.partial(retention, b0, j) for j in range(RET_SLABS)]
        tasks.append(functools.partial(pool, r0))
        return tasks

    for p in range(npiece):
        r0 = p * sub
        if p + 1 < npiece:
            matmuls = project_tasks(lambda r=r0 + sub: x_ref[0, r:r + sub, :], z_ref, r0, r0 + sub)
        else:
            matmuls = project_tasks(lambda: xn_ref[0], z0_ref, 0, tile)
        if p >= 1:
            matmuls = _spread(matmuls, output_tasks(r0 - sub))
        for task in _spread(matmuls, mix_tasks(r0)):
            task()
    for task in output_tasks(tile - sub):
        task()

    kbuf[0:BLOCK + sub, :] = kbuf[tile:tile + BLOCK + sub, :]
    vbuf[0:BLOCK + sub, :] = vbuf[tile:tile + BLOCK + sub, :]
    ubuf[0:POOL_TAIL + sub, :] = ubuf[tile:tile + POOL_TAIL + sub, :]


def _const_spec(shape):
    nd = len(shape)
    return pl.BlockSpec(shape, lambda b, t, _nd=nd: (0,) * _nd)


def _layer_spec(shape, layer):
    nd = len(shape)
    return pl.BlockSpec((None,) + tuple(shape[1:]), lambda b, t, _nd=nd: (layer,) + (0,) * (_nd - 1))


def _mixer_call(layer, x, sinks, g, win, bias, cos, sin_a, sin_b, dm, zeta, xi, gtab, wpool, pscale, wout):
    batch, seq, _ = x.shape
    tile, sub = MIXER_TILE, MIXER_SUB
    grid = (batch, seq // tile)
    last_piece = seq // sub - 1
    f32, bf16 = jnp.float32, jnp.bfloat16
    rot_spec = pl.BlockSpec((tile, LANES), lambda b, t: (t, 0))
    in_specs = [
        pl.BlockSpec(memory_space=pltpu.SMEM),
        pl.BlockSpec((1, tile, D_MODEL), lambda b, t: (b, t, 0)),
        pl.BlockSpec((1, sub, D_MODEL), lambda b, t: (b, jnp.minimum((t + 1) * (tile // sub), last_piece), 0)),
        _layer_spec(g.shape, layer), _layer_spec(win.shape, layer), _const_spec(bias.shape),
        rot_spec, rot_spec, rot_spec,
        _const_spec(dm.shape), _const_spec(zeta.shape), _const_spec(xi.shape), _const_spec(gtab.shape),
        _layer_spec(wpool.shape, layer), _layer_spec(pscale.shape, layer), _layer_spec(wout.shape, layer),
    ]
    return pl.pallas_call(
        functools.partial(_mixer_kernel, tile=tile, sub=sub, layer=layer),
        grid=grid,
        in_specs=in_specs,
        out_specs=pl.BlockSpec((1, tile, D_MODEL), lambda b, t: (b, t, 0)),
        out_shape=jax.ShapeDtypeStruct(x.shape, x.dtype),
        scratch_shapes=[
            pltpu.VMEM((sub, Z_WIDTH), f32),
            pltpu.VMEM((tile - sub, Z_WIDTH), f32),
            pltpu.VMEM((tile, D_MODEL), bf16),
            pltpu.VMEM((BLOCK + tile + sub, ATT_KV_W), bf16),
            pltpu.VMEM((BLOCK + tile + sub, ATT_KV_W), bf16),
            pltpu.VMEM((POOL_TAIL + tile + sub, POOL_WIDTH), f32),
            pltpu.VMEM((RET_SLABS, LANES, LANES), f32),
        ],
        compiler_params=pltpu.CompilerParams(
            dimension_semantics=("arbitrary", "arbitrary"),
            vmem_limit_bytes=VMEM_LIMIT_BYTES),
        name="mixer",
    )(sinks, x, x, g, win, bias, cos, sin_a, sin_b, dm, zeta, xi, gtab, wpool, pscale, wout)


def _mlp_kernel(x_ref, g_ref, wup_ref, wdown_ref, gf_ref, o_ref, *, final_norm):
    f32, bf16 = jnp.float32, jnp.bfloat16
    x = x_ref[0]
    h = _rms_norm(x, g_ref[...]).astype(bf16)
    acc = x
    for c in range(D_FF // FF_CHUNK):
        a = jnp.dot(h, wup_ref[:, c * FF_CHUNK:(c + 1) * FF_CHUNK], preferred_element_type=f32)
        a = jnp.square(jnp.maximum(a, 0.0)).astype(bf16)
        acc = acc + jnp.dot(a, wdown_ref[c * FF_CHUNK:(c + 1) * FF_CHUNK, :], preferred_element_type=f32)
    if final_norm:
        acc = _rms_norm(acc, gf_ref[...])
    o_ref[0] = acc


def _mlp_call(layer, x, g, wup, wdown, gf, final_norm):
    batch, seq, _ = x.shape
    tile = MLP_TILE
    grid = (batch, seq // tile)
    return pl.pallas_call(
        functools.partial(_mlp_kernel, final_norm=final_norm),
        grid=grid,
        in_specs=[
            pl.BlockSpec((1, tile, D_MODEL), lambda b, t: (b, t, 0)),
            _layer_spec(g.shape, layer), _layer_spec(wup.shape, layer), _layer_spec(wdown.shape, layer),
            _const_spec(gf.shape),
        ],
        out_specs=pl.BlockSpec((1, tile, D_MODEL), lambda b, t: (b, t, 0)),
        out_shape=jax.ShapeDtypeStruct(x.shape, x.dtype),
        compiler_params=pltpu.CompilerParams(
            dimension_semantics=("arbitrary", "arbitrary"),
            vmem_limit_bytes=VMEM_LIMIT_BYTES),
        name="mlp",
    )(x, g, wup, wdown, gf)


def _att_slab_order(w, axis):
    heads = [lax.slice_in_dim(w, hd * HEAD_DIM, (hd + 1) * HEAD_DIM, axis=axis) for hd in range(ATT_Q_HEADS)]
    order = []
    for j in range(ATT_SLABS):
        order += [heads[j], heads[j + ATT_GROUP]]
    return jnp.concatenate(order, axis=axis)


def _prep_w_in(w_in):
    w = w_in.astype(jnp.bfloat16)
    qa = _att_slab_order(w[..., 0:ATT_Q_W], axis=2) * jnp.asarray(HEAD_DIM ** -0.5, jnp.bfloat16)
    return jnp.concatenate([qa, w[..., ATT_Q_W:]], axis=-1)


def _prep_w_out(w_out):
    w = w_out.astype(jnp.bfloat16)
    return jnp.concatenate([_att_slab_order(w[:, 0:ATT_Q_W], axis=1), w[:, ATT_Q_W:]], axis=1)


def _attention_bias(rel_bias):
    i = np.arange(BLOCK)[:, None]
    j = np.arange(2 * BLOCK)[None, :]
    dist = BLOCK + i - j
    band = (dist >= 0) & (dist < BLOCK)
    bucket = _t5_bucket(dist)
    table = rel_bias.astype(jnp.float32)
    bias = jnp.zeros((ATT_Q_HEADS, BLOCK, 2 * BLOCK), jnp.float32)
    for b in range(N_BUCKETS):
        bias = jnp.where(jnp.asarray(bucket == b)[None], table[b][:, None, None], bias)
    normal = jnp.where(jnp.asarray(band)[None], bias, NEG_INF)
    first = jnp.where(jnp.asarray(band & (j >= BLOCK))[None], bias, NEG_INF)
    return jnp.stack([normal, first]).reshape(2, ATT_Q_HEADS * BLOCK, 2 * BLOCK)


def _rotary_tables(seq):
    inv = 1.0 / (ROPE_BASE ** jnp.linspace(0.0, 1.0, HEAD_DIM // 2, dtype=jnp.float32))
    ang = jnp.arange(seq).astype(jnp.float32)[:, None] * inv[None, :]
    reps = (1, LANES // HEAD_DIM)
    cos = jnp.tile(jnp.repeat(jnp.cos(ang), 2, axis=-1), reps)
    sin = jnp.tile(jnp.repeat(jnp.sin(ang), 2, axis=-1), reps)
    even = (np.arange(LANES) % 2 == 0)[None, :]
    return cos, jnp.where(even, -sin, 0.0), jnp.where(even, 0.0, sin)


def _retention_tables():
    c = BLOCK
    f32 = np.float32
    scale = f32(HEAD_DIM ** -0.5)
    lg = np.log(f32(1.0) - f32(2.0) ** (f32(-5.0) - np.arange(RET_HEADS, dtype=f32))).astype(f32)
    idx = np.arange(c, dtype=f32)
    diff = idx[:, None] - idx[None, :]
    dmask = np.where(diff >= 0, np.exp(lg[:, None, None] * np.maximum(diff, f32(0.0))), f32(0.0)).astype(f32)
    dm = np.stack([np.concatenate([dmask[2 * j], dmask[2 * j + 1]], axis=1)
                   for j in range(RET_SLABS)]) * scale
    zeta = (np.exp(lg[:, None] * (f32(c) - f32(1.0) - idx)[None, :]) * scale).astype(f32)
    xi = np.exp(lg[None, :] * (idx[:, None] + f32(1.0))).astype(f32)
    g_chunk = np.exp(lg * f32(c)).astype(f32)
    head_of_lane = (np.arange(LANES) >= HALF).astype(np.int32)
    zeta_t = np.stack([zeta[2 * j + head_of_lane].T for j in range(RET_SLABS)])
    xi_t = np.stack([xi[:, 2 * j + head_of_lane] for j in range(RET_SLABS)])
    same = head_of_lane[:, None] == head_of_lane[None, :]
    gtab = np.stack([np.where(same, g_chunk[2 * j + head_of_lane][:, None], f32(0.0))
                     for j in range(RET_SLABS)]).astype(f32)
    return jnp.asarray(dm), jnp.asarray(zeta_t), jnp.asarray(xi_t), jnp.asarray(gtab)


def _pool_block_diag(pool_w):
    depth, groups = pool_w.shape[0], pool_w.shape[1]
    eye = jnp.eye(groups, dtype=jnp.bfloat16)[None, :, None, :, None]
    wide = pool_w.astype(jnp.bfloat16)[:, :, :, None, :] * eye
    return wide.reshape(depth, POOL_WIDTH, POOL_WIDTH)


def kernel(x, attn_norm_g, w_in, attn_sinks, rel_bias, pool_w, pool_scale, w_out, mlp_norm_g, w_up, w_down,
           final_norm_g):
    depth = w_in.shape[0]
    seq = x.shape[1]
    bf16 = jnp.bfloat16
    w_in_b = _prep_w_in(w_in)
    w_out_b = _prep_w_out(w_out)
    w_pool = _pool_block_diag(pool_w)
    w_up_b = w_up.astype(bf16)
    w_down_b = w_down.astype(bf16)
    bias = _attention_bias(rel_bias)
    cos, sin_a, sin_b = _rotary_tables(seq)
    dm, zeta_t, xi_t, gtab = _retention_tables()
    g_attn = attn_norm_g.reshape(depth, 1, D_MODEL)
    g_mlp = mlp_norm_g.reshape(depth, 1, D_MODEL)
    p_scale = pool_scale.reshape(depth, 1, POOL_WIDTH)
    gf = final_norm_g.reshape(1, D_MODEL)
    for layer in range(depth):
        x = _mixer_call(layer, x, attn_sinks, g_attn, w_in_b, bias, cos, sin_a, sin_b, dm, zeta_t, xi_t, gtab,
                        w_pool, p_scale, w_out_b)
        x = _mlp_call(layer, x, g_mlp, w_up_b, w_down_b, gf, final_norm=(layer == depth - 1))
    return x
```

```python
import functools

import jax
import jax.numpy as jnp
import numpy as np
from jax import lax
from jax.experimental import pallas as pl
from jax.experimental.pallas import tpu as pltpu

D_MODEL = 1024
HEAD_DIM = 64
ATT_Q_HEADS = 6
ATT_KV_HEADS = 2
ATT_GROUP = ATT_Q_HEADS // ATT_KV_HEADS
RET_HEADS = 6
POOL_WINDOWS = (2, 4, 8, 16)
POOL_GROUP_WIDTH = 64
POOL_WIDTH = 256
BLOCK = 128
N_BUCKETS = 32
MAX_DISTANCE = 128
D_FF = 4 * D_MODEL
RMS_EPS = 1e-6
ROPE_BASE = 10000.0
NEG_INF = -1e30

LANES = 128
HALF = LANES // 2
ATT_Q_W = ATT_Q_HEADS * HEAD_DIM
ATT_KV_W = ATT_KV_HEADS * HEAD_DIM
RET_W = RET_HEADS * HEAD_DIM
RET_SLABS = RET_W // LANES
ATT_SLABS = ATT_Q_W // LANES
POOL_TAIL = 16

C_QA = 0
C_KA = C_QA + ATT_Q_W
C_VA = C_KA + ATT_KV_W
C_QR = C_VA + ATT_KV_W
C_KR = C_QR + RET_W
C_VR = C_KR + RET_W
C_GR = C_VR + RET_W
C_UP = C_GR + RET_W
Z_WIDTH = C_UP + POOL_WIDTH

Y_A = 0
Y_R = ATT_Q_W
Y_P = ATT_Q_W + RET_W

MIXER_TILE = 512
MIXER_SUB = 256
PROJ_CHUNK = 256
OUT_CHUNK = 256
MLP_TILE = 512
FF_CHUNK = 1024
VMEM_LIMIT_BYTES = 56 * 1024 * 1024


def _rms_norm(x, g):
    ms = jnp.mean(x * x, axis=-1, keepdims=True)
    return x * lax.rsqrt(ms + RMS_EPS) * g


def _t5_bucket(dist):
    max_exact = N_BUCKETS // 2
    n = np.maximum(dist, 0)
    large = max_exact + (np.log(np.maximum(n, 1) / max_exact)
                         / np.log(MAX_DISTANCE / max_exact)
                         * (N_BUCKETS - max_exact)).astype(np.int64)
    large = np.minimum(large, N_BUCKETS - 1)
    return np.where(n < max_exact, n, large).astype(np.int32)


def _spread(first, second):
    keyed = [((k + 0.5) / len(first), 0, task) for k, task in enumerate(first)]
    keyed += [((k + 0.5) / len(second), 1, task) for k, task in enumerate(second)]
    return [task for _, _, task in sorted(keyed, key=lambda item: item[:2])]


def _mixer_kernel(sinks_ref, x_ref, xn_ref, g_ref, win_ref, bias_ref, cos_ref, sina_ref, sinb_ref, dm_ref,
                  zeta_ref, xi_ref, gtab_ref, wpool_ref, pscale_ref, wout_ref, o_ref,
                  z0_ref, z_ref, y_ref, kbuf, vbuf, ubuf, r_ref, *, tile, sub, layer):
    t = pl.program_id(1)
    npiece = tile // sub
    assert npiece >= 2, "piece 0 of the next tile is projected while the last piece is mixed"
    f32, bf16 = jnp.float32, jnp.bfloat16

    lane = lax.broadcasted_iota(jnp.int32, (BLOCK, LANES), 1)
    lo = lane < HALF

    def zrows(r0):
        return (z0_ref, r0) if r0 < sub else (z_ref, r0 - sub)

    def project_tasks(read_x, zdst, zrow, brow):
        cache = []

        def normed():
            if not cache:
                cache.append(_rms_norm(read_x(), g_ref[...]).astype(bf16))
            return cache[0]

        side = ((C_KA, ATT_KV_W, kbuf, BLOCK), (C_VA, ATT_KV_W, vbuf, BLOCK), (C_UP, POOL_WIDTH, ubuf, POOL_TAIL))
        chunks = [(c0, min(c0 + PROJ_CHUNK, Z_WIDTH)) for c0 in range(0, Z_WIDTH, PROJ_CHUNK)]

        def make(c0, c1):
            def task():
                z = jnp.dot(normed(), win_ref[:, c0:c1], preferred_element_type=f32)
                zdst[zrow:zrow + sub, c0:c1] = z
                for cs, width, buf, head in side:
                    a, b = max(c0, cs), min(c1, cs + width)
                    if a < b:
                        buf[head + brow:head + brow + sub, a - cs:b - cs] = z[:, a - c0:b - c0].astype(buf.dtype)
            return task

        return [make(c0, c1) for c0, c1 in chunks]

    @pl.when(t == 0)
    def _():
        kbuf[0:BLOCK, :] = jnp.zeros((BLOCK, ATT_KV_W), bf16)
        vbuf[0:BLOCK, :] = jnp.zeros((BLOCK, ATT_KV_W), bf16)
        ubuf[0:POOL_TAIL, :] = jnp.zeros((POOL_TAIL, POOL_WIDTH), f32)
        r_ref[...] = jnp.zeros(r_ref.shape, f32)
        for task in project_tasks(lambda: x_ref[0, 0:sub, :], z0_ref, 0, 0):
            task()

    def attention_tasks(r0):
        live = {"ps": [], "denoms": []}

        def swap_halves(v):
            return pltpu.roll(v, HALF, 1)

        def scores():
            zr, zo = zrows(r0)
            qs = [zr[zo:zo + BLOCK, C_QA + j * LANES:C_QA + (j + 1) * LANES] for j in range(ATT_SLABS)]
            swapped = {}
            parts = []
            for hd in range(ATT_Q_HEADS):
                slab, half, group = hd // 2, hd % 2, hd // ATT_GROUP
                q = qs[slab]
                if half != group:
                    q = swapped.setdefault(slab, swap_halves(q))
                parts.append(jnp.where(lo, q, 0.0) if group == 0 else jnp.where(lo, 0.0, q))
            qstack = jnp.concatenate(parts, axis=0).astype(bf16)
            kk = kbuf[r0:r0 + 2 * BLOCK, :]
            s = lax.dot_general(qstack, kk, (((1,), (1,)), ((), ())), preferred_element_type=f32)
            live["s"] = s + (bias_ref[jnp.where(t == 0, 1, 0)] if r0 == 0 else bias_ref[0])

        def softmax(heads):
            for hd in heads:
                sh = live["s"][hd * BLOCK:(hd + 1) * BLOCK]
                sink = sinks_ref[layer, hd]
                m = jnp.maximum(jnp.max(sh, axis=-1, keepdims=True), sink)
                p = jnp.exp(sh - m)
                live["denoms"].append(jnp.sum(p, axis=-1, keepdims=True) + jnp.exp(sink - m))
                live["ps"].append(p.astype(bf16))

        def weighted_values():
            denoms = live["denoms"]
            pstack = jnp.concatenate(live["ps"], axis=0)
            o = jnp.dot(pstack, vbuf[r0:r0 + 2 * BLOCK, :], preferred_element_type=f32)
            halves = []
            for hd in range(ATT_Q_HEADS):
                oh = o[hd * BLOCK:(hd + 1) * BLOCK] / denoms[hd]
                halves.append(oh if hd % 2 == hd // ATT_GROUP else swap_halves(oh))
            for j in range(ATT_SLABS):
                y_ref[r0:r0 + BLOCK, Y_A + j * LANES:Y_A + (j + 1) * LANES] = (
                    jnp.where(lo, halves[2 * j], halves[2 * j + 1]).astype(bf16))

        return [scores, functools.partial(softmax, range(0, ATT_GROUP)),
                functools.partial(softmax, range(ATT_GROUP, ATT_Q_HEADS)), weighted_values]

    def rotary(v, cos, sin_a, sin_b):
        return v * cos + pltpu.roll(v, LANES - 1, 1) * sin_a + pltpu.roll(v, 1, 1) * sin_b

    def retention(r0, j):
        zr, zo = zrows(r0)
        cos = cos_ref[r0:r0 + BLOCK, :]
        sin_a = sina_ref[r0:r0 + BLOCK, :]
        sin_b = sinb_ref[r0:r0 + BLOCK, :]
        c0 = j * LANES
        q = rotary(zr[zo:zo + BLOCK, C_QR + c0:C_QR + c0 + LANES], cos, sin_a, sin_b)
        k = rotary(zr[zo:zo + BLOCK, C_KR + c0:C_KR + c0 + LANES], cos, sin_a, sin_b)
        v = zr[zo:zo + BLOCK, C_VR + c0:C_VR + c0 + LANES]
        gate = zr[zo:zo + BLOCK, C_GR + c0:C_GR + c0 + LANES]
        qb = q.astype(bf16)
        kb = k.astype(bf16)
        ksplit = jnp.concatenate([jnp.where(lo, k, 0.0), jnp.where(lo, 0.0, k)], axis=0).astype(bf16)
        sc = lax.dot_general(qb, ksplit, (((1,), (1,)), ((), ())), preferred_element_type=f32)
        acat = (sc * dm_ref[j]).astype(bf16)
        vsplit = jnp.concatenate([jnp.where(lo, v, 0.0), jnp.where(lo, 0.0, v)], axis=0).astype(bf16)
        inner = jnp.dot(acat, vsplit, preferred_element_type=f32)
        r_prev = r_ref[j]
        cross = jnp.dot(qb, r_prev.astype(bf16), preferred_element_type=f32) * xi_ref[j]
        o = inner + cross
        vz = (v * zeta_ref[j]).astype(bf16)
        u = lax.dot_general(kb, vz, (((0,), (0,)), ((), ())), preferred_element_type=f32)
        gt = gtab_ref[j]
        r_ref[j] = r_prev * gt + jnp.where(gt > 0.0, u, 0.0)
        o2 = o * o
        ms_lo = jnp.sum(jnp.where(lo, o2, 0.0), axis=-1, keepdims=True)
        ms_hi = jnp.sum(jnp.where(lo, 0.0, o2), axis=-1, keepdims=True)
        ms = jnp.where(lo, ms_lo, ms_hi) * (1.0 / HEAD_DIM)
        on = o * lax.rsqrt(ms + RMS_EPS)
        y_ref[r0:r0 + BLOCK, Y_R + c0:Y_R + c0 + LANES] = (jax.nn.silu(gate) * on).astype(bf16)

    def pool(r0):
        lo_t = lax.broadcasted_iota(jnp.int32, (sub, LANES), 1) < HALF
        pos1 = lax.broadcasted_iota(jnp.int32, (sub, LANES), 0) + (t * tile + r0 + 1)
        pooled = []
        for sl in range(POOL_WIDTH // LANES):
            w_small, w_big = POOL_WINDOWS[2 * sl], POOL_WINDOWS[2 * sl + 1]
            cols = slice(sl * LANES, (sl + 1) * LANES)

            def shifted(d, cols=cols):
                return ubuf[POOL_TAIL + r0 - d:POOL_TAIL + r0 - d + sub, cols]

            u0 = shifted(0)
            acc = u0
            for d in range(1, w_small):
                acc = acc + shifted(d)
            small = acc
            for d in range(w_small, w_big):
                acc = acc + shifted(d)
            win = jnp.where(lo_t, small, acc)
            cnt = jnp.minimum(pos1, jnp.where(lo_t, w_small, w_big)).astype(f32)
            pooled.append((win / cnt - u0).astype(bf16))
        pooled = jnp.concatenate(pooled, axis=1)
        mixed = jnp.dot(pooled, wpool_ref[...], preferred_element_type=f32) * pscale_ref[...]
        y_ref[r0:r0 + sub, Y_P:Y_P + POOL_WIDTH] = mixed.astype(bf16)

    def output_tasks(r0):
        def make(c0):
            def task():
                o_ref[0, r0:r0 + sub, c0:c0 + OUT_CHUNK] = x_ref[0, r0:r0 + sub, c0:c0 + OUT_CHUNK] + jnp.dot(
                    y_ref[r0:r0 + sub, :], wout_ref[:, c0:c0 + OUT_CHUNK], preferred_element_type=f32)
            return task
        return [make(c0) for c0 in range(0, D_MODEL, OUT_CHUNK)]

    def mix_tasks(r0):
        tasks = []
        for b0 in range(r0, r0 + sub, BLOCK):
            tasks += attention_tasks(b0)
            tasks += [functools.partial(retention, b0, j) for j in range(RET_SLABS)]
        tasks.append(functools.partial(pool, r0))
        return tasks

    for p in range(npiece):
        r0 = p * sub
        if p + 1 < npiece:
            matmuls = project_tasks(lambda r=r0 + sub: x_ref[0, r:r + sub, :], z_ref, r0, r0 + sub)
        else:
            matmuls = project_tasks(lambda: xn_ref[0], z0_ref, 0, tile)
        if p >= 1:
            matmuls = _spread(matmuls, output_tasks(r0 - sub))
        for task in _spread(matmuls, mix_tasks(r0)):
            task()
    for task in output_tasks(tile - sub):
        task()

    kbuf[0:BLOCK + sub, :] = kbuf[tile:tile + BLOCK + sub, :]
    vbuf[0:BLOCK + sub, :] = vbuf[tile:tile + BLOCK + sub, :]
    ubuf[0:POOL_TAIL + sub, :] = ubuf[tile:tile + POOL_TAIL + sub, :]


def _const_spec(shape):
    nd = len(shape)
    return pl.BlockSpec(shape, lambda b, t, _nd=nd: (0,) * _nd)


def _layer_spec(shape, layer):
    nd = len(shape)
    return pl.BlockSpec((None,) + tuple(shape[1:]), lambda b, t, _nd=nd: (layer,) + (0,) * (_nd - 1))


def _mixer_call(layer, x, sinks, g, win, bias, cos, sin_a, sin_b, dm, zeta, xi, gtab, wpool, pscale, wout):
    batch, seq, _ = x.shape
    tile, sub = MIXER_TILE, MIXER_SUB
    grid = (batch, seq // tile)
    last_piece = seq // sub - 1
    f32, bf16 = jnp.float32, jnp.bfloat16
    rot_spec = pl.BlockSpec((tile, LANES), lambda b, t: (t, 0))
    in_specs = [
        pl.BlockSpec(memory_space=pltpu.SMEM),
        pl.BlockSpec((1, tile, D_MODEL), lambda b, t: (b, t, 0)),
        pl.BlockSpec((1, sub, D_MODEL), lambda b, t: (b, jnp.minimum((t + 1) * (tile // sub), last_piece), 0)),
        _layer_spec(g.shape, layer), _layer_spec(win.shape, layer), _const_spec(bias.shape),
        rot_spec, rot_spec, rot_spec,
        _const_spec(dm.shape), _const_spec(zeta.shape), _const_spec(xi.shape), _const_spec(gtab.shape),
        _layer_spec(wpool.shape, layer), _layer_spec(pscale.shape, layer), _layer_spec(wout.shape, layer),
    ]
    return pl.pallas_call(
        functools.partial(_mixer_kernel, tile=tile, sub=sub, layer=layer),
        grid=grid,
        in_specs=in_specs,
        out_specs=pl.BlockSpec((1, tile, D_MODEL), lambda b, t: (b, t, 0)),
        out_shape=jax.ShapeDtypeStruct(x.shape, x.dtype),
        scratch_shapes=[
            pltpu.VMEM((sub, Z_WIDTH), f32),
            pltpu.VMEM((tile - sub, Z_WIDTH), f32),
            pltpu.VMEM((tile, D_MODEL), bf16),
            pltpu.VMEM((BLOCK + tile + sub, ATT_KV_W), bf16),
            pltpu.VMEM((BLOCK + tile + sub, ATT_KV_W), bf16),
            pltpu.VMEM((POOL_TAIL + tile + sub, POOL_WIDTH), f32),
            pltpu.VMEM((RET_SLABS, LANES, LANES), f32),
        ],
        compiler_params=pltpu.CompilerParams(
            dimension_semantics=("arbitrary", "arbitrary"),
            vmem_limit_bytes=VMEM_LIMIT_BYTES),
        name="mixer",
    )(sinks, x, x, g, win, bias, cos, sin_a, sin_b, dm, zeta, xi, gtab, wpool, pscale, wout)


def _mlp_kernel(x_ref, g_ref, wup_ref, wdown_ref, gf_ref, o_ref, *, final_norm):
    f32, bf16 = jnp.float32, jnp.bfloat16
    x = x_ref[0]
    h = _rms_norm(x, g_ref[...]).astype(bf16)
    acc = x
    for c in range(D_FF // FF_CHUNK):
        a = jnp.dot(h, wup_ref[:, c * FF_CHUNK:(c + 1) * FF_CHUNK], preferred_element_type=f32)
        a = jnp.square(jnp.maximum(a, 0.0)).astype(bf16)
        acc = acc + jnp.dot(a, wdown_ref[c * FF_CHUNK:(c + 1) * FF_CHUNK, :], preferred_element_type=f32)
    if final_norm:
        acc = _rms_norm(acc, gf_ref[...])
    o_ref[0] = acc


def _mlp_call(layer, x, g, wup, wdown, gf, final_norm):
    batch, seq, _ = x.shape
    tile = MLP_TILE
    grid = (batch, seq // tile)
    return pl.pallas_call(
        functools.partial(_mlp_kernel, final_norm=final_norm),
        grid=grid,
        in_specs=[
            pl.BlockSpec((1, tile, D_MODEL), lambda b, t: (b, t, 0)),
            _layer_spec(g.shape, layer), _layer_spec(wup.shape, layer), _layer_spec(wdown.shape, layer),
            _const_spec(gf.shape),
        ],
        out_specs=pl.BlockSpec((1, tile, D_MODEL), lambda b, t: (b, t, 0)),
        out_shape=jax.ShapeDtypeStruct(x.shape, x.dtype),
        compiler_params=pltpu.CompilerParams(
            dimension_semantics=("arbitrary", "arbitrary"),
            vmem_limit_bytes=VMEM_LIMIT_BYTES),
        name="mlp",
    )(x, g, wup, wdown, gf)


def _prep_w_in(w_in):
    col_scale = np.ones((Z_WIDTH,), np.float32)
    col_scale[C_QA:C_QA + ATT_Q_W] = HEAD_DIM ** -0.5
    return (w_in * col_scale).astype(jnp.bfloat16)


def _attention_bias(rel_bias):
    i = np.arange(BLOCK)[:, None]
    j = np.arange(2 * BLOCK)[None, :]
    dist = BLOCK + i - j
    band = (dist >= 0) & (dist < BLOCK)
    onehot = (_t5_bucket(dist).reshape(1, -1) == np.arange(N_BUCKETS)[:, None]).astype(np.float32)
    bias = jnp.dot(rel_bias.astype(jnp.float32).T, onehot, precision=lax.Precision.HIGHEST)
    bias = bias.reshape(ATT_Q_HEADS, BLOCK, 2 * BLOCK)
    normal = jnp.where(jnp.asarray(band)[None], bias, NEG_INF)
    first = jnp.where(jnp.asarray(band & (j >= BLOCK))[None], bias, NEG_INF)
    return jnp.stack([normal, first]).reshape(2, ATT_Q_HEADS * BLOCK, 2 * BLOCK)


def _rotary_tables(seq):
    inv = 1.0 / (ROPE_BASE ** jnp.linspace(0.0, 1.0, HEAD_DIM // 2, dtype=jnp.float32))
    ang = jnp.arange(seq).astype(jnp.float32)[:, None] * inv[None, :]
    reps = (1, LANES // HEAD_DIM)
    cos = jnp.tile(jnp.repeat(jnp.cos(ang), 2, axis=-1), reps)
    sin = jnp.tile(jnp.repeat(jnp.sin(ang), 2, axis=-1), reps)
    even = (np.arange(LANES) % 2 == 0)[None, :]
    return cos, jnp.where(even, -sin, 0.0), jnp.where(even, 0.0, sin)


def _retention_tables():
    c = BLOCK
    f32 = np.float32
    scale = f32(HEAD_DIM ** -0.5)
    lg = np.log(f32(1.0) - f32(2.0) ** (f32(-5.0) - np.arange(RET_HEADS, dtype=f32))).astype(f32)
    idx = np.arange(c, dtype=f32)
    diff = idx[:, None] - idx[None, :]
    dmask = np.where(diff >= 0, np.exp(lg[:, None, None] * np.maximum(diff, f32(0.0))), f32(0.0)).astype(f32)
    dm = np.stack([np.concatenate([dmask[2 * j], dmask[2 * j + 1]], axis=1)
                   for j in range(RET_SLABS)]) * scale
    zeta = (np.exp(lg[:, None] * (f32(c) - f32(1.0) - idx)[None, :]) * scale).astype(f32)
    xi = np.exp(lg[None, :] * (idx[:, None] + f32(1.0))).astype(f32)
    g_chunk = np.exp(lg * f32(c)).astype(f32)
    head_of_lane = (np.arange(LANES) >= HALF).astype(np.int32)
    zeta_t = np.stack([zeta[2 * j + head_of_lane].T for j in range(RET_SLABS)])
    xi_t = np.stack([xi[:, 2 * j + head_of_lane] for j in range(RET_SLABS)])
    same = head_of_lane[:, None] == head_of_lane[None, :]
    gtab = np.stack([np.where(same, g_chunk[2 * j + head_of_lane][:, None], f32(0.0))
                     for j in range(RET_SLABS)]).astype(f32)
    return jnp.asarray(dm), jnp.asarray(zeta_t), jnp.asarray(xi_t), jnp.asarray(gtab)


def _pool_block_diag(pool_w):
    depth, groups = pool_w.shape[0], pool_w.shape[1]
    eye = jnp.eye(groups, dtype=jnp.bfloat16)[None, :, None, :, None]
    wide = pool_w.astype(jnp.bfloat16)[:, :, :, None, :] * eye
    return wide.reshape(depth, POOL_WIDTH, POOL_WIDTH)


def kernel(x, attn_norm_g, w_in, attn_sinks, rel_bias, pool_w, pool_scale, w_out, mlp_norm_g, w_up, w_down,
           final_norm_g):
    depth = w_in.shape[0]
    seq = x.shape[1]
    bf16 = jnp.bfloat16
    w_in_b = _prep_w_in(w_in)
    w_out_b = w_out.astype(bf16)
    w_pool = _pool_block_diag(pool_w)
    w_up_b = w_up.astype(bf16)
    w_down_b = w_down.astype(bf16)
    bias = _attention_bias(rel_bias)
    cos, sin_a, sin_b = _rotary_tables(seq)
    dm, zeta_t, xi_t, gtab = _retention_tables()
    g_attn = attn_norm_g.reshape(depth, 1, D_MODEL)
    g_mlp = mlp_norm_g.reshape(depth, 1, D_MODEL)
    p_scale = pool_scale.reshape(depth, 1, POOL_WIDTH)
    gf = final_norm_g.reshape(1, D_MODEL)
    for layer in range(depth):
        x = _mixer_call(layer, x, attn_sinks, g_attn, w_in_b, bias, cos, sin_a, sin_b, dm, zeta_t, xi_t, gtab,
                        w_pool, p_scale, w_out_b)
        x = _mlp_call(layer, x, g_mlp, w_up_b, w_down_b, gf, final_norm=(layer == depth - 1))
    return x
```

```python
import functools

import jax
import jax.numpy as jnp
import numpy as np
from jax import lax
from jax.experimental import pallas as pl
from jax.experimental.pallas import tpu as pltpu

D_MODEL = 1024
HEAD_DIM = 64
ATT_Q_HEADS = 6
ATT_KV_HEADS = 2
ATT_GROUP = ATT_Q_HEADS // ATT_KV_HEADS
RET_HEADS = 6
POOL_WINDOWS = (2, 4, 8, 16)
POOL_GROUP_WIDTH = 64
POOL_WIDTH = 256
BLOCK = 128
N_BUCKETS = 32
MAX_DISTANCE = 128
D_FF = 4 * D_MODEL
RMS_EPS = 1e-6
ROPE_BASE = 10000.0
NEG_INF = -1e30

LANES = 128
HALF = LANES // 2
ATT_Q_W = ATT_Q_HEADS * HEAD_DIM
ATT_KV_W = ATT_KV_HEADS * HEAD_DIM
RET_W = RET_HEADS * HEAD_DIM
RET_SLABS = RET_W // LANES
ATT_SLABS = ATT_Q_W // LANES
POOL_TAIL = 16

C_QA = 0
C_KA = C_QA + ATT_Q_W
C_VA = C_KA + ATT_KV_W
C_QR = C_VA + ATT_KV_W
C_KR = C_QR + RET_W
C_VR = C_KR + RET_W
C_GR = C_VR + RET_W
C_UP = C_GR + RET_W
Z_WIDTH = C_UP + POOL_WIDTH

Y_A = 0
Y_R = ATT_Q_W
Y_P = ATT_Q_W + RET_W

MIXER_TILE = 512
MIXER_SUB = 256
PROJ_CHUNK = 256
OUT_CHUNK = 256
MLP_TILE = 1024
FF_CHUNK = 1024
VMEM_LIMIT_BYTES = 56 * 1024 * 1024


def _rms_norm(x, g):
    ms = jnp.mean(x * x, axis=-1, keepdims=True)
    return x * lax.rsqrt(ms + RMS_EPS) * g


def _t5_bucket(dist):
    max_exact = N_BUCKETS // 2
    n = np.maximum(dist, 0)
    large = max_exact + (np.log(np.maximum(n, 1) / max_exact)
                         / np.log(MAX_DISTANCE / max_exact)
                         * (N_BUCKETS - max_exact)).astype(np.int64)
    large = np.minimum(large, N_BUCKETS - 1)
    return np.where(n < max_exact, n, large).astype(np.int32)


def _spread(first, second):
    keyed = [((k + 0.5) / len(first), 0, task) for k, task in enumerate(first)]
    keyed += [((k + 0.5) / len(second), 1, task) for k, task in enumerate(second)]
    return [task for _, _, task in sorted(keyed, key=lambda item: item[:2])]


def _mixer_kernel(sinks_ref, x_ref, xn_ref, g_ref, win_ref, bias_ref, cos_ref, sina_ref, sinb_ref, dm_ref,
                  zeta_ref, xi_ref, gtab_ref, wpool_ref, pscale_ref, wout_ref, o_ref,
                  z0_ref, z_ref, y_ref, kbuf, vbuf, ubuf, r_ref, *, tile, sub, layer):
    t = pl.program_id(1)
    npiece = tile // sub
    assert npiece >= 2, "piece 0 of the next tile is projected while the last piece is mixed"
    f32, bf16 = jnp.float32, jnp.bfloat16

    lane = lax.broadcasted_iota(jnp.int32, (BLOCK, LANES), 1)
    lo = lane < HALF

    def zrows(r0):
        return (z0_ref, r0) if r0 < sub else (z_ref, r0 - sub)

    def project_tasks(read_x, zdst, zrow, brow):
        cache = []

        def normed():
            if not cache:
                cache.append(_rms_norm(read_x(), g_ref[...]).astype(bf16))
            return cache[0]

        side = ((C_KA, ATT_KV_W, kbuf, BLOCK), (C_VA, ATT_KV_W, vbuf, BLOCK), (C_UP, POOL_WIDTH, ubuf, POOL_TAIL))
        chunks = [(c0, min(c0 + PROJ_CHUNK, Z_WIDTH)) for c0 in range(0, Z_WIDTH, PROJ_CHUNK)]

        def make(c0, c1):
            def task():
                z = jnp.dot(normed(), win_ref[:, c0:c1], preferred_element_type=f32)
                zdst[zrow:zrow + sub, c0:c1] = z
                for cs, width, buf, head in side:
                    a, b = max(c0, cs), min(c1, cs + width)
                    if a < b:
                        buf[head + brow:head + brow + sub, a - cs:b - cs] = z[:, a - c0:b - c0].astype(buf.dtype)
            return task

        return [make(c0, c1) for c0, c1 in chunks]

    @pl.when(t == 0)
    def _():
        kbuf[0:BLOCK, :] = jnp.zeros((BLOCK, ATT_KV_W), bf16)
        vbuf[0:BLOCK, :] = jnp.zeros((BLOCK, ATT_KV_W), bf16)
        ubuf[0:POOL_TAIL, :] = jnp.zeros((POOL_TAIL, POOL_WIDTH), f32)
        r_ref[...] = jnp.zeros(r_ref.shape, f32)
        for task in project_tasks(lambda: x_ref[0, 0:sub, :], z0_ref, 0, 0):
            task()

    def attention_tasks(r0):
        live = {"ps": [], "denoms": []}

        def swap_halves(v):
            return pltpu.roll(v, HALF, 1)

        def scores():
            zr, zo = zrows(r0)
            qs = [zr[zo:zo + BLOCK, C_QA + j * LANES:C_QA + (j + 1) * LANES] for j in range(ATT_SLABS)]
            swapped = {}
            parts = []
            for hd in range(ATT_Q_HEADS):
                slab, half, group = hd // 2, hd % 2, hd // ATT_GROUP
                q = qs[slab]
                if half != group:
                    q = swapped.setdefault(slab, swap_halves(q))
                parts.append(jnp.where(lo, q, 0.0) if group == 0 else jnp.where(lo, 0.0, q))
            qstack = jnp.concatenate(parts, axis=0).astype(bf16)
            kk = kbuf[r0:r0 + 2 * BLOCK, :]
            s = lax.dot_general(qstack, kk, (((1,), (1,)), ((), ())), preferred_element_type=f32)
            live["s"] = s + (bias_ref[jnp.where(t == 0, 1, 0)] if r0 == 0 else bias_ref[0])

        def softmax(heads):
            for hd in heads:
                sh = live["s"][hd * BLOCK:(hd + 1) * BLOCK]
                sink = sinks_ref[layer, hd]
                m = jnp.maximum(jnp.max(sh, axis=-1, keepdims=True), sink)
                p = jnp.exp(sh - m)
                live["denoms"].append(jnp.sum(p, axis=-1, keepdims=True) + jnp.exp(sink - m))
                live["ps"].append(p.astype(bf16))

        def weighted_values():
            denoms = live["denoms"]
            pstack = jnp.concatenate(live["ps"], axis=0)
            o = jnp.dot(pstack, vbuf[r0:r0 + 2 * BLOCK, :], preferred_element_type=f32)
            halves = []
            for hd in range(ATT_Q_HEADS):
                oh = o[hd * BLOCK:(hd + 1) * BLOCK] / denoms[hd]
                halves.append(oh if hd % 2 == hd // ATT_GROUP else swap_halves(oh))
            for j in range(ATT_SLABS):
                y_ref[r0:r0 + BLOCK, Y_A + j * LANES:Y_A + (j + 1) * LANES] = (
                    jnp.where(lo, halves[2 * j], halves[2 * j + 1]).astype(bf16))

        return [scores, functools.partial(softmax, range(0, ATT_GROUP)),
                functools.partial(softmax, range(ATT_GROUP, ATT_Q_HEADS)), weighted_values]

    def rotary(v, cos, sin_a, sin_b):
        return v * cos + pltpu.roll(v, LANES - 1, 1) * sin_a + pltpu.roll(v, 1, 1) * sin_b

    def retention(r0, j):
        zr, zo = zrows(r0)
        cos = cos_ref[r0:r0 + BLOCK, :]
        sin_a = sina_ref[r0:r0 + BLOCK, :]
        sin_b = sinb_ref[r0:r0 + BLOCK, :]
        c0 = j * LANES
        q = rotary(zr[zo:zo + BLOCK, C_QR + c0:C_QR + c0 + LANES], cos, sin_a, sin_b)
        k = rotary(zr[zo:zo + BLOCK, C_KR + c0:C_KR + c0 + LANES], cos, sin_a, sin_b)
        v = zr[zo:zo + BLOCK, C_VR + c0:C_VR + c0 + LANES]
        gate = zr[zo:zo + BLOCK, C_GR + c0:C_GR + c0 + LANES]
        qb = q.astype(bf16)
        kb = k.astype(bf16)
        ksplit = jnp.concatenate([jnp.where(lo, k, 0.0), jnp.where(lo, 0.0, k)], axis=0).astype(bf16)
        sc = lax.dot_general(qb, ksplit, (((1,), (1,)), ((), ())), preferred_element_type=f32)
        acat = (sc * dm_ref[j]).astype(bf16)
        vsplit = jnp.concatenate([jnp.where(lo, v, 0.0), jnp.where(lo, 0.0, v)], axis=0).astype(bf16)
        inner = jnp.dot(acat, vsplit, preferred_element_type=f32)
        r_prev = r_ref[j]
        cross = jnp.dot(qb, r_prev.astype(bf16), preferred_element_type=f32) * xi_ref[j]
        o = inner + cross
        vz = (v * zeta_ref[j]).astype(bf16)
        u = lax.dot_general(kb, vz, (((0,), (0,)), ((), ())), preferred_element_type=f32)
        gt = gtab_ref[j]
        r_ref[j] = r_prev * gt + jnp.where(gt > 0.0, u, 0.0)
        o2 = o * o
        ms_lo = jnp.sum(jnp.where(lo, o2, 0.0), axis=-1, keepdims=True)
        ms_hi = jnp.sum(jnp.where(lo, 0.0, o2), axis=-1, keepdims=True)
        ms = jnp.where(lo, ms_lo, ms_hi) * (1.0 / HEAD_DIM)
        on = o * lax.rsqrt(ms + RMS_EPS)
        y_ref[r0:r0 + BLOCK, Y_R + c0:Y_R + c0 + LANES] = (jax.nn.silu(gate) * on).astype(bf16)

    def pool(r0):
        lo_t = lax.broadcasted_iota(jnp.int32, (sub, LANES), 1) < HALF
        pos1 = lax.broadcasted_iota(jnp.int32, (sub, LANES), 0) + (t * tile + r0 + 1)
        pooled = []
        for sl in range(POOL_WIDTH // LANES):
            w_small, w_big = POOL_WINDOWS[2 * sl], POOL_WINDOWS[2 * sl + 1]
            cols = slice(sl * LANES, (sl + 1) * LANES)

            def shifted(d, cols=cols):
                return ubuf[POOL_TAIL + r0 - d:POOL_TAIL + r0 - d + sub, cols]

            u0 = shifted(0)
            acc = u0
            for d in range(1, w_small):
                acc = acc + shifted(d)
            small = acc
            for d in range(w_small, w_big):
                acc = acc + shifted(d)
            win = jnp.where(lo_t, small, acc)
            cnt = jnp.minimum(pos1, jnp.where(lo_t, w_small, w_big)).astype(f32)
            pooled.append((win / cnt - u0).astype(bf16))
        pooled = jnp.concatenate(pooled, axis=1)
        mixed = jnp.dot(pooled, wpool_ref[...], preferred_element_type=f32) * pscale_ref[...]
        y_ref[r0:r0 + sub, Y_P:Y_P + POOL_WIDTH] = mixed.astype(bf16)

    def output_tasks(r0):
        def make(c0):
            def task():
                o_ref[0, r0:r0 + sub, c0:c0 + OUT_CHUNK] = x_ref[0, r0:r0 + sub, c0:c0 + OUT_CHUNK] + jnp.dot(
                    y_ref[r0:r0 + sub, :], wout_ref[:, c0:c0 + OUT_CHUNK], preferred_element_type=f32)
            return task
        return [make(c0) for c0 in range(0, D_MODEL, OUT_CHUNK)]

    def mix_tasks(r0):
        tasks = []
        for b0 in range(r0, r0 + sub, BLOCK):
            tasks += attention_tasks(b0)
            tasks += [functools.partial(retention, b0, j) for j in range(RET_SLABS)]
        tasks.append(functools.partial(pool, r0))
        return tasks

    for p in range(npiece):
        r0 = p * sub
        if p + 1 < npiece:
            matmuls = project_tasks(lambda r=r0 + sub: x_ref[0, r:r + sub, :], z_ref, r0, r0 + sub)
        else:
            matmuls = project_tasks(lambda: xn_ref[0], z0_ref, 0, tile)
        if p >= 1:
            matmuls = _spread(matmuls, output_tasks(r0 - sub))
        for task in _spread(matmuls, mix_tasks(r0)):
            task()
    for task in output_tasks(tile - sub):
        task()

    kbuf[0:BLOCK + sub, :] = kbuf[tile:tile + BLOCK + sub, :]
    vbuf[0:BLOCK + sub, :] = vbuf[tile:tile + BLOCK + sub, :]
    ubuf[0:POOL_TAIL + sub, :] = ubuf[tile:tile + POOL_TAIL + sub, :]


_RESIDENT = pl.Buffered(1)


def _const_spec(shape):
    nd = len(shape)
    return pl.BlockSpec(shape, lambda b, t, _nd=nd: (0,) * _nd, pipeline_mode=_RESIDENT)


def _layer_spec(shape, layer):
    nd = len(shape)
    return pl.BlockSpec((None,) + tuple(shape[1:]), lambda b, t, _nd=nd: (layer,) + (0,) * (_nd - 1),
                        pipeline_mode=_RESIDENT)


def _mixer_call(layer, x, sinks, g, win, bias, cos, sin_a, sin_b, dm, zeta, xi, gtab, wpool, pscale, wout):
    batch, seq, _ = x.shape
    tile, sub = MIXER_TILE, MIXER_SUB
    grid = (batch, seq // tile)
    last_piece = seq // sub - 1
    f32, bf16 = jnp.float32, jnp.bfloat16
    rot_spec = pl.BlockSpec((tile, LANES), lambda b, t: (t, 0))
    in_specs = [
        pl.BlockSpec(memory_space=pltpu.SMEM),
        pl.BlockSpec((1, tile, D_MODEL), lambda b, t: (b, t, 0)),
        pl.BlockSpec((1, sub, D_MODEL), lambda b, t: (b, jnp.minimum((t + 1) * (tile // sub), last_piece), 0)),
        _layer_spec(g.shape, layer), _layer_spec(win.shape, layer), _const_spec(bias.shape),
        rot_spec, rot_spec, rot_spec,
        _const_spec(dm.shape), _const_spec(zeta.shape), _const_spec(xi.shape), _const_spec(gtab.shape),
        _layer_spec(wpool.shape, layer), _layer_spec(pscale.shape, layer), _layer_spec(wout.shape, layer),
    ]
    return pl.pallas_call(
        functools.partial(_mixer_kernel, tile=tile, sub=sub, layer=layer),
        grid=grid,
        in_specs=in_specs,
        out_specs=pl.BlockSpec((1, tile, D_MODEL), lambda b, t: (b, t, 0)),
        out_shape=jax.ShapeDtypeStruct(x.shape, x.dtype),
        scratch_shapes=[
            pltpu.VMEM((sub, Z_WIDTH), f32),
            pltpu.VMEM((tile - sub, Z_WIDTH), f32),
            pltpu.VMEM((tile, D_MODEL), bf16),
            pltpu.VMEM((BLOCK + tile + sub, ATT_KV_W), bf16),
            pltpu.VMEM((BLOCK + tile + sub, ATT_KV_W), bf16),
            pltpu.VMEM((POOL_TAIL + tile + sub, POOL_WIDTH), f32),
            pltpu.VMEM((RET_SLABS, LANES, LANES), f32),
        ],
        compiler_params=pltpu.CompilerParams(
            dimension_semantics=("arbitrary", "arbitrary"),
            vmem_limit_bytes=VMEM_LIMIT_BYTES),
        name="mixer",
    )(sinks, x, x, g, win, bias, cos, sin_a, sin_b, dm, zeta, xi, gtab, wpool, pscale, wout)


def _mlp_kernel(x_ref, g_ref, wup_ref, wdown_ref, gf_ref, o_ref, *, final_norm):
    f32, bf16 = jnp.float32, jnp.bfloat16
    x = x_ref[0]
    h = _rms_norm(x, g_ref[...]).astype(bf16)
    acc = x
    for c in range(D_FF // FF_CHUNK):
        a = jnp.dot(h, wup_ref[:, c * FF_CHUNK:(c + 1) * FF_CHUNK], preferred_element_type=f32)
        a = jnp.square(jnp.maximum(a, 0.0)).astype(bf16)
        acc = acc + jnp.dot(a, wdown_ref[c * FF_CHUNK:(c + 1) * FF_CHUNK, :], preferred_element_type=f32)
    if final_norm:
        acc = _rms_norm(acc, gf_ref[...])
    o_ref[0] = acc


def _mlp_call(layer, x, g, wup, wdown, gf, final_norm):
    batch, seq, _ = x.shape
    tile = MLP_TILE
    grid = (batch, seq // tile)
    return pl.pallas_call(
        functools.partial(_mlp_kernel, final_norm=final_norm),
        grid=grid,
        in_specs=[
            pl.BlockSpec((1, tile, D_MODEL), lambda b, t: (b, t, 0)),
            _layer_spec(g.shape, layer), _layer_spec(wup.shape, layer), _layer_spec(wdown.shape, layer),
            _const_spec(gf.shape),
        ],
        out_specs=pl.BlockSpec((1, tile, D_MODEL), lambda b, t: (b, t, 0)),
        out_shape=jax.ShapeDtypeStruct(x.shape, x.dtype),
        compiler_params=pltpu.CompilerParams(
            dimension_semantics=("arbitrary", "arbitrary"),
            vmem_limit_bytes=VMEM_LIMIT_BYTES),
        name="mlp",
    )(x, g, wup, wdown, gf)


def _prep_w_in(w_in):
    col_scale = np.ones((Z_WIDTH,), np.float32)
    col_scale[C_QA:C_QA + ATT_Q_W] = HEAD_DIM ** -0.5
    return (w_in * col_scale).astype(jnp.bfloat16)


def _attention_bias(rel_bias):
    i = np.arange(BLOCK)[:, None]
    j = np.arange(2 * BLOCK)[None, :]
    dist = BLOCK + i - j
    band = (dist >= 0) & (dist < BLOCK)
    onehot = (_t5_bucket(dist).reshape(1, -1) == np.arange(N_BUCKETS)[:, None]).astype(np.float32)
    bias = jnp.dot(rel_bias.astype(jnp.float32).T, onehot, precision=lax.Precision.HIGHEST)
    bias = bias.reshape(ATT_Q_HEADS, BLOCK, 2 * BLOCK)
    normal = jnp.where(jnp.asarray(band)[None], bias, NEG_INF)
    first = jnp.where(jnp.asarray(band & (j >= BLOCK))[None], bias, NEG_INF)
    return jnp.stack([normal, first]).reshape(2, ATT_Q_HEADS * BLOCK, 2 * BLOCK)


def _rotary_tables(seq):
    inv = 1.0 / (ROPE_BASE ** jnp.linspace(0.0, 1.0, HEAD_DIM // 2, dtype=jnp.float32))
    inv_lane = jnp.tile(jnp.repeat(inv, 2), LANES // HEAD_DIM)
    ang = jnp.arange(seq).astype(jnp.float32)[:, None] * inv_lane[None, :]
    cos, sin = jnp.cos(ang), jnp.sin(ang)
    even = (np.arange(LANES) % 2 == 0)[None, :]
    return cos, jnp.where(even, -sin, 0.0), jnp.where(even, 0.0, sin)


def _retention_tables():
    c = BLOCK
    f32 = np.float32
    scale = f32(HEAD_DIM ** -0.5)
    lg = np.log(f32(1.0) - f32(2.0) ** (f32(-5.0) - np.arange(RET_HEADS, dtype=f32))).astype(f32)
    idx = np.arange(c, dtype=f32)
    diff = idx[:, None] - idx[None, :]
    dmask = np.where(diff >= 0, np.exp(lg[:, None, None] * np.maximum(diff, f32(0.0))), f32(0.0)).astype(f32)
    dm = np.stack([np.concatenate([dmask[2 * j], dmask[2 * j + 1]], axis=1)
                   for j in range(RET_SLABS)]) * scale
    zeta = (np.exp(lg[:, None] * (f32(c) - f32(1.0) - idx)[None, :]) * scale).astype(f32)
    xi = np.exp(lg[None, :] * (idx[:, None] + f32(1.0))).astype(f32)
    g_chunk = np.exp(lg * f32(c)).astype(f32)
    head_of_lane = (np.arange(LANES) >= HALF).astype(np.int32)
    zeta_t = np.stack([zeta[2 * j + head_of_lane].T for j in range(RET_SLABS)])
    xi_t = np.stack([xi[:, 2 * j + head_of_lane] for j in range(RET_SLABS)])
    same = head_of_lane[:, None] == head_of_lane[None, :]
    gtab = np.stack([np.where(same, g_chunk[2 * j + head_of_lane][:, None], f32(0.0))
                     for j in range(RET_SLABS)]).astype(f32)
    return jnp.asarray(dm), jnp.asarray(zeta_t), jnp.asarray(xi_t), jnp.asarray(gtab)


def _pool_block_diag(pool_w):
    depth, groups = pool_w.shape[0], pool_w.shape[1]
    eye = jnp.eye(groups, dtype=jnp.bfloat16)[None, :, None, :, None]
    wide = pool_w.astype(jnp.bfloat16)[:, :, :, None, :] * eye
    return wide.reshape(depth, POOL_WIDTH, POOL_WIDTH)


def kernel(x, attn_norm_g, w_in, attn_sinks, rel_bias, pool_w, pool_scale, w_out, mlp_norm_g, w_up, w_down,
           final_norm_g):
    depth = w_in.shape[0]
    seq = x.shape[1]
    bf16 = jnp.bfloat16
    w_in_b = _prep_w_in(w_in)
    w_out_b = w_out.astype(bf16)
    w_pool = _pool_block_diag(pool_w)
    w_up_b = w_up.astype(bf16)
    w_down_b = w_down.astype(bf16)
    bias = _attention_bias(rel_bias)
    cos, sin_a, sin_b = _rotary_tables(seq)
    dm, zeta_t, xi_t, gtab = _retention_tables()
    g_attn = attn_norm_g.reshape(depth, 1, D_MODEL)
    g_mlp = mlp_norm_g.reshape(depth, 1, D_MODEL)
    p_scale = pool_scale.reshape(depth, 1, POOL_WIDTH)
    gf = final_norm_g.reshape(1, D_MODEL)
    for layer in range(depth):
        x = _mixer_call(layer, x, attn_sinks, g_attn, w_in_b, bias, cos, sin_a, sin_b, dm, zeta_t, xi_t, gtab,
                        w_pool, p_scale, w_out_b)
        x = _mlp_call(layer, x, g_mlp, w_up_b, w_down_b, gf, final_norm=(layer == depth - 1))
    return x
```

```python
import functools

import jax
import jax.numpy as jnp
import numpy as np
from jax import lax
from jax.experimental import pallas as pl
from jax.experimental.pallas import tpu as pltpu

D_MODEL = 1024
HEAD_DIM = 64
ATT_Q_HEADS = 6
ATT_KV_HEADS = 2
ATT_GROUP = ATT_Q_HEADS // ATT_KV_HEADS
RET_HEADS = 6
POOL_WINDOWS = (2, 4, 8, 16)
POOL_GROUP_WIDTH = 64
POOL_WIDTH = 256
BLOCK = 128
N_BUCKETS = 32
MAX_DISTANCE = 128
D_FF = 4 * D_MODEL
RMS_EPS = 1e-6
ROPE_BASE = 10000.0
NEG_INF = -1e30

LANES = 128
HALF = LANES // 2
ATT_Q_W = ATT_Q_HEADS * HEAD_DIM
ATT_KV_W = ATT_KV_HEADS * HEAD_DIM
RET_W = RET_HEADS * HEAD_DIM
RET_SLABS = RET_W // LANES
ATT_SLABS = ATT_Q_W // LANES
POOL_TAIL = 16

C_QA = 0
C_KA = C_QA + ATT_Q_W
C_VA = C_KA + ATT_KV_W
C_QR = C_VA + ATT_KV_W
C_KR = C_QR + RET_W
C_VR = C_KR + RET_W
C_GR = C_VR + RET_W
C_UP = C_GR + RET_W
Z_WIDTH = C_UP + POOL_WIDTH

Y_A = 0
Y_R = ATT_Q_W
Y_P = ATT_Q_W + RET_W

MIXER_TILE = 1024
MIXER_SUB = 256
PROJ_CHUNK = 256
OUT_CHUNK = 256
W_SCORES, W_SOFTMAX, W_VALUES, W_RETENTION = 0.5, 1.0, 1.0, 1.5
W_POOL_NARROW, W_POOL_WIDE, W_POOL_MAPS = 1.0, 3.0, 0.5
MLP_TILE = 1024
FF_CHUNK = 1024
VMEM_LIMIT_BYTES = 56 * 1024 * 1024


def _rms_norm(x, g):
    ms = jnp.mean(x * x, axis=-1, keepdims=True)
    return x * lax.rsqrt(ms + RMS_EPS) * g


def _t5_bucket(dist):
    max_exact = N_BUCKETS // 2
    n = np.maximum(dist, 0)
    large = max_exact + (np.log(np.maximum(n, 1) / max_exact)
                         / np.log(MAX_DISTANCE / max_exact)
                         * (N_BUCKETS - max_exact)).astype(np.int64)
    large = np.minimum(large, N_BUCKETS - 1)
    return np.where(n < max_exact, n, large).astype(np.int32)


def _spread(first, second):
    keyed = []
    for order, tasks in enumerate((first, second)):
        total = sum(weight for weight, _ in tasks)
        done = 0.0
        for weight, task in tasks:
            keyed.append(((done + 0.5 * weight) / total, order, (weight, task)))
            done += weight
    return [item for _, _, item in sorted(keyed, key=lambda entry: entry[:2])]


def _mixer_kernel(sinks_ref, x_ref, xn_ref, g_ref, win_ref, bias_ref, cos_ref, sina_ref, sinb_ref, dm_ref,
                  zeta_ref, xi_ref, gtab_ref, wpool_ref, pscale_ref, wout_ref, o_ref,
                  z0_ref, z_ref, y_ref, kbuf, vbuf, ubuf, r_ref, *, tile, sub, layer):
    t = pl.program_id(1)
    npiece = tile // sub
    assert npiece >= 2, "piece 0 of the next tile is projected while the last piece is mixed"
    f32, bf16 = jnp.float32, jnp.bfloat16

    lane = lax.broadcasted_iota(jnp.int32, (BLOCK, LANES), 1)
    lo = lane < HALF

    def zrows(r0):
        return (z0_ref, r0) if r0 < sub else (z_ref, r0 - sub)

    def project_tasks(read_x, zdst, zrow, brow):
        cache = []

        def normed():
            if not cache:
                cache.append(_rms_norm(read_x(), g_ref[...]).astype(bf16))
            return cache[0]

        side = ((C_KA, ATT_KV_W, kbuf, BLOCK), (C_VA, ATT_KV_W, vbuf, BLOCK), (C_UP, POOL_WIDTH, ubuf, POOL_TAIL))
        chunks = [(c0, min(c0 + PROJ_CHUNK, Z_WIDTH)) for c0 in range(0, Z_WIDTH, PROJ_CHUNK)]

        def make(c0, c1):
            def task():
                z = jnp.dot(normed(), win_ref[:, c0:c1], preferred_element_type=f32)
                zdst[zrow:zrow + sub, c0:c1] = z
                for cs, width, buf, head in side:
                    a, b = max(c0, cs), min(c1, cs + width)
                    if a < b:
                        buf[head + brow:head + brow + sub, a - cs:b - cs] = z[:, a - c0:b - c0].astype(buf.dtype)
            return task

        return [((c1 - c0) / PROJ_CHUNK, make(c0, c1)) for c0, c1 in chunks]

    @pl.when(t == 0)
    def _():
        kbuf[0:BLOCK, :] = jnp.zeros((BLOCK, ATT_KV_W), bf16)
        vbuf[0:BLOCK, :] = jnp.zeros((BLOCK, ATT_KV_W), bf16)
        ubuf[0:POOL_TAIL, :] = jnp.zeros((POOL_TAIL, POOL_WIDTH), f32)
        r_ref[...] = jnp.zeros(r_ref.shape, f32)
        for _, task in project_tasks(lambda: x_ref[0, 0:sub, :], z0_ref, 0, 0):
            task()

    def attention_tasks(r0):
        live = {"ps": [], "denoms": []}

        def swap_halves(v):
            return pltpu.roll(v, HALF, 1)

        def scores():
            zr, zo = zrows(r0)
            qs = [zr[zo:zo + BLOCK, C_QA + j * LANES:C_QA + (j + 1) * LANES] for j in range(ATT_SLABS)]
            swapped = {}
            parts = []
            for hd in range(ATT_Q_HEADS):
                slab, half, group = hd // 2, hd % 2, hd // ATT_GROUP
                q = qs[slab]
                if half != group:
                    q = swapped.setdefault(slab, swap_halves(q))
                parts.append(jnp.where(lo, q, 0.0) if group == 0 else jnp.where(lo, 0.0, q))
            qstack = jnp.concatenate(parts, axis=0).astype(bf16)
            kk = kbuf[r0:r0 + 2 * BLOCK, :]
            s = lax.dot_general(qstack, kk, (((1,), (1,)), ((), ())), preferred_element_type=f32)
            live["s"] = s + (bias_ref[jnp.where(t == 0, 1, 0)] if r0 == 0 else bias_ref[0])

        def softmax(heads):
            for hd in heads:
                sh = live["s"][hd * BLOCK:(hd + 1) * BLOCK]
                sink = sinks_ref[layer, hd]
                m = jnp.maximum(jnp.max(sh, axis=-1, keepdims=True), sink)
                p = jnp.exp(sh - m)
                live["denoms"].append(jnp.sum(p, axis=-1, keepdims=True) + jnp.exp(sink - m))
                live["ps"].append(p.astype(bf16))

        def weighted_values():
            denoms = live["denoms"]
            pstack = jnp.concatenate(live["ps"], axis=0)
            o = jnp.dot(pstack, vbuf[r0:r0 + 2 * BLOCK, :], preferred_element_type=f32)
            halves = []
            for hd in range(ATT_Q_HEADS):
                oh = o[hd * BLOCK:(hd + 1) * BLOCK] / denoms[hd]
                halves.append(oh if hd % 2 == hd // ATT_GROUP else swap_halves(oh))
            for j in range(ATT_SLABS):
                y_ref[r0:r0 + BLOCK, Y_A + j * LANES:Y_A + (j + 1) * LANES] = (
                    jnp.where(lo, halves[2 * j], halves[2 * j + 1]).astype(bf16))

        return ([(W_SCORES, scores)] + [(W_SOFTMAX, functools.partial(softmax, (hd,))) for hd in range(ATT_Q_HEADS)]
                + [(W_VALUES, weighted_values)])

    def rotary(v, cos, sin_a, sin_b):
        return v * cos + pltpu.roll(v, LANES - 1, 1) * sin_a + pltpu.roll(v, 1, 1) * sin_b

    def retention(r0, j):
        zr, zo = zrows(r0)
        cos = cos_ref[r0:r0 + BLOCK, :]
        sin_a = sina_ref[r0:r0 + BLOCK, :]
        sin_b = sinb_ref[r0:r0 + BLOCK, :]
        c0 = j * LANES
        q = rotary(zr[zo:zo + BLOCK, C_QR + c0:C_QR + c0 + LANES], cos, sin_a, sin_b)
        k = rotary(zr[zo:zo + BLOCK, C_KR + c0:C_KR + c0 + LANES], cos, sin_a, sin_b)
        v = zr[zo:zo + BLOCK, C_VR + c0:C_VR + c0 + LANES]
        gate = zr[zo:zo + BLOCK, C_GR + c0:C_GR + c0 + LANES]
        qb = q.astype(bf16)
        kb = k.astype(bf16)
        ksplit = jnp.concatenate([jnp.where(lo, k, 0.0), jnp.where(lo, 0.0, k)], axis=0).astype(bf16)
        sc = lax.dot_general(qb, ksplit, (((1,), (1,)), ((), ())), preferred_element_type=f32)
        acat = (sc * dm_ref[j]).astype(bf16)
        vsplit = jnp.concatenate([jnp.where(lo, v, 0.0), jnp.where(lo, 0.0, v)], axis=0).astype(bf16)
        inner = jnp.dot(acat, vsplit, preferred_element_type=f32)
        r_prev = r_ref[j]
        cross = jnp.dot(qb, r_prev.astype(bf16), preferred_element_type=f32) * xi_ref[j]
        o = inner + cross
        vz = (v * zeta_ref[j]).astype(bf16)
        u = lax.dot_general(kb, vz, (((0,), (0,)), ((), ())), preferred_element_type=f32)
        gt = gtab_ref[j]
        r_ref[j] = r_prev * gt + jnp.where(gt > 0.0, u, 0.0)
        o2 = o * o
        ms_lo = jnp.sum(jnp.where(lo, o2, 0.0), axis=-1, keepdims=True)
        ms_hi = jnp.sum(jnp.where(lo, 0.0, o2), axis=-1, keepdims=True)
        ms = jnp.where(lo, ms_lo, ms_hi) * (1.0 / HEAD_DIM)
        on = o * lax.rsqrt(ms + RMS_EPS)
        y_ref[r0:r0 + BLOCK, Y_R + c0:Y_R + c0 + LANES] = (jax.nn.silu(gate) * on).astype(bf16)

    def pool_tasks(r0):
        pooled = {}

        def window(sl):
            lo_t = lax.broadcasted_iota(jnp.int32, (sub, LANES), 1) < HALF
            pos1 = lax.broadcasted_iota(jnp.int32, (sub, LANES), 0) + (t * tile + r0 + 1)
            w_small, w_big = POOL_WINDOWS[2 * sl], POOL_WINDOWS[2 * sl + 1]

            def shifted(d):
                return ubuf[POOL_TAIL + r0 - d:POOL_TAIL + r0 - d + sub, sl * LANES:(sl + 1) * LANES]

            u0 = shifted(0)
            acc = u0
            for d in range(1, w_small):
                acc = acc + shifted(d)
            small = acc
            for d in range(w_small, w_big):
                acc = acc + shifted(d)
            win = jnp.where(lo_t, small, acc)
            cnt = jnp.minimum(pos1, jnp.where(lo_t, w_small, w_big)).astype(f32)
            pooled[sl] = (win / cnt - u0).astype(bf16)

        def group_maps():
            both = jnp.concatenate([pooled[0], pooled[1]], axis=1)
            mixed = jnp.dot(both, wpool_ref[...], preferred_element_type=f32) * pscale_ref[...]
            y_ref[r0:r0 + sub, Y_P:Y_P + POOL_WIDTH] = mixed.astype(bf16)

        return [(W_POOL_NARROW, functools.partial(window, 0)), (W_POOL_WIDE, functools.partial(window, 1)),
                (W_POOL_MAPS, group_maps)]

    def output_tasks(r0):
        def make(c0):
            def task():
                o_ref[0, r0:r0 + sub, c0:c0 + OUT_CHUNK] = x_ref[0, r0:r0 + sub, c0:c0 + OUT_CHUNK] + jnp.dot(
                    y_ref[r0:r0 + sub, :], wout_ref[:, c0:c0 + OUT_CHUNK], preferred_element_type=f32)
            return task
        return [(1.0, make(c0)) for c0 in range(0, D_MODEL, OUT_CHUNK)]

    def mix_tasks(r0):
        tasks = []
        for b0 in range(r0, r0 + sub, BLOCK):
            tasks += attention_tasks(b0)
            tasks += [(W_RETENTION, functools.partial(retention, b0, j)) for j in range(RET_SLABS)]
        return tasks + pool_tasks(r0)

    for p in range(npiece):
        r0 = p * sub
        if p + 1 < npiece:
            matmuls = project_tasks(lambda r=r0 + sub: x_ref[0, r:r + sub, :], z_ref, r0, r0 + sub)
        else:
            matmuls = project_tasks(lambda: xn_ref[0], z0_ref, 0, tile)
        if p >= 1:
            matmuls = _spread(matmuls, output_tasks(r0 - sub))
        for _, task in _spread(matmuls, mix_tasks(r0)):
            task()
    for _, task in output_tasks(tile - sub):
        task()

    kbuf[0:BLOCK + sub, :] = kbuf[tile:tile + BLOCK + sub, :]
    vbuf[0:BLOCK + sub, :] = vbuf[tile:tile + BLOCK + sub, :]
    ubuf[0:POOL_TAIL + sub, :] = ubuf[tile:tile + POOL_TAIL + sub, :]


_RESIDENT = pl.Buffered(1)


def _const_spec(shape):
    nd = len(shape)
    return pl.BlockSpec(shape, lambda b, t, _nd=nd: (0,) * _nd, pipeline_mode=_RESIDENT)


def _layer_spec(shape, layer):
    nd = len(shape)
    return pl.BlockSpec((None,) + tuple(shape[1:]), lambda b, t, _nd=nd: (layer,) + (0,) * (_nd - 1),
                        pipeline_mode=_RESIDENT)


def _mixer_call(layer, x, sinks, g, win, bias, cos, sin_a, sin_b, dm, zeta, xi, gtab, wpool, pscale, wout):
    batch, seq, _ = x.shape
    tile, sub = MIXER_TILE, MIXER_SUB
    grid = (batch, seq // tile)
    last_piece = seq // sub - 1
    f32, bf16 = jnp.float32, jnp.bfloat16
    rot_spec = pl.BlockSpec((tile, LANES), lambda b, t: (t, 0))
    in_specs = [
        pl.BlockSpec(memory_space=pltpu.SMEM),
        pl.BlockSpec((1, tile, D_MODEL), lambda b, t: (b, t, 0)),
        pl.BlockSpec((1, sub, D_MODEL), lambda b, t: (b, jnp.minimum((t + 1) * (tile // sub), last_piece), 0)),
        _layer_spec(g.shape, layer), _layer_spec(win.shape, layer), _const_spec(bias.shape),
        rot_spec, rot_spec, rot_spec,
        _const_spec(dm.shape), _const_spec(zeta.shape), _const_spec(xi.shape), _const_spec(gtab.shape),
        _layer_spec(wpool.shape, layer), _layer_spec(pscale.shape, layer), _layer_spec(wout.shape, layer),
    ]
    return pl.pallas_call(
        functools.partial(_mixer_kernel, tile=tile, sub=sub, layer=layer),
        grid=grid,
        in_specs=in_specs,
        out_specs=pl.BlockSpec((1, tile, D_MODEL), lambda b, t: (b, t, 0)),
        out_shape=jax.ShapeDtypeStruct(x.shape, x.dtype),
        scratch_shapes=[
            pltpu.VMEM((sub, Z_WIDTH), f32),
            pltpu.VMEM((tile - sub, Z_WIDTH), f32),
            pltpu.VMEM((tile, D_MODEL), bf16),
            pltpu.VMEM((BLOCK + tile + sub, ATT_KV_W), bf16),
            pltpu.VMEM((BLOCK + tile + sub, ATT_KV_W), bf16),
            pltpu.VMEM((POOL_TAIL + tile + sub, POOL_WIDTH), f32),
            pltpu.VMEM((RET_SLABS, LANES, LANES), f32),
        ],
        compiler_params=pltpu.CompilerParams(
            dimension_semantics=("arbitrary", "arbitrary"),
            vmem_limit_bytes=VMEM_LIMIT_BYTES),
        name="mixer",
    )(sinks, x, x, g, win, bias, cos, sin_a, sin_b, dm, zeta, xi, gtab, wpool, pscale, wout)


def _mlp_kernel(x_ref, g_ref, wup_ref, wdown_ref, gf_ref, o_ref, *, final_norm):
    f32, bf16 = jnp.float32, jnp.bfloat16
    x = x_ref[0]
    h = _rms_norm(x, g_ref[...]).astype(bf16)
    acc = x
    for c in range(D_FF // FF_CHUNK):
        a = jnp.dot(h, wup_ref[:, c * FF_CHUNK:(c + 1) * FF_CHUNK], preferred_element_type=f32)
        a = jnp.square(jnp.maximum(a, 0.0)).astype(bf16)
        acc = acc + jnp.dot(a, wdown_ref[c * FF_CHUNK:(c + 1) * FF_CHUNK, :], preferred_element_type=f32)
    if final_norm:
        acc = _rms_norm(acc, gf_ref[...])
    o_ref[0] = acc


def _mlp_call(layer, x, g, wup, wdown, gf, final_norm):
    batch, seq, _ = x.shape
    tile = MLP_TILE
    grid = (batch, seq // tile)
    return pl.pallas_call(
        functools.partial(_mlp_kernel, final_norm=final_norm),
        grid=grid,
        in_specs=[
            pl.BlockSpec((1, tile, D_MODEL), lambda b, t: (b, t, 0)),
            _layer_spec(g.shape, layer), _layer_spec(wup.shape, layer), _layer_spec(wdown.shape, layer),
            _const_spec(gf.shape),
        ],
        out_specs=pl.BlockSpec((1, tile, D_MODEL), lambda b, t: (b, t, 0)),
        out_shape=jax.ShapeDtypeStruct(x.shape, x.dtype),
        compiler_params=pltpu.CompilerParams(
            dimension_semantics=("arbitrary", "arbitrary"),
            vmem_limit_bytes=VMEM_LIMIT_BYTES),
        name="mlp",
    )(x, g, wup, wdown, gf)


def _prep_w_in(w_in):
    col_scale = np.ones((Z_WIDTH,), np.float32)
    col_scale[C_QA:C_QA + ATT_Q_W] = HEAD_DIM ** -0.5
    return (w_in * col_scale).astype(jnp.bfloat16)


def _attention_bias(rel_bias):
    i = np.arange(BLOCK)[:, None]
    j = np.arange(2 * BLOCK)[None, :]
    dist = BLOCK + i - j
    band = (dist >= 0) & (dist < BLOCK)
    onehot = (_t5_bucket(dist).reshape(1, -1) == np.arange(N_BUCKETS)[:, None]).astype(np.float32)
    bias = jnp.dot(rel_bias.astype(jnp.float32).T, onehot, precision=lax.Precision.HIGHEST)
    bias = bias.reshape(ATT_Q_HEADS, BLOCK, 2 * BLOCK)
    normal = jnp.where(jnp.asarray(band)[None], bias, NEG_INF)
    first = jnp.where(jnp.asarray(band & (j >= BLOCK))[None], bias, NEG_INF)
    return jnp.stack([normal, first]).reshape(2, ATT_Q_HEADS * BLOCK, 2 * BLOCK)


def _rotary_tables(seq):
    inv = 1.0 / (ROPE_BASE ** jnp.linspace(0.0, 1.0, HEAD_DIM // 2, dtype=jnp.float32))
    inv_lane = jnp.tile(jnp.repeat(inv, 2), LANES // HEAD_DIM)
    ang = jnp.arange(seq).astype(jnp.float32)[:, None] * inv_lane[None, :]
    cos, sin = jnp.cos(ang), jnp.sin(ang)
    even = (np.arange(LANES) % 2 == 0)[None, :]
    return cos, jnp.where(even, -sin, 0.0), jnp.where(even, 0.0, sin)


def _retention_tables():
    c = BLOCK
    f32 = np.float32
    scale = f32(HEAD_DIM ** -0.5)
    lg = np.log(f32(1.0) - f32(2.0) ** (f32(-5.0) - np.arange(RET_HEADS, dtype=f32))).astype(f32)
    idx = np.arange(c, dtype=f32)
    diff = idx[:, None] - idx[None, :]
    dmask = np.where(diff >= 0, np.exp(lg[:, None, None] * np.maximum(diff, f32(0.0))), f32(0.0)).astype(f32)
    dm = np.stack([np.concatenate([dmask[2 * j], dmask[2 * j + 1]], axis=1)
                   for j in range(RET_SLABS)]) * scale
    zeta = (np.exp(lg[:, None] * (f32(c) - f32(1.0) - idx)[None, :]) * scale).astype(f32)
    xi = np.exp(lg[None, :] * (idx[:, None] + f32(1.0))).astype(f32)
    g_chunk = np.exp(lg * f32(c)).astype(f32)
    head_of_lane = (np.arange(LANES) >= HALF).astype(np.int32)
    zeta_t = np.stack([zeta[2 * j + head_of_lane].T for j in range(RET_SLABS)])
    xi_t = np.stack([xi[:, 2 * j + head_of_lane] for j in range(RET_SLABS)])
    same = head_of_lane[:, None] == head_of_lane[None, :]
    gtab = np.stack([np.where(same, g_chunk[2 * j + head_of_lane][:, None], f32(0.0))
                     for j in range(RET_SLABS)]).astype(f32)
    return jnp.asarray(dm), jnp.asarray(zeta_t), jnp.asarray(xi_t), jnp.asarray(gtab)


def _pool_block_diag(pool_w):
    depth, groups = pool_w.shape[0], pool_w.shape[1]
    eye = jnp.eye(groups, dtype=jnp.bfloat16)[None, :, None, :, None]
    wide = pool_w.astype(jnp.bfloat16)[:, :, :, None, :] * eye
    return wide.reshape(depth, POOL_WIDTH, POOL_WIDTH)


def kernel(x, attn_norm_g, w_in, attn_sinks, rel_bias, pool_w, pool_scale, w_out, mlp_norm_g, w_up, w_down,
           final_norm_g):
    depth = w_in.shape[0]
    seq = x.shape[1]
    bf16 = jnp.bfloat16
    w_in_b = _prep_w_in(w_in)
    w_out_b = w_out.astype(bf16)
    w_pool = _pool_block_diag(pool_w)
    w_up_b = w_up.astype(bf16)
    w_down_b = w_down.astype(bf16)
    bias = _attention_bias(rel_bias)
    cos, sin_a, sin_b = _rotary_tables(seq)
    dm, zeta_t, xi_t, gtab = _retention_tables()
    g_attn = attn_norm_g.reshape(depth, 1, D_MODEL)
    g_mlp = mlp_norm_g.reshape(depth, 1, D_MODEL)
    p_scale = pool_scale.reshape(depth, 1, POOL_WIDTH)
    gf = final_norm_g.reshape(1, D_MODEL)
    for layer in range(depth):
        x = _mixer_call(layer, x, attn_sinks, g_attn, w_in_b, bias, cos, sin_a, sin_b, dm, zeta_t, xi_t, gtab,
                        w_pool, p_scale, w_out_b)
        x = _mlp_call(layer, x, g_mlp, w_up_b, w_down_b, gf, final_norm=(layer == depth - 1))
    return x
```

```python
import functools

import jax
import jax.numpy as jnp
import numpy as np
from jax import lax
from jax.experimental import pallas as pl
from jax.experimental.pallas import tpu as pltpu

D_MODEL = 1024
HEAD_DIM = 64
ATT_Q_HEADS = 6
ATT_KV_HEADS = 2
ATT_GROUP = ATT_Q_HEADS // ATT_KV_HEADS
RET_HEADS = 6
POOL_WINDOWS = (2, 4, 8, 16)
POOL_GROUP_WIDTH = 64
POOL_WIDTH = 256
BLOCK = 128
N_BUCKETS = 32
MAX_DISTANCE = 128
D_FF = 4 * D_MODEL
RMS_EPS = 1e-6
ROPE_BASE = 10000.0
NEG_INF = -1e30

LANES = 128
HALF = LANES // 2
ATT_Q_W = ATT_Q_HEADS * HEAD_DIM
ATT_KV_W = ATT_KV_HEADS * HEAD_DIM
RET_W = RET_HEADS * HEAD_DIM
RET_SLABS = RET_W // LANES
ATT_SLABS = ATT_Q_W // LANES
POOL_TAIL = 16

C_QA = 0
C_KA = C_QA + ATT_Q_W
C_VA = C_KA + ATT_KV_W
C_QR = C_VA + ATT_KV_W
C_KR = C_QR + RET_W
C_VR = C_KR + RET_W
C_GR = C_VR + RET_W
C_UP = C_GR + RET_W
Z_WIDTH = C_UP + POOL_WIDTH

Y_A = 0
Y_R = ATT_Q_W
Y_P = ATT_Q_W + RET_W

LAYER_TILE = 512
LAYER_SUB = 256
PROJ_CHUNK = 256
OUT_CHUNK = 256
FF_TASK = 512
W_PROJ, W_OUT, W_FF = 1.0, 1.0, 4.0
W_SCORES, W_SOFTMAX, W_VALUES, W_RETENTION = 0.5, 1.0, 1.0, 1.5
W_POOL_NARROW, W_POOL_WIDE, W_POOL_MAPS = 1.0, 3.0, 0.5
VMEM_LIMIT_BYTES = 56 * 1024 * 1024


def _rms_norm(x, g):
    ms = jnp.mean(x * x, axis=-1, keepdims=True)
    return x * lax.rsqrt(ms + RMS_EPS) * g


def _t5_bucket(dist):
    max_exact = N_BUCKETS // 2
    n = np.maximum(dist, 0)
    large = max_exact + (np.log(np.maximum(n, 1) / max_exact)
                         / np.log(MAX_DISTANCE / max_exact)
                         * (N_BUCKETS - max_exact)).astype(np.int64)
    large = np.minimum(large, N_BUCKETS - 1)
    return np.where(n < max_exact, n, large).astype(np.int32)


def _spread(first, second):
    keyed = []
    for order, tasks in enumerate((first, second)):
        total = sum(weight for weight, _ in tasks)
        done = 0.0
        for weight, task in tasks:
            keyed.append(((done + 0.5 * weight) / total, order, (weight, task)))
            done += weight
    return [item for _, _, item in sorted(keyed, key=lambda entry: entry[:2])]


def _layer_kernel(sinks_ref, x_ref, xn_ref, ga_ref, win_ref, bias_ref, cos_ref, sina_ref, sinb_ref, dm_ref,
                  zeta_ref, xi_ref, gtab_ref, wpool_ref, pscale_ref, wout_ref, gm_ref, wup_ref, wdown_ref,
                  gf_ref, o_ref,
                  z0_ref, z_ref, y_ref, x1_ref, kbuf, vbuf, ubuf, r_ref,
                  *, tile, sub, ntiles, layer, final_norm):
    t = pl.program_id(1)
    npiece = tile // sub
    assert npiece >= 2, "piece 0 of the next tile is projected while the last piece is mixed"
    f32, bf16 = jnp.float32, jnp.bfloat16

    lane = lax.broadcasted_iota(jnp.int32, (BLOCK, LANES), 1)
    lo = lane < HALF

    def zrows(r0):
        return (z0_ref, r0) if r0 < sub else (z_ref, r0 - sub)

    def project_tasks(read_x, zdst, zrow, brow):
        cache = []

        def normed():
            if not cache:
                cache.append(_rms_norm(read_x(), ga_ref[...]).astype(bf16))
            return cache[0]

        side = ((C_KA, ATT_KV_W, kbuf, BLOCK), (C_VA, ATT_KV_W, vbuf, BLOCK), (C_UP, POOL_WIDTH, ubuf, POOL_TAIL))
        chunks = [(c0, min(c0 + PROJ_CHUNK, Z_WIDTH)) for c0 in range(0, Z_WIDTH, PROJ_CHUNK)]

        def make(c0, c1):
            def task():
                z = jnp.dot(normed(), win_ref[:, c0:c1], preferred_element_type=f32)
                zdst[zrow:zrow + sub, c0:c1] = z
                for cs, width, buf, head in side:
                    a, b = max(c0, cs), min(c1, cs + width)
                    if a < b:
                        buf[head + brow:head + brow + sub, a - cs:b - cs] = z[:, a - c0:b - c0].astype(buf.dtype)
            return task

        return [(W_PROJ * (c1 - c0) / PROJ_CHUNK, make(c0, c1)) for c0, c1 in chunks]

    def attention_tasks(r0, first_block):
        live = {"ps": [], "denoms": []}

        def swap_halves(v):
            return pltpu.roll(v, HALF, 1)

        def scores():
            zr, zo = zrows(r0)
            qs = [zr[zo:zo + BLOCK, C_QA + j * LANES:C_QA + (j + 1) * LANES] for j in range(ATT_SLABS)]
            swapped = {}
            parts = []
            for hd in range(ATT_Q_HEADS):
                slab, half, group = hd // 2, hd % 2, hd // ATT_GROUP
                q = qs[slab]
                if half != group:
                    q = swapped.setdefault(slab, swap_halves(q))
                parts.append(jnp.where(lo, q, 0.0) if group == 0 else jnp.where(lo, 0.0, q))
            qstack = jnp.concatenate(parts, axis=0).astype(bf16)
            kk = kbuf[r0:r0 + 2 * BLOCK, :]
            s = lax.dot_general(qstack, kk, (((1,), (1,)), ((), ())), preferred_element_type=f32)
            live["s"] = s + bias_ref[1 if first_block else 0]

        def softmax(hd):
            sh = live["s"][hd * BLOCK:(hd + 1) * BLOCK]
            sink = sinks_ref[layer, hd]
            m = jnp.maximum(jnp.max(sh, axis=-1, keepdims=True), sink)
            p = jnp.exp(sh - m)
            live["denoms"].append(jnp.sum(p, axis=-1, keepdims=True) + jnp.exp(sink - m))
            live["ps"].append(p.astype(bf16))

        def weighted_values():
            denoms = live["denoms"]
            pstack = jnp.concatenate(live["ps"], axis=0)
            o = jnp.dot(pstack, vbuf[r0:r0 + 2 * BLOCK, :], preferred_element_type=f32)
            halves = []
            for hd in range(ATT_Q_HEADS):
                oh = o[hd * BLOCK:(hd + 1) * BLOCK] / denoms[hd]
                halves.append(oh if hd % 2 == hd // ATT_GROUP else swap_halves(oh))
            for j in range(ATT_SLABS):
                y_ref[r0:r0 + BLOCK, Y_A + j * LANES:Y_A + (j + 1) * LANES] = (
                    jnp.where(lo, halves[2 * j], halves[2 * j + 1]).astype(bf16))

        return ([(W_SCORES, scores)] + [(W_SOFTMAX, functools.partial(softmax, hd)) for hd in range(ATT_Q_HEADS)]
                + [(W_VALUES, weighted_values)])

    def rotary(v, cos, sin_a, sin_b):
        return v * cos + pltpu.roll(v, LANES - 1, 1) * sin_a + pltpu.roll(v, 1, 1) * sin_b

    def retention(r0, j):
        zr, zo = zrows(r0)
        cos = cos_ref[r0:r0 + BLOCK, :]
        sin_a = sina_ref[r0:r0 + BLOCK, :]
        sin_b = sinb_ref[r0:r0 + BLOCK, :]
        c0 = j * LANES
        q = rotary(zr[zo:zo + BLOCK, C_QR + c0:C_QR + c0 + LANES], cos, sin_a, sin_b)
        k = rotary(zr[zo:zo + BLOCK, C_KR + c0:C_KR + c0 + LANES], cos, sin_a, sin_b)
        v = zr[zo:zo + BLOCK, C_VR + c0:C_VR + c0 + LANES]
        gate = zr[zo:zo + BLOCK, C_GR + c0:C_GR + c0 + LANES]
        qb = q.astype(bf16)
        kb = k.astype(bf16)
        ksplit = jnp.concatenate([jnp.where(lo, k, 0.0), jnp.where(lo, 0.0, k)], axis=0).astype(bf16)
        sc = lax.dot_general(qb, ksplit, (((1,), (1,)), ((), ())), preferred_element_type=f32)
        acat = (sc * dm_ref[j]).astype(bf16)
        vsplit = jnp.concatenate([jnp.where(lo, v, 0.0), jnp.where(lo, 0.0, v)], axis=0).astype(bf16)
        inner = jnp.dot(acat, vsplit, preferred_element_type=f32)
        r_prev = r_ref[j]
        cross = jnp.dot(qb, r_prev.astype(bf16), preferred_element_type=f32) * xi_ref[j]
        o = inner + cross
        vz = (v * zeta_ref[j]).astype(bf16)
        u = lax.dot_general(kb, vz, (((0,), (0,)), ((), ())), preferred_element_type=f32)
        gt = gtab_ref[j]
        r_ref[j] = r_prev * gt + jnp.where(gt > 0.0, u, 0.0)
        o2 = o * o
        ms_lo = jnp.sum(jnp.where(lo, o2, 0.0), axis=-1, keepdims=True)
        ms_hi = jnp.sum(jnp.where(lo, 0.0, o2), axis=-1, keepdims=True)
        ms = jnp.where(lo, ms_lo, ms_hi) * (1.0 / HEAD_DIM)
        on = o * lax.rsqrt(ms + RMS_EPS)
        y_ref[r0:r0 + BLOCK, Y_R + c0:Y_R + c0 + LANES] = (jax.nn.silu(gate) * on).astype(bf16)

    def pool_tasks(r0):
        pooled = {}

        def window(sl):
            lo_t = lax.broadcasted_iota(jnp.int32, (sub, LANES), 1) < HALF
            pos1 = lax.broadcasted_iota(jnp.int32, (sub, LANES), 0) + (t * tile + r0 + 1)
            w_small, w_big = POOL_WINDOWS[2 * sl], POOL_WINDOWS[2 * sl + 1]

            def shifted(d):
                return ubuf[POOL_TAIL + r0 - d:POOL_TAIL + r0 - d + sub, sl * LANES:(sl + 1) * LANES]

            u0 = shifted(0)
            acc = u0
            for d in range(1, w_small):
                acc = acc + shifted(d)
            small = acc
            for d in range(w_small, w_big):
                acc = acc + shifted(d)
            win = jnp.where(lo_t, small, acc)
            cnt = jnp.minimum(pos1, jnp.where(lo_t, w_small, w_big)).astype(f32)
            pooled[sl] = (win / cnt - u0).astype(bf16)

        def group_maps():
            both = jnp.concatenate([pooled[0], pooled[1]], axis=1)
            mixed = jnp.dot(both, wpool_ref[...], preferred_element_type=f32) * pscale_ref[...]
            y_ref[r0:r0 + sub, Y_P:Y_P + POOL_WIDTH] = mixed.astype(bf16)

        return [(W_POOL_NARROW, functools.partial(window, 0)), (W_POOL_WIDE, functools.partial(window, 1)),
                (W_POOL_MAPS, group_maps)]

    def output_tasks(r0):
        def make(c0):
            def task():
                x1_ref[r0:r0 + sub, c0:c0 + OUT_CHUNK] = x_ref[0, r0:r0 + sub, c0:c0 + OUT_CHUNK] + jnp.dot(
                    y_ref[r0:r0 + sub, :], wout_ref[:, c0:c0 + OUT_CHUNK], preferred_element_type=f32)
            return task
        return [(W_OUT, make(c0)) for c0 in range(0, D_MODEL, OUT_CHUNK)]

    def mlp_tasks(r0):
        cache = []
        hidden = {}

        def normed():
            if not cache:
                cache.append(_rms_norm(x1_ref[r0:r0 + sub, :], gm_ref[...]).astype(bf16))
            return cache[0]

        def up(c0):
            a = jnp.dot(normed(), wup_ref[:, c0:c0 + FF_TASK], preferred_element_type=f32)
            hidden[c0] = jnp.square(jnp.maximum(a, 0.0)).astype(bf16)

        def down(c0):
            part = jnp.dot(hidden.pop(c0), wdown_ref[c0:c0 + FF_TASK, :], preferred_element_type=f32)
            base = x1_ref[r0:r0 + sub, :] if c0 == 0 else o_ref[0, r0:r0 + sub, :]
            acc = base + part
            if final_norm and c0 + FF_TASK == D_FF:
                acc = _rms_norm(acc, gf_ref[...])
            o_ref[0, r0:r0 + sub, :] = acc

        tasks = []
        for c0 in range(0, D_FF, FF_TASK):
            tasks += [(W_FF, functools.partial(up, c0)), (W_FF, functools.partial(down, c0))]
        return tasks

    def mix_tasks(r0, first_tile):
        tasks = []
        for b0 in range(r0, r0 + sub, BLOCK):
            tasks += attention_tasks(b0, first_tile and b0 == 0)
            tasks += [(W_RETENTION, functools.partial(retention, b0, j)) for j in range(RET_SLABS)]
        return tasks + pool_tasks(r0)

    def emit_mixer(first_tile, with_mlp):
        for p in range(npiece):
            r0 = p * sub
            if p + 1 < npiece:
                matmuls = project_tasks(lambda r=r0 + sub: x_ref[0, r:r + sub, :], z_ref, r0, r0 + sub)
            else:
                matmuls = project_tasks(lambda: xn_ref[0], z0_ref, 0, tile)
            if p >= 1:
                matmuls = _spread(matmuls, output_tasks(r0 - sub))
            if with_mlp:
                matmuls = _spread(matmuls, mlp_tasks(r0))
            for _, task in _spread(matmuls, mix_tasks(r0, first_tile)):
                task()
        for _, task in output_tasks(tile - sub):
            task()
        kbuf[0:BLOCK + sub, :] = kbuf[tile:tile + BLOCK + sub, :]
        vbuf[0:BLOCK + sub, :] = vbuf[tile:tile + BLOCK + sub, :]
        ubuf[0:POOL_TAIL + sub, :] = ubuf[tile:tile + POOL_TAIL + sub, :]

    @pl.when(t == 0)
    def _():
        kbuf[0:BLOCK, :] = jnp.zeros((BLOCK, ATT_KV_W), bf16)
        vbuf[0:BLOCK, :] = jnp.zeros((BLOCK, ATT_KV_W), bf16)
        ubuf[0:POOL_TAIL, :] = jnp.zeros((POOL_TAIL, POOL_WIDTH), f32)
        r_ref[...] = jnp.zeros(r_ref.shape, f32)
        for _, task in project_tasks(lambda: x_ref[0, 0:sub, :], z0_ref, 0, 0):
            task()
        emit_mixer(True, False)

    @pl.when(jnp.logical_and(t > 0, t < ntiles))
    def _():
        emit_mixer(False, True)

    @pl.when(t == ntiles)
    def _():
        for p in range(npiece):
            for _, task in mlp_tasks(p * sub):
                task()


_RESIDENT = pl.Buffered(1)


def _const_spec(shape):
    nd = len(shape)
    return pl.BlockSpec(shape, lambda b, t, _nd=nd: (0,) * _nd, pipeline_mode=_RESIDENT)


def _layer_spec(shape, layer):
    nd = len(shape)
    return pl.BlockSpec((None,) + tuple(shape[1:]), lambda b, t, _nd=nd: (layer,) + (0,) * (_nd - 1),
                        pipeline_mode=_RESIDENT)


def _layer_call(layer, final_norm, x, sinks, ga, win, bias, cos, sin_a, sin_b, dm, zeta, xi, gtab, wpool, pscale,
                wout, gm, wup, wdown, gf):
    batch, seq, _ = x.shape
    tile, sub = LAYER_TILE, LAYER_SUB
    ntiles = seq // tile
    last_tile = ntiles - 1
    last_piece = seq // sub - 1
    f32, bf16 = jnp.float32, jnp.bfloat16
    rot_spec = pl.BlockSpec((tile, LANES), lambda b, t: (jnp.minimum(t, last_tile), 0))
    in_specs = [
        pl.BlockSpec(memory_space=pltpu.SMEM),
        pl.BlockSpec((1, tile, D_MODEL), lambda b, t: (b, jnp.minimum(t, last_tile), 0)),
        pl.BlockSpec((1, sub, D_MODEL), lambda b, t: (b, jnp.minimum((t + 1) * (tile // sub), last_piece), 0)),
        _layer_spec(ga.shape, layer), _layer_spec(win.shape, layer), _const_spec(bias.shape),
        rot_spec, rot_spec, rot_spec,
        _const_spec(dm.shape), _const_spec(zeta.shape), _const_spec(xi.shape), _const_spec(gtab.shape),
        _layer_spec(wpool.shape, layer), _layer_spec(pscale.shape, layer), _layer_spec(wout.shape, layer),
        _layer_spec(gm.shape, layer), _layer_spec(wup.shape, layer), _layer_spec(wdown.shape, layer),
        _const_spec(gf.shape),
    ]
    return pl.pallas_call(
        functools.partial(_layer_kernel, tile=tile, sub=sub, ntiles=ntiles, layer=layer, final_norm=final_norm),
        grid=(batch, ntiles + 1),
        in_specs=in_specs,
        out_specs=pl.BlockSpec((1, tile, D_MODEL), lambda b, t: (b, jnp.maximum(t - 1, 0), 0)),
        out_shape=jax.ShapeDtypeStruct(x.shape, x.dtype),
        scratch_shapes=[
            pltpu.VMEM((sub, Z_WIDTH), f32),
            pltpu.VMEM((tile - sub, Z_WIDTH), f32),
            pltpu.VMEM((tile, D_MODEL), bf16),
            pltpu.VMEM((tile, D_MODEL), f32),
            pltpu.VMEM((BLOCK + tile + sub, ATT_KV_W), bf16),
            pltpu.VMEM((BLOCK + tile + sub, ATT_KV_W), bf16),
            pltpu.VMEM((POOL_TAIL + tile + sub, POOL_WIDTH), f32),
            pltpu.VMEM((RET_SLABS, LANES, LANES), f32),
        ],
        compiler_params=pltpu.CompilerParams(
            dimension_semantics=("arbitrary", "arbitrary"),
            vmem_limit_bytes=VMEM_LIMIT_BYTES),
        name="layer",
    )(sinks, x, x, ga, win, bias, cos, sin_a, sin_b, dm, zeta, xi, gtab, wpool, pscale, wout, gm, wup, wdown, gf)


def _prep_w_in(w_in):
    col_scale = np.ones((Z_WIDTH,), np.float32)
    col_scale[C_QA:C_QA + ATT_Q_W] = HEAD_DIM ** -0.5
    return (w_in * col_scale).astype(jnp.bfloat16)


def _attention_bias(rel_bias):
    i = np.arange(BLOCK)[:, None]
    j = np.arange(2 * BLOCK)[None, :]
    dist = BLOCK + i - j
    band = (dist >= 0) & (dist < BLOCK)
    onehot = (_t5_bucket(dist).reshape(1, -1) == np.arange(N_BUCKETS)[:, None]).astype(np.float32)
    bias = jnp.dot(rel_bias.astype(jnp.float32).T, onehot, precision=lax.Precision.HIGHEST)
    bias = bias.reshape(ATT_Q_HEADS, BLOCK, 2 * BLOCK)
    normal = jnp.where(jnp.asarray(band)[None], bias, NEG_INF)
    first = jnp.where(jnp.asarray(band & (j >= BLOCK))[None], bias, NEG_INF)
    return jnp.stack([normal, first]).reshape(2, ATT_Q_HEADS * BLOCK, 2 * BLOCK)


def _rotary_tables(seq):
    inv = 1.0 / (ROPE_BASE ** jnp.linspace(0.0, 1.0, HEAD_DIM // 2, dtype=jnp.float32))
    inv_lane = jnp.tile(jnp.repeat(inv, 2), LANES // HEAD_DIM)
    ang = jnp.arange(seq).astype(jnp.float32)[:, None] * inv_lane[None, :]
    cos, sin = jnp.cos(ang), jnp.sin(ang)
    even = (np.arange(LANES) % 2 == 0)[None, :]
    return cos, jnp.where(even, -sin, 0.0), jnp.where(even, 0.0, sin)


def _retention_tables():
    c = BLOCK
    f32 = np.float32
    scale = f32(HEAD_DIM ** -0.5)
    lg = np.log(f32(1.0) - f32(2.0) ** (f32(-5.0) - np.arange(RET_HEADS, dtype=f32))).astype(f32)
    idx = np.arange(c, dtype=f32)
    diff = idx[:, None] - idx[None, :]
    dmask = np.where(diff >= 0, np.exp(lg[:, None, None] * np.maximum(diff, f32(0.0))), f32(0.0)).astype(f32)
    dm = np.stack([np.concatenate([dmask[2 * j], dmask[2 * j + 1]], axis=1)
                   for j in range(RET_SLABS)]) * scale
    zeta = (np.exp(lg[:, None] * (f32(c) - f32(1.0) - idx)[None, :]) * scale).astype(f32)
    xi = np.exp(lg[None, :] * (idx[:, None] + f32(1.0))).astype(f32)
    g_chunk = np.exp(lg * f32(c)).astype(f32)
    head_of_lane = (np.arange(LANES) >= HALF).astype(np.int32)
    zeta_t = np.stack([zeta[2 * j + head_of_lane].T for j in range(RET_SLABS)])
    xi_t = np.stack([xi[:, 2 * j + head_of_lane] for j in range(RET_SLABS)])
    same = head_of_lane[:, None] == head_of_lane[None, :]
    gtab = np.stack([np.where(same, g_chunk[2 * j + head_of_lane][:, None], f32(0.0))
                     for j in range(RET_SLABS)]).astype(f32)
    return jnp.asarray(dm), jnp.asarray(zeta_t), jnp.asarray(xi_t), jnp.asarray(gtab)


def _pool_block_diag(pool_w):
    depth, groups = pool_w.shape[0], pool_w.shape[1]
    eye = jnp.eye(groups, dtype=jnp.bfloat16)[None, :, None, :, None]
    wide = pool_w.astype(jnp.bfloat16)[:, :, :, None, :] * eye
    return wide.reshape(depth, POOL_WIDTH, POOL_WIDTH)


def kernel(x, attn_norm_g, w_in, attn_sinks, rel_bias, pool_w, pool_scale, w_out, mlp_norm_g, w_up, w_down,
           final_norm_g):
    depth = w_in.shape[0]
    seq = x.shape[1]
    bf16 = jnp.bfloat16
    w_in_b = _prep_w_in(w_in)
    w_out_b = w_out.astype(bf16)
    w_pool = _pool_block_diag(pool_w)
    w_up_b = w_up.astype(bf16)
    w_down_b = w_down.astype(bf16)
    bias = _attention_bias(rel_bias)
    cos, sin_a, sin_b = _rotary_tables(seq)
    dm, zeta_t, xi_t, gtab = _retention_tables()
    g_attn = attn_norm_g.reshape(depth, 1, D_MODEL)
    g_mlp = mlp_norm_g.reshape(depth, 1, D_MODEL)
    p_scale = pool_scale.reshape(depth, 1, POOL_WIDTH)
    gf = final_norm_g.reshape(1, D_MODEL)
    for layer in range(depth):
        x = _layer_call(layer, layer == depth - 1, x, attn_sinks, g_attn, w_in_b, bias, cos, sin_a, sin_b, dm,
                        zeta_t, xi_t, gtab, w_pool, p_scale, w_out_b, g_mlp, w_up_b, w_down_b, gf)
    return x
```

```python
import functools

import jax
import jax.numpy as jnp
import numpy as np
from jax import lax
from jax.experimental import pallas as pl
from jax.experimental.pallas import tpu as pltpu

D_MODEL = 1024
HEAD_DIM = 64
ATT_Q_HEADS = 6
ATT_KV_HEADS = 2
ATT_GROUP = ATT_Q_HEADS // ATT_KV_HEADS
RET_HEADS = 6
POOL_WINDOWS = (2, 4, 8, 16)
POOL_GROUP_WIDTH = 64
POOL_WIDTH = 256
BLOCK = 128
N_BUCKETS = 32
MAX_DISTANCE = 128
D_FF = 4 * D_MODEL
RMS_EPS = 1e-6
ROPE_BASE = 10000.0
NEG_INF = -1e30

LANES = 128
HALF = LANES // 2
ATT_Q_W = ATT_Q_HEADS * HEAD_DIM
ATT_KV_W = ATT_KV_HEADS * HEAD_DIM
RET_W = RET_HEADS * HEAD_DIM
RET_SLABS = RET_W // LANES
ATT_SLABS = ATT_Q_W // LANES
POOL_TAIL = 16

C_QA = 0
C_KA = C_QA + ATT_Q_W
C_VA = C_KA + ATT_KV_W
C_QR = C_VA + ATT_KV_W
C_KR = C_QR + RET_W
C_VR = C_KR + RET_W
C_GR = C_VR + RET_W
C_UP = C_GR + RET_W
Z_WIDTH = C_UP + POOL_WIDTH

Y_A = 0
Y_R = ATT_Q_W
Y_P = ATT_Q_W + RET_W

MIXER_TILE = 1024
MIXER_SUB = 256
PROJ_CHUNK = 256
OUT_CHUNK = 256
W_SCORES, W_SOFTMAX, W_VALUES, W_RETENTION = 0.5, 1.0, 1.0, 1.5
W_POOL_NARROW, W_POOL_WIDE, W_POOL_MAPS = 1.0, 3.0, 0.5
MLP_TILE = 1024
FF_CHUNK = 1024
VMEM_LIMIT_BYTES = 56 * 1024 * 1024


def _rms_norm(x, g):
    ms = jnp.mean(x * x, axis=-1, keepdims=True)
    return x * lax.rsqrt(ms + RMS_EPS) * g


def _t5_bucket(dist):
    max_exact = N_BUCKETS // 2
    n = np.maximum(dist, 0)
    large = max_exact + (np.log(np.maximum(n, 1) / max_exact)
                         / np.log(MAX_DISTANCE / max_exact)
                         * (N_BUCKETS - max_exact)).astype(np.int64)
    large = np.minimum(large, N_BUCKETS - 1)
    return np.where(n < max_exact, n, large).astype(np.int32)


def _spread(first, second):
    keyed = []
    for order, tasks in enumerate((first, second)):
        total = sum(weight for weight, _ in tasks)
        done = 0.0
        for weight, task in tasks:
            keyed.append(((done + 0.5 * weight) / total, order, (weight, task)))
            done += weight
    return [item for _, _, item in sorted(keyed, key=lambda entry: entry[:2])]


def _mixer_kernel(sinks_ref, x_ref, xn_ref, g_ref, win_ref, bias_ref, cos_ref, sina_ref, sinb_ref, dm_ref,
                  zeta_ref, xi_ref, gtab_ref, wpool_ref, pscale_ref, wout_ref, o_ref,
                  z0_ref, z_ref, y_ref, kbuf, vbuf, ubuf, r_ref, *, tile, sub, layer):
    t = pl.program_id(1)
    npiece = tile // sub
    assert npiece >= 2, "piece 0 of the next tile is projected while the last piece is mixed"
    f32, bf16 = jnp.float32, jnp.bfloat16

    lane = lax.broadcasted_iota(jnp.int32, (BLOCK, LANES), 1)
    lo = lane < HALF

    def zrows(r0):
        return (z0_ref, r0) if r0 < sub else (z_ref, r0 - sub)

    def project_tasks(read_x, zdst, zrow, brow):
        cache = []

        def normed():
            if not cache:
                cache.append(_rms_norm(read_x(), g_ref[...]).astype(bf16))
            return cache[0]

        side = ((C_KA, ATT_KV_W, kbuf, BLOCK), (C_VA, ATT_KV_W, vbuf, BLOCK), (C_UP, POOL_WIDTH, ubuf, POOL_TAIL))
        chunks = [(c0, min(c0 + PROJ_CHUNK, Z_WIDTH)) for c0 in range(0, Z_WIDTH, PROJ_CHUNK)]

        def make(c0, c1):
            def task():
                z = jnp.dot(normed(), win_ref[:, c0:c1], preferred_element_type=f32)
                zdst[zrow:zrow + sub, c0:c1] = z
                for cs, width, buf, head in side:
                    a, b = max(c0, cs), min(c1, cs + width)
                    if a < b:
                        buf[head + brow:head + brow + sub, a - cs:b - cs] = z[:, a - c0:b - c0].astype(buf.dtype)
            return task

        return [((c1 - c0) / PROJ_CHUNK, make(c0, c1)) for c0, c1 in chunks]

    @pl.when(t == 0)
    def _():
        kbuf[0:BLOCK, :] = jnp.zeros((BLOCK, ATT_KV_W), bf16)
        vbuf[0:BLOCK, :] = jnp.zeros((BLOCK, ATT_KV_W), bf16)
        ubuf[0:POOL_TAIL, :] = jnp.zeros((POOL_TAIL, POOL_WIDTH), f32)
        r_ref[...] = jnp.zeros(r_ref.shape, f32)
        for _, task in project_tasks(lambda: x_ref[0, 0:sub, :], z0_ref, 0, 0):
            task()

    def attention_tasks(r0):
        live = {"ps": [], "denoms": []}

        def swap_halves(v):
            return pltpu.roll(v, HALF, 1)

        def scores():
            zr, zo = zrows(r0)
            qs = [zr[zo:zo + BLOCK, C_QA + j * LANES:C_QA + (j + 1) * LANES] for j in range(ATT_SLABS)]
            swapped = {}
            parts = []
            for hd in range(ATT_Q_HEADS):
                slab, half, group = hd // 2, hd % 2, hd // ATT_GROUP
                q = qs[slab]
                if half != group:
                    q = swapped.setdefault(slab, swap_halves(q))
                parts.append(jnp.where(lo, q, 0.0) if group == 0 else jnp.where(lo, 0.0, q))
            qstack = jnp.concatenate(parts, axis=0).astype(bf16)
            kk = kbuf[r0:r0 + 2 * BLOCK, :]
            s = lax.dot_general(qstack, kk, (((1,), (1,)), ((), ())), preferred_element_type=f32)
            live["s"] = s + (bias_ref[jnp.where(t == 0, 1, 0)] if r0 == 0 else bias_ref[0])

        def softmax(heads):
            for hd in heads:
                sh = live["s"][hd * BLOCK:(hd + 1) * BLOCK]
                sink = sinks_ref[layer, hd]
                m = jnp.maximum(jnp.max(sh, axis=-1, keepdims=True), sink)
                p = jnp.exp(sh - m)
                live["denoms"].append(jnp.sum(p, axis=-1, keepdims=True) + jnp.exp(sink - m))
                live["ps"].append(p.astype(bf16))

        def weighted_values():
            denoms = live["denoms"]
            pstack = jnp.concatenate(live["ps"], axis=0)
            o = jnp.dot(pstack, vbuf[r0:r0 + 2 * BLOCK, :], preferred_element_type=f32)
            halves = []
            for hd in range(ATT_Q_HEADS):
                oh = o[hd * BLOCK:(hd + 1) * BLOCK] / denoms[hd]
                halves.append(oh if hd % 2 == hd // ATT_GROUP else swap_halves(oh))
            for j in range(ATT_SLABS):
                y_ref[r0:r0 + BLOCK, Y_A + j * LANES:Y_A + (j + 1) * LANES] = (
                    jnp.where(lo, halves[2 * j], halves[2 * j + 1]).astype(bf16))

        return ([(W_SCORES, scores)] + [(W_SOFTMAX, functools.partial(softmax, (hd,))) for hd in range(ATT_Q_HEADS)]
                + [(W_VALUES, weighted_values)])

    def rotary(v, cos, sin_a, sin_b):
        return v * cos + pltpu.roll(v, LANES - 1, 1) * sin_a + pltpu.roll(v, 1, 1) * sin_b

    def retention(r0, j):
        zr, zo = zrows(r0)
        cos = cos_ref[r0:r0 + BLOCK, :]
        sin_a = sina_ref[r0:r0 + BLOCK, :]
        sin_b = sinb_ref[r0:r0 + BLOCK, :]
        c0 = j * LANES
        q = rotary(zr[zo:zo + BLOCK, C_QR + c0:C_QR + c0 + LANES], cos, sin_a, sin_b)
        k = rotary(zr[zo:zo + BLOCK, C_KR + c0:C_KR + c0 + LANES], cos, sin_a, sin_b)
        v = zr[zo:zo + BLOCK, C_VR + c0:C_VR + c0 + LANES]
        gate = zr[zo:zo + BLOCK, C_GR + c0:C_GR + c0 + LANES]
        qb = q.astype(bf16)
        kb = k.astype(bf16)
        ksplit = jnp.concatenate([jnp.where(lo, k, 0.0), jnp.where(lo, 0.0, k)], axis=0).astype(bf16)
        sc = lax.dot_general(qb, ksplit, (((1,), (1,)), ((), ())), preferred_element_type=f32)
        acat = (sc * dm_ref[j]).astype(bf16)
        vsplit = jnp.concatenate([jnp.where(lo, v, 0.0), jnp.where(lo, 0.0, v)], axis=0).astype(bf16)
        inner = jnp.dot(acat, vsplit, preferred_element_type=f32)
        r_prev = r_ref[j]
        cross = jnp.dot(qb, r_prev.astype(bf16), preferred_element_type=f32) * xi_ref[j]
        o = inner + cross
        vz = (v * zeta_ref[j]).astype(bf16)
        u = lax.dot_general(kb, vz, (((0,), (0,)), ((), ())), preferred_element_type=f32)
        gt = gtab_ref[j]
        r_ref[j] = r_prev * gt + jnp.where(gt > 0.0, u, 0.0)
        o2 = o * o
        ms_lo = jnp.sum(jnp.where(lo, o2, 0.0), axis=-1, keepdims=True)
        ms_hi = jnp.sum(jnp.where(lo, 0.0, o2), axis=-1, keepdims=True)
        ms = jnp.where(lo, ms_lo, ms_hi) * (1.0 / HEAD_DIM)
        on = o * lax.rsqrt(ms + RMS_EPS)
        y_ref[r0:r0 + BLOCK, Y_R + c0:Y_R + c0 + LANES] = (jax.nn.silu(gate) * on).astype(bf16)

    def pool_tasks(r0):
        pooled = {}

        def window(sl):
            lo_t = lax.broadcasted_iota(jnp.int32, (sub, LANES), 1) < HALF
            pos1 = lax.broadcasted_iota(jnp.int32, (sub, LANES), 0) + (t * tile + r0 + 1)
            w_small, w_big = POOL_WINDOWS[2 * sl], POOL_WINDOWS[2 * sl + 1]

            def shifted(d):
                return ubuf[POOL_TAIL + r0 - d:POOL_TAIL + r0 - d + sub, sl * LANES:(sl + 1) * LANES]

            u0 = shifted(0)
            acc = u0
            for d in range(1, w_small):
                acc = acc + shifted(d)
            small = acc
            for d in range(w_small, w_big):
                acc = acc + shifted(d)
            win = jnp.where(lo_t, small, acc)
            cnt = jnp.minimum(pos1, jnp.where(lo_t, w_small, w_big)).astype(f32)
            pooled[sl] = (win / cnt - u0).astype(bf16)

        def group_maps():
            both = jnp.concatenate([pooled[0], pooled[1]], axis=1)
            mixed = jnp.dot(both, wpool_ref[...], preferred_element_type=f32) * pscale_ref[...]
            y_ref[r0:r0 + sub, Y_P:Y_P + POOL_WIDTH] = mixed.astype(bf16)

        return [(W_POOL_NARROW, functools.partial(window, 0)), (W_POOL_WIDE, functools.partial(window, 1)),
                (W_POOL_MAPS, group_maps)]

    def output_tasks(r0):
        def make(c0):
            def task():
                o_ref[0, r0:r0 + sub, c0:c0 + OUT_CHUNK] = x_ref[0, r0:r0 + sub, c0:c0 + OUT_CHUNK] + jnp.dot(
                    y_ref[r0:r0 + sub, :], wout_ref[:, c0:c0 + OUT_CHUNK], preferred_element_type=f32)
            return task
        return [(1.0, make(c0)) for c0 in range(0, D_MODEL, OUT_CHUNK)]

    def mix_tasks(r0):
        tasks = []
        for b0 in range(r0, r0 + sub, BLOCK):
            tasks += attention_tasks(b0)
            tasks += [(W_RETENTION, functools.partial(retention, b0, j)) for j in range(RET_SLABS)]
        return tasks + pool_tasks(r0)

    for p in range(npiece):
        r0 = p * sub
        if p + 1 < npiece:
            matmuls = project_tasks(lambda r=r0 + sub: x_ref[0, r:r + sub, :], z_ref, r0, r0 + sub)
        else:
            matmuls = project_tasks(lambda: xn_ref[0], z0_ref, 0, tile)
        if p >= 1:
            matmuls = _spread(matmuls, output_tasks(r0 - sub))
        for _, task in _spread(matmuls, mix_tasks(r0)):
            task()
    for _, task in output_tasks(tile - sub):
        task()

    kbuf[0:BLOCK + sub, :] = kbuf[tile:tile + BLOCK + sub, :]
    vbuf[0:BLOCK + sub, :] = vbuf[tile:tile + BLOCK + sub, :]
    ubuf[0:POOL_TAIL + sub, :] = ubuf[tile:tile + POOL_TAIL + sub, :]


_RESIDENT = pl.Buffered(1)


def _const_spec(shape):
    nd = len(shape)
    return pl.BlockSpec(shape, lambda b, t, _nd=nd: (0,) * _nd, pipeline_mode=_RESIDENT)


def _layer_spec(shape, layer):
    nd = len(shape)
    return pl.BlockSpec((None,) + tuple(shape[1:]), lambda b, t, _nd=nd: (layer,) + (0,) * (_nd - 1),
                        pipeline_mode=_RESIDENT)


def _mixer_call(layer, x, sinks, g, win, bias, cos, sin_a, sin_b, dm, zeta, xi, gtab, wpool, pscale, wout):
    batch, seq, _ = x.shape
    tile, sub = MIXER_TILE, MIXER_SUB
    grid = (batch, seq // tile)
    last_piece = seq // sub - 1
    f32, bf16 = jnp.float32, jnp.bfloat16
    rot_spec = pl.BlockSpec((tile, LANES), lambda b, t: (t, 0))
    in_specs = [
        pl.BlockSpec(memory_space=pltpu.SMEM),
        pl.BlockSpec((1, tile, D_MODEL), lambda b, t: (b, t, 0)),
        pl.BlockSpec((1, sub, D_MODEL), lambda b, t: (b, jnp.minimum((t + 1) * (tile // sub), last_piece), 0)),
        _layer_spec(g.shape, layer), _const_spec(win.shape), _const_spec(bias.shape),
        rot_spec, rot_spec, rot_spec,
        _const_spec(dm.shape), _const_spec(zeta.shape), _const_spec(xi.shape), _const_spec(gtab.shape),
        _layer_spec(wpool.shape, layer), _layer_spec(pscale.shape, layer), _const_spec(wout.shape),
    ]
    return pl.pallas_call(
        functools.partial(_mixer_kernel, tile=tile, sub=sub, layer=layer),
        grid=grid,
        in_specs=in_specs,
        out_specs=pl.BlockSpec((1, tile, D_MODEL), lambda b, t: (b, t, 0)),
        out_shape=jax.ShapeDtypeStruct(x.shape, x.dtype),
        scratch_shapes=[
            pltpu.VMEM((sub, Z_WIDTH), f32),
            pltpu.VMEM((tile - sub, Z_WIDTH), f32),
            pltpu.VMEM((tile, D_MODEL), bf16),
            pltpu.VMEM((BLOCK + tile + sub, ATT_KV_W), bf16),
            pltpu.VMEM((BLOCK + tile + sub, ATT_KV_W), bf16),
            pltpu.VMEM((POOL_TAIL + tile + sub, POOL_WIDTH), f32),
            pltpu.VMEM((RET_SLABS, LANES, LANES), f32),
        ],
        compiler_params=pltpu.CompilerParams(
            dimension_semantics=("arbitrary", "arbitrary"),
            vmem_limit_bytes=VMEM_LIMIT_BYTES),
        name="mixer",
    )(sinks, x, x, g, win, bias, cos, sin_a, sin_b, dm, zeta, xi, gtab, wpool, pscale, wout)


def _mlp_kernel(x_ref, g_ref, wup_ref, wdown_ref, gf_ref, *rest, final_norm, cast_next):
    f32, bf16 = jnp.float32, jnp.bfloat16
    o_ref = rest[len(rest) // 2] if cast_next else rest[0]
    x = x_ref[0]
    h = _rms_norm(x, g_ref[...]).astype(bf16)
    acc = x
    for c in range(D_FF // FF_CHUNK):
        a = jnp.dot(h, wup_ref[:, c * FF_CHUNK:(c + 1) * FF_CHUNK], preferred_element_type=f32)
        a = jnp.square(jnp.maximum(a, 0.0)).astype(bf16)
        acc = acc + jnp.dot(a, wdown_ref[c * FF_CHUNK:(c + 1) * FF_CHUNK, :], preferred_element_type=f32)
    if final_norm:
        acc = _rms_norm(acc, gf_ref[...])
    o_ref[0] = acc
    if cast_next:
        scale_ref, nwin_ref, nwout_ref, nwup_ref, nwdown_ref, _, owin_ref, owout_ref, owup_ref, owdown_ref = rest
        owin_ref[...] = (nwin_ref[...] * scale_ref[...]).astype(bf16)
        owout_ref[...] = nwout_ref[...].astype(bf16)
        owup_ref[...] = nwup_ref[...].astype(bf16)
        owdown_ref[...] = nwdown_ref[...].astype(bf16)


def _mlp_call(layer, x, g, wup, wdown, gf, final_norm, next_weights=None):
    batch, seq, _ = x.shape
    tile = MLP_TILE
    ntiles = seq // tile
    grid = (batch, ntiles)
    x_spec = pl.BlockSpec((1, tile, D_MODEL), lambda b, t: (b, t, 0))
    in_specs = [x_spec, _layer_spec(g.shape, layer), _const_spec(wup.shape), _const_spec(wdown.shape),
                _const_spec(gf.shape)]
    out_specs = [x_spec]
    out_shape = [jax.ShapeDtypeStruct(x.shape, x.dtype)]
    operands = [x, g, wup, wdown, gf]
    if next_weights is not None:
        col_scale, *stacked = next_weights
        steps = batch * ntiles
        in_specs.append(_const_spec(col_scale.shape))
        operands.append(col_scale)
        for w in stacked:
            rows = w.shape[1] // steps
            in_specs.append(pl.BlockSpec((None, rows, w.shape[2]), lambda b, t: (layer + 1, b * ntiles + t, 0)))
            out_specs.append(pl.BlockSpec((rows, w.shape[2]), lambda b, t: (b * ntiles + t, 0)))
            out_shape.append(jax.ShapeDtypeStruct(w.shape[1:], jnp.bfloat16))
            operands.append(w)
    outs = pl.pallas_call(
        functools.partial(_mlp_kernel, final_norm=final_norm, cast_next=next_weights is not None),
        grid=grid,
        in_specs=in_specs,
        out_specs=out_specs,
        out_shape=out_shape,
        compiler_params=pltpu.CompilerParams(
            dimension_semantics=("arbitrary", "arbitrary"),
            vmem_limit_bytes=VMEM_LIMIT_BYTES),
        name="mlp",
    )(*operands)
    return outs[0], (tuple(outs[1:]) if next_weights is not None else None)


def _w_in_col_scale():
    col_scale = np.ones((1, Z_WIDTH), np.float32)
    col_scale[:, C_QA:C_QA + ATT_Q_W] = HEAD_DIM ** -0.5
    return jnp.asarray(col_scale)


def _attention_bias(rel_bias):
    i = np.arange(BLOCK)[:, None]
    j = np.arange(2 * BLOCK)[None, :]
    dist = BLOCK + i - j
    band = (dist >= 0) & (dist < BLOCK)
    onehot = (_t5_bucket(dist).reshape(1, -1) == np.arange(N_BUCKETS)[:, None]).astype(np.float32)
    bias = jnp.dot(rel_bias.astype(jnp.float32).T, onehot, precision=lax.Precision.HIGHEST)
    bias = bias.reshape(ATT_Q_HEADS, BLOCK, 2 * BLOCK)
    normal = jnp.where(jnp.asarray(band)[None], bias, NEG_INF)
    first = jnp.where(jnp.asarray(band & (j >= BLOCK))[None], bias, NEG_INF)
    return jnp.stack([normal, first]).reshape(2, ATT_Q_HEADS * BLOCK, 2 * BLOCK)


def _rotary_tables(seq):
    f32 = np.float32
    inv = (f32(1.0) / np.power(f32(ROPE_BASE), np.linspace(0.0, 1.0, HEAD_DIM // 2, dtype=f32))).astype(f32)
    inv_lane = np.tile(np.repeat(inv, 2), LANES // HEAD_DIM)
    ang = (np.arange(seq, dtype=f32)[:, None] * inv_lane[None, :]).astype(f32)
    cos, sin = np.cos(ang).astype(f32), np.sin(ang).astype(f32)
    even = (np.arange(LANES) % 2 == 0)[None, :]
    zero = f32(0.0)
    return jnp.asarray(cos), jnp.asarray(np.where(even, -sin, zero)), jnp.asarray(np.where(even, zero, sin))


def _retention_tables():
    c = BLOCK
    f32 = np.float32
    scale = f32(HEAD_DIM ** -0.5)
    lg = np.log(f32(1.0) - f32(2.0) ** (f32(-5.0) - np.arange(RET_HEADS, dtype=f32))).astype(f32)
    idx = np.arange(c, dtype=f32)
    diff = idx[:, None] - idx[None, :]
    dmask = np.where(diff >= 0, np.exp(lg[:, None, None] * np.maximum(diff, f32(0.0))), f32(0.0)).astype(f32)
    dm = np.stack([np.concatenate([dmask[2 * j], dmask[2 * j + 1]], axis=1)
                   for j in range(RET_SLABS)]) * scale
    zeta = (np.exp(lg[:, None] * (f32(c) - f32(1.0) - idx)[None, :]) * scale).astype(f32)
    xi = np.exp(lg[None, :] * (idx[:, None] + f32(1.0))).astype(f32)
    g_chunk = np.exp(lg * f32(c)).astype(f32)
    head_of_lane = (np.arange(LANES) >= HALF).astype(np.int32)
    zeta_t = np.stack([zeta[2 * j + head_of_lane].T for j in range(RET_SLABS)])
    xi_t = np.stack([xi[:, 2 * j + head_of_lane] for j in range(RET_SLABS)])
    same = head_of_lane[:, None] == head_of_lane[None, :]
    gtab = np.stack([np.where(same, g_chunk[2 * j + head_of_lane][:, None], f32(0.0))
                     for j in range(RET_SLABS)]).astype(f32)
    return jnp.asarray(dm), jnp.asarray(zeta_t), jnp.asarray(xi_t), jnp.asarray(gtab)


def _pool_block_diag(pool_w):
    depth, groups = pool_w.shape[0], pool_w.shape[1]
    eye = jnp.eye(groups, dtype=jnp.bfloat16)[None, :, None, :, None]
    wide = pool_w.astype(jnp.bfloat16)[:, :, :, None, :] * eye
    return wide.reshape(depth, POOL_WIDTH, POOL_WIDTH)


def kernel(x, attn_norm_g, w_in, attn_sinks, rel_bias, pool_w, pool_scale, w_out, mlp_norm_g, w_up, w_down,
           final_norm_g):
    depth = w_in.shape[0]
    seq = x.shape[1]
    bf16 = jnp.bfloat16
    col_scale = _w_in_col_scale()
    weights = ((w_in[0] * col_scale).astype(bf16), w_out[0].astype(bf16), w_up[0].astype(bf16),
               w_down[0].astype(bf16))
    w_pool = _pool_block_diag(pool_w)
    bias = _attention_bias(rel_bias)
    cos, sin_a, sin_b = _rotary_tables(seq)
    dm, zeta_t, xi_t, gtab = _retention_tables()
    g_attn = attn_norm_g.reshape(depth, 1, D_MODEL)
    g_mlp = mlp_norm_g.reshape(depth, 1, D_MODEL)
    p_scale = pool_scale.reshape(depth, 1, POOL_WIDTH)
    gf = final_norm_g.reshape(1, D_MODEL)
    for layer in range(depth):
        last = layer == depth - 1
        w_in_b, w_out_b, w_up_b, w_down_b = weights
        x = _mixer_call(layer, x, attn_sinks, g_attn, w_in_b, bias, cos, sin_a, sin_b, dm, zeta_t, xi_t, gtab,
                        w_pool, p_scale, w_out_b)
        x, weights = _mlp_call(layer, x, g_mlp, w_up_b, w_down_b, gf, final_norm=last,
                               next_weights=None if last else (col_scale, w_in, w_out, w_up, w_down))
    return x
```

```python
import functools

import jax
import jax.numpy as jnp
import numpy as np
from jax import lax
from jax.experimental import pallas as pl
from jax.experimental.pallas import tpu as pltpu

D_MODEL = 1024
HEAD_DIM = 64
ATT_Q_HEADS = 6
ATT_KV_HEADS = 2
ATT_GROUP = ATT_Q_HEADS // ATT_KV_HEADS
RET_HEADS = 6
POOL_WINDOWS = (2, 4, 8, 16)
POOL_GROUP_WIDTH = 64
POOL_WIDTH = 256
BLOCK = 128
N_BUCKETS = 32
MAX_DISTANCE = 128
D_FF = 4 * D_MODEL
RMS_EPS = 1e-6
ROPE_BASE = 10000.0
NEG_INF = -1e30

LANES = 128
HALF = LANES // 2
ATT_Q_W = ATT_Q_HEADS * HEAD_DIM
ATT_KV_W = ATT_KV_HEADS * HEAD_DIM
RET_W = RET_HEADS * HEAD_DIM
RET_SLABS = RET_W // LANES
ATT_SLABS = ATT_Q_W // LANES
POOL_TAIL = 16

C_QA = 0
C_KA = C_QA + ATT_Q_W
C_VA = C_KA + ATT_KV_W
C_QR = C_VA + ATT_KV_W
C_KR = C_QR + RET_W
C_VR = C_KR + RET_W
C_GR = C_VR + RET_W
C_UP = C_GR + RET_W
Z_WIDTH = C_UP + POOL_WIDTH

Y_A = 0
Y_R = ATT_Q_W
Y_P = ATT_Q_W + RET_W

MIXER_TILE = 1024
MIXER_SUB = 256
PROJ_CHUNK = 256
OUT_CHUNK = 256
W_SCORES, W_SOFTMAX, W_VALUES, W_RETENTION = 0.5, 1.0, 1.0, 1.5
W_POOL_NARROW, W_POOL_WIDE, W_POOL_MAPS = 1.0, 3.0, 0.5
MLP_TILE = 1024
FF_CHUNK = 1024
VMEM_LIMIT_BYTES = 56 * 1024 * 1024


def _rms_norm(x, g):
    ms = jnp.mean(x * x, axis=-1, keepdims=True)
    return x * lax.rsqrt(ms + RMS_EPS) * g


def _t5_bucket(dist):
    max_exact = N_BUCKETS // 2
    n = np.maximum(dist, 0)
    large = max_exact + (np.log(np.maximum(n, 1) / max_exact)
                         / np.log(MAX_DISTANCE / max_exact)
                         * (N_BUCKETS - max_exact)).astype(np.int64)
    large = np.minimum(large, N_BUCKETS - 1)
    return np.where(n < max_exact, n, large).astype(np.int32)


def _spread(first, second):
    keyed = []
    for order, tasks in enumerate((first, second)):
        total = sum(weight for weight, _ in tasks)
        done = 0.0
        for weight, task in tasks:
            keyed.append(((done + 0.5 * weight) / total, order, (weight, task)))
            done += weight
    return [item for _, _, item in sorted(keyed, key=lambda entry: entry[:2])]


def _mixer_kernel(sinks_ref, x_ref, xn_ref, g_ref, win_ref, bias_ref, cos_ref, sin_ref, dm_ref,
                  zeta_ref, xi_ref, gtab_ref, wpool_ref, pscale_ref, wout_ref, *rest, tile, sub, layer, cast_mlp):
    if cast_mlp:
        wup_f32_ref, wdown_f32_ref, o_ref, wup_bf16_ref, wdown_bf16_ref, *scratch = rest
    else:
        o_ref, *scratch = rest
    z0_ref, z_ref, y_ref, kbuf, vbuf, ubuf, r_ref = scratch
    t = pl.program_id(1)
    npiece = tile // sub
    assert npiece >= 2, "piece 0 of the next tile is projected while the last piece is mixed"
    f32, bf16 = jnp.float32, jnp.bfloat16

    lane = lax.broadcasted_iota(jnp.int32, (BLOCK, LANES), 1)
    lo = lane < HALF

    def zrows(r0):
        return (z0_ref, r0) if r0 < sub else (z_ref, r0 - sub)

    def project_tasks(read_x, zdst, zrow, brow):
        cache = []

        def normed():
            if not cache:
                cache.append(_rms_norm(read_x(), g_ref[...]).astype(bf16))
            return cache[0]

        side = ((C_KA, ATT_KV_W, kbuf, BLOCK), (C_VA, ATT_KV_W, vbuf, BLOCK), (C_UP, POOL_WIDTH, ubuf, POOL_TAIL))
        chunks = [(c0, min(c0 + PROJ_CHUNK, Z_WIDTH)) for c0 in range(0, Z_WIDTH, PROJ_CHUNK)]

        def make(c0, c1):
            def task():
                z = jnp.dot(normed(), win_ref[:, c0:c1], preferred_element_type=f32)
                zdst[zrow:zrow + sub, c0:c1] = z
                for cs, width, buf, head in side:
                    a, b = max(c0, cs), min(c1, cs + width)
                    if a < b:
                        buf[head + brow:head + brow + sub, a - cs:b - cs] = z[:, a - c0:b - c0].astype(buf.dtype)
            return task

        return [((c1 - c0) / PROJ_CHUNK, make(c0, c1)) for c0, c1 in chunks]

    @pl.when(t == 0)
    def _():
        kbuf[0:BLOCK, :] = jnp.zeros((BLOCK, ATT_KV_W), bf16)
        vbuf[0:BLOCK, :] = jnp.zeros((BLOCK, ATT_KV_W), bf16)
        ubuf[0:POOL_TAIL, :] = jnp.zeros((POOL_TAIL, POOL_WIDTH), f32)
        r_ref[...] = jnp.zeros(r_ref.shape, f32)
        for _, task in project_tasks(lambda: x_ref[0, 0:sub, :], z0_ref, 0, 0):
            task()

    def attention_tasks(r0):
        live = {"ps": [], "denoms": []}

        def swap_halves(v):
            return pltpu.roll(v, HALF, 1)

        def scores():
            zr, zo = zrows(r0)
            qs = [zr[zo:zo + BLOCK, C_QA + j * LANES:C_QA + (j + 1) * LANES] for j in range(ATT_SLABS)]
            swapped = {}
            parts = []
            for hd in range(ATT_Q_HEADS):
                slab, half, group = hd // 2, hd % 2, hd // ATT_GROUP
                q = qs[slab]
                if half != group:
                    q = swapped.setdefault(slab, swap_halves(q))
                parts.append(jnp.where(lo, q, 0.0) if group == 0 else jnp.where(lo, 0.0, q))
            qstack = jnp.concatenate(parts, axis=0).astype(bf16)
            kk = kbuf[r0:r0 + 2 * BLOCK, :]
            s = lax.dot_general(qstack, kk, (((1,), (1,)), ((), ())), preferred_element_type=f32)
            live["s"] = s + (bias_ref[jnp.where(t == 0, 1, 0)] if r0 == 0 else bias_ref[0])

        def softmax(heads):
            for hd in heads:
                sh = live["s"][hd * BLOCK:(hd + 1) * BLOCK]
                sink = sinks_ref[layer, hd]
                m = jnp.maximum(jnp.max(sh, axis=-1, keepdims=True), sink)
                p = jnp.exp(sh - m)
                live["denoms"].append(jnp.sum(p, axis=-1, keepdims=True) + jnp.exp(sink - m))
                live["ps"].append(p.astype(bf16))

        def weighted_values():
            denoms = live["denoms"]
            pstack = jnp.concatenate(live["ps"], axis=0)
            o = jnp.dot(pstack, vbuf[r0:r0 + 2 * BLOCK, :], preferred_element_type=f32)
            halves = []
            for hd in range(ATT_Q_HEADS):
                oh = o[hd * BLOCK:(hd + 1) * BLOCK] / denoms[hd]
                halves.append(oh if hd % 2 == hd // ATT_GROUP else swap_halves(oh))
            for j in range(ATT_SLABS):
                y_ref[r0:r0 + BLOCK, Y_A + j * LANES:Y_A + (j + 1) * LANES] = (
                    jnp.where(lo, halves[2 * j], halves[2 * j + 1]).astype(bf16))

        return ([(W_SCORES, scores)] + [(W_SOFTMAX, functools.partial(softmax, (hd,))) for hd in range(ATT_Q_HEADS)]
                + [(W_VALUES, weighted_values)])

    even = lane % 2 == 0

    def rotary(v, cos, sin):
        partner = jnp.where(even, pltpu.roll(v, LANES - 1, 1), pltpu.roll(v, 1, 1))
        return v * cos + partner * sin

    def retention(r0, j):
        zr, zo = zrows(r0)
        cos = cos_ref[r0:r0 + BLOCK, :]
        sin = sin_ref[r0:r0 + BLOCK, :]
        c0 = j * LANES
        q = rotary(zr[zo:zo + BLOCK, C_QR + c0:C_QR + c0 + LANES], cos, sin)
        k = rotary(zr[zo:zo + BLOCK, C_KR + c0:C_KR + c0 + LANES], cos, sin)
        v = zr[zo:zo + BLOCK, C_VR + c0:C_VR + c0 + LANES]
        gate = zr[zo:zo + BLOCK, C_GR + c0:C_GR + c0 + LANES]
        qb = q.astype(bf16)
        kb = k.astype(bf16)
        ksplit = jnp.concatenate([jnp.where(lo, k, 0.0), jnp.where(lo, 0.0, k)], axis=0).astype(bf16)
        sc = lax.dot_general(qb, ksplit, (((1,), (1,)), ((), ())), preferred_element_type=f32)
        acat = (sc * dm_ref[j]).astype(bf16)
        vsplit = jnp.concatenate([jnp.where(lo, v, 0.0), jnp.where(lo, 0.0, v)], axis=0).astype(bf16)
        inner = jnp.dot(acat, vsplit, preferred_element_type=f32)
        r_prev = r_ref[j]
        cross = jnp.dot(qb, r_prev.astype(bf16), preferred_element_type=f32) * xi_ref[j]
        o = inner + cross
        vz = (v * zeta_ref[j]).astype(bf16)
        u = lax.dot_general(kb, vz, (((0,), (0,)), ((), ())), preferred_element_type=f32)
        gt = gtab_ref[j]
        r_ref[j] = r_prev * gt + jnp.where(gt > 0.0, u, 0.0)
        o2 = o * o
        ms_lo = jnp.sum(jnp.where(lo, o2, 0.0), axis=-1, keepdims=True)
        ms_hi = jnp.sum(jnp.where(lo, 0.0, o2), axis=-1, keepdims=True)
        ms = jnp.where(lo, ms_lo, ms_hi) * (1.0 / HEAD_DIM)
        on = o * lax.rsqrt(ms + RMS_EPS)
        y_ref[r0:r0 + BLOCK, Y_R + c0:Y_R + c0 + LANES] = (jax.nn.silu(gate) * on).astype(bf16)

    def pool_tasks(r0):
        pooled = {}

        def window(sl):
            lo_t = lax.broadcasted_iota(jnp.int32, (sub, LANES), 1) < HALF
            pos1 = lax.broadcasted_iota(jnp.int32, (sub, LANES), 0) + (t * tile + r0 + 1)
            w_small, w_big = POOL_WINDOWS[2 * sl], POOL_WINDOWS[2 * sl + 1]

            def shifted(d):
                return ubuf[POOL_TAIL + r0 - d:POOL_TAIL + r0 - d + sub, sl * LANES:(sl + 1) * LANES]

            u0 = shifted(0)
            acc = u0
            for d in range(1, w_small):
                acc = acc + shifted(d)
            small = acc
            for d in range(w_small, w_big):
                acc = acc + shifted(d)
            win = jnp.where(lo_t, small, acc)
            cnt = jnp.minimum(pos1, jnp.where(lo_t, w_small, w_big)).astype(f32)
            pooled[sl] = (win / cnt - u0).astype(bf16)

        def group_maps():
            both = jnp.concatenate([pooled[0], pooled[1]], axis=1)
            mixed = jnp.dot(both, wpool_ref[...], preferred_element_type=f32) * pscale_ref[...]
            y_ref[r0:r0 + sub, Y_P:Y_P + POOL_WIDTH] = mixed.astype(bf16)

        return [(W_POOL_NARROW, functools.partial(window, 0)), (W_POOL_WIDE, functools.partial(window, 1)),
                (W_POOL_MAPS, group_maps)]

    def output_tasks(r0):
        def make(c0):
            def task():
                o_ref[0, r0:r0 + sub, c0:c0 + OUT_CHUNK] = x_ref[0, r0:r0 + sub, c0:c0 + OUT_CHUNK] + jnp.dot(
                    y_ref[r0:r0 + sub, :], wout_ref[:, c0:c0 + OUT_CHUNK], preferred_element_type=f32)
            return task
        return [(1.0, make(c0)) for c0 in range(0, D_MODEL, OUT_CHUNK)]

    def mix_tasks(r0):
        tasks = []
        for b0 in range(r0, r0 + sub, BLOCK):
            tasks += attention_tasks(b0)
            tasks += [(W_RETENTION, functools.partial(retention, b0, j)) for j in range(RET_SLABS)]
        return tasks + pool_tasks(r0)

    for p in range(npiece):
        r0 = p * sub
        if p + 1 < npiece:
            matmuls = project_tasks(lambda r=r0 + sub: x_ref[0, r:r + sub, :], z_ref, r0, r0 + sub)
        else:
            matmuls = project_tasks(lambda: xn_ref[0], z0_ref, 0, tile)
        if p >= 1:
            matmuls = _spread(matmuls, output_tasks(r0 - sub))
        for _, task in _spread(matmuls, mix_tasks(r0)):
            task()
    for _, task in output_tasks(tile - sub):
        task()

    kbuf[0:BLOCK + sub, :] = kbuf[tile:tile + BLOCK + sub, :]
    vbuf[0:BLOCK + sub, :] = vbuf[tile:tile + BLOCK + sub, :]
    ubuf[0:POOL_TAIL + sub, :] = ubuf[tile:tile + POOL_TAIL + sub, :]

    if cast_mlp:
        wup_bf16_ref[...] = wup_f32_ref[...].astype(bf16)
        wdown_bf16_ref[...] = wdown_f32_ref[...].astype(bf16)


_RESIDENT = pl.Buffered(1)


def _const_spec(shape):
    nd = len(shape)
    return pl.BlockSpec(shape, lambda b, t, _nd=nd: (0,) * _nd, pipeline_mode=_RESIDENT)


def _layer_spec(shape, layer):
    nd = len(shape)
    return pl.BlockSpec((None,) + tuple(shape[1:]), lambda b, t, _nd=nd: (layer,) + (0,) * (_nd - 1),
                        pipeline_mode=_RESIDENT)


def _mixer_call(layer, x, sinks, g, win, bias, cos, sin, dm, zeta, xi, gtab, wpool, pscale, wout, mlp_weights=None):
    batch, seq, _ = x.shape
    tile, sub = MIXER_TILE, MIXER_SUB
    ntiles = seq // tile
    grid = (batch, ntiles)
    last_piece = seq // sub - 1
    f32, bf16 = jnp.float32, jnp.bfloat16
    rot_spec = pl.BlockSpec((tile, LANES), lambda b, t: (t, 0))
    x_spec = pl.BlockSpec((1, tile, D_MODEL), lambda b, t: (b, t, 0))
    in_specs = [
        pl.BlockSpec(memory_space=pltpu.SMEM),
        x_spec,
        pl.BlockSpec((1, sub, D_MODEL), lambda b, t: (b, jnp.minimum((t + 1) * (tile // sub), last_piece), 0)),
        _layer_spec(g.shape, layer), _const_spec(win.shape), _const_spec(bias.shape),
        rot_spec, rot_spec,
        _const_spec(dm.shape), _const_spec(zeta.shape), _const_spec(xi.shape), _const_spec(gtab.shape),
        _layer_spec(wpool.shape, layer), _layer_spec(pscale.shape, layer), _const_spec(wout.shape),
    ]
    operands = [sinks, x, x, g, win, bias, cos, sin, dm, zeta, xi, gtab, wpool, pscale, wout]
    out_specs = [x_spec]
    out_shape = [jax.ShapeDtypeStruct(x.shape, x.dtype)]
    if mlp_weights is not None:
        steps = batch * ntiles
        for w in mlp_weights:
            rows = w.shape[1] // steps
            in_specs.append(pl.BlockSpec((None, rows, w.shape[2]), lambda b, t: (layer, b * ntiles + t, 0)))
            out_specs.append(pl.BlockSpec((rows, w.shape[2]), lambda b, t: (b * ntiles + t, 0)))
            out_shape.append(jax.ShapeDtypeStruct(w.shape[1:], bf16))
            operands.append(w)
    outs = pl.pallas_call(
        functools.partial(_mixer_kernel, tile=tile, sub=sub, layer=layer, cast_mlp=mlp_weights is not None),
        grid=grid,
        in_specs=in_specs,
        out_specs=out_specs,
        out_shape=out_shape,
        scratch_shapes=[
            pltpu.VMEM((sub, Z_WIDTH), f32),
            pltpu.VMEM((tile - sub, Z_WIDTH), f32),
            pltpu.VMEM((tile, D_MODEL), bf16),
            pltpu.VMEM((BLOCK + tile + sub, ATT_KV_W), bf16),
            pltpu.VMEM((BLOCK + tile + sub, ATT_KV_W), bf16),
            pltpu.VMEM((POOL_TAIL + tile + sub, POOL_WIDTH), f32),
            pltpu.VMEM((RET_SLABS, LANES, LANES), f32),
        ],
        compiler_params=pltpu.CompilerParams(
            dimension_semantics=("arbitrary", "arbitrary"),
            vmem_limit_bytes=VMEM_LIMIT_BYTES),
        name="mixer",
    )(*operands)
    return outs[0], (tuple(outs[1:]) if mlp_weights is not None else None)


def _mlp_kernel(x_ref, g_ref, wup_ref, wdown_ref, gf_ref, *rest, final_norm, cast_next):
    f32, bf16 = jnp.float32, jnp.bfloat16
    o_ref = rest[len(rest) // 2] if cast_next else rest[0]
    x = x_ref[0]
    h = _rms_norm(x, g_ref[...]).astype(bf16)
    acc = x
    for c in range(D_FF // FF_CHUNK):
        a = jnp.dot(h, wup_ref[:, c * FF_CHUNK:(c + 1) * FF_CHUNK], preferred_element_type=f32)
        a = jnp.square(jnp.maximum(a, 0.0)).astype(bf16)
        acc = acc + jnp.dot(a, wdown_ref[c * FF_CHUNK:(c + 1) * FF_CHUNK, :], preferred_element_type=f32)
    if final_norm:
        acc = _rms_norm(acc, gf_ref[...])
    o_ref[0] = acc
    if cast_next:
        scale_ref, nwin_ref, nwout_ref, nwup_ref, nwdown_ref, _, owin_ref, owout_ref, owup_ref, owdown_ref = rest
        owin_ref[...] = (nwin_ref[...] * scale_ref[...]).astype(bf16)
        owout_ref[...] = nwout_ref[...].astype(bf16)
        owup_ref[...] = nwup_ref[...].astype(bf16)
        owdown_ref[...] = nwdown_ref[...].astype(bf16)


def _mlp_call(layer, x, g, wup, wdown, gf, final_norm, next_weights=None):
    batch, seq, _ = x.shape
    tile = MLP_TILE
    ntiles = seq // tile
    grid = (batch, ntiles)
    x_spec = pl.BlockSpec((1, tile, D_MODEL), lambda b, t: (b, t, 0))
    in_specs = [x_spec, _layer_spec(g.shape, layer), _const_spec(wup.shape), _const_spec(wdown.shape),
                _const_spec(gf.shape)]
    out_specs = [x_spec]
    out_shape = [jax.ShapeDtypeStruct(x.shape, x.dtype)]
    operands = [x, g, wup, wdown, gf]
    if next_weights is not None:
        col_scale, *stacked = next_weights
        steps = batch * ntiles
        in_specs.append(_const_spec(col_scale.shape))
        operands.append(col_scale)
        for w in stacked:
            rows = w.shape[1] // steps
            in_specs.append(pl.BlockSpec((None, rows, w.shape[2]), lambda b, t: (layer + 1, b * ntiles + t, 0)))
            out_specs.append(pl.BlockSpec((rows, w.shape[2]), lambda b, t: (b * ntiles + t, 0)))
            out_shape.append(jax.ShapeDtypeStruct(w.shape[1:], jnp.bfloat16))
            operands.append(w)
    outs = pl.pallas_call(
        functools.partial(_mlp_kernel, final_norm=final_norm, cast_next=next_weights is not None),
        grid=grid,
        in_specs=in_specs,
        out_specs=out_specs,
        out_shape=out_shape,
        compiler_params=pltpu.CompilerParams(
            dimension_semantics=("arbitrary", "arbitrary"),
            vmem_limit_bytes=VMEM_LIMIT_BYTES),
        name="mlp",
    )(*operands)
    return outs[0], (tuple(outs[1:]) if next_weights is not None else None)


def _w_in_col_scale():
    col_scale = np.ones((1, Z_WIDTH), np.float32)
    col_scale[:, C_QA:C_QA + ATT_Q_W] = HEAD_DIM ** -0.5
    return jnp.asarray(col_scale)


def _attention_bias(rel_bias):
    i = np.arange(BLOCK)[:, None]
    j = np.arange(2 * BLOCK)[None, :]
    dist = BLOCK + i - j
    band = (dist >= 0) & (dist < BLOCK)
    onehot = (_t5_bucket(dist).reshape(1, -1) == np.arange(N_BUCKETS)[:, None]).astype(np.float32)
    bias = jnp.dot(rel_bias.astype(jnp.float32).T, onehot, precision=lax.Precision.HIGHEST)
    bias = bias.reshape(ATT_Q_HEADS, BLOCK, 2 * BLOCK)
    normal = jnp.where(jnp.asarray(band)[None], bias, NEG_INF)
    first = jnp.where(jnp.asarray(band & (j >= BLOCK))[None], bias, NEG_INF)
    return jnp.stack([normal, first]).reshape(2, ATT_Q_HEADS * BLOCK, 2 * BLOCK)


def _rotary_tables(seq):
    f32 = np.float32
    inv = (f32(1.0) / np.power(f32(ROPE_BASE), np.linspace(0.0, 1.0, HEAD_DIM // 2, dtype=f32))).astype(f32)
    inv_lane = np.tile(np.repeat(inv, 2), LANES // HEAD_DIM)
    ang = (np.arange(seq, dtype=f32)[:, None] * inv_lane[None, :]).astype(f32)
    cos, sin = np.cos(ang).astype(f32), np.sin(ang).astype(f32)
    even = (np.arange(LANES) % 2 == 0)[None, :]
    return jnp.asarray(cos), jnp.asarray(np.where(even, -sin, sin))


def _retention_tables():
    c = BLOCK
    f32 = np.float32
    scale = f32(HEAD_DIM ** -0.5)
    lg = np.log(f32(1.0) - f32(2.0) ** (f32(-5.0) - np.arange(RET_HEADS, dtype=f32))).astype(f32)
    idx = np.arange(c, dtype=f32)
    diff = idx[:, None] - idx[None, :]
    dmask = np.where(diff >= 0, np.exp(lg[:, None, None] * np.maximum(diff, f32(0.0))), f32(0.0)).astype(f32)
    dm = np.stack([np.concatenate([dmask[2 * j], dmask[2 * j + 1]], axis=1)
                   for j in range(RET_SLABS)]) * scale
    zeta = (np.exp(lg[:, None] * (f32(c) - f32(1.0) - idx)[None, :]) * scale).astype(f32)
    xi = np.exp(lg[None, :] * (idx[:, None] + f32(1.0))).astype(f32)
    g_chunk = np.exp(lg * f32(c)).astype(f32)
    head_of_lane = (np.arange(LANES) >= HALF).astype(np.int32)
    zeta_t = np.stack([zeta[2 * j + head_of_lane].T for j in range(RET_SLABS)])
    xi_t = np.stack([xi[:, 2 * j + head_of_lane] for j in range(RET_SLABS)])
    same = head_of_lane[:, None] == head_of_lane[None, :]
    gtab = np.stack([np.where(same, g_chunk[2 * j + head_of_lane][:, None], f32(0.0))
                     for j in range(RET_SLABS)]).astype(f32)
    return jnp.asarray(dm), jnp.asarray(zeta_t), jnp.asarray(xi_t), jnp.asarray(gtab)


def _pool_block_diag(pool_w):
    depth, groups = pool_w.shape[0], pool_w.shape[1]
    eye = jnp.eye(groups, dtype=jnp.bfloat16)[None, :, None, :, None]
    wide = pool_w.astype(jnp.bfloat16)[:, :, :, None, :] * eye
    return wide.reshape(depth, POOL_WIDTH, POOL_WIDTH)


def kernel(x, attn_norm_g, w_in, attn_sinks, rel_bias, pool_w, pool_scale, w_out, mlp_norm_g, w_up, w_down,
           final_norm_g):
    depth = w_in.shape[0]
    seq = x.shape[1]
    bf16 = jnp.bfloat16
    col_scale = _w_in_col_scale()
    weights = ((w_in[0] * col_scale).astype(bf16), w_out[0].astype(bf16), None, None)
    w_pool = _pool_block_diag(pool_w)
    bias = _attention_bias(rel_bias)
    cos, sin = _rotary_tables(seq)
    dm, zeta_t, xi_t, gtab = _retention_tables()
    g_attn = attn_norm_g.reshape(depth, 1, D_MODEL)
    g_mlp = mlp_norm_g.reshape(depth, 1, D_MODEL)
    p_scale = pool_scale.reshape(depth, 1, POOL_WIDTH)
    gf = final_norm_g.reshape(1, D_MODEL)
    for layer in range(depth):
        last = layer == depth - 1
        w_in_b, w_out_b, w_up_b, w_down_b = weights
        x, converted = _mixer_call(layer, x, attn_sinks, g_attn, w_in_b, bias, cos, sin, dm, zeta_t, xi_t, gtab,
                                   w_pool, p_scale, w_out_b, mlp_weights=(w_up, w_down) if layer == 0 else None)
        if converted is not None:
            w_up_b, w_down_b = converted
        x, weights = _mlp_call(layer, x, g_mlp, w_up_b, w_down_b, gf, final_norm=last,
                               next_weights=None if last else (col_scale, w_in, w_out, w_up, w_down))
    return x
```

```python
import functools

import jax
import jax.numpy as jnp
import numpy as np
from jax import lax
from jax.experimental import pallas as pl
from jax.experimental.pallas import tpu as pltpu

D_MODEL = 1024
HEAD_DIM = 64
ATT_Q_HEADS = 6
ATT_KV_HEADS = 2
ATT_GROUP = ATT_Q_HEADS // ATT_KV_HEADS
RET_HEADS = 6
POOL_WINDOWS = (2, 4, 8, 16)
POOL_GROUP_WIDTH = 64
POOL_WIDTH = 256
BLOCK = 128
N_BUCKETS = 32
MAX_DISTANCE = 128
D_FF = 4 * D_MODEL
RMS_EPS = 1e-6
ROPE_BASE = 10000.0
NEG_INF = -1e30

LANES = 128
HALF = LANES // 2
ATT_Q_W = ATT_Q_HEADS * HEAD_DIM
ATT_KV_W = ATT_KV_HEADS * HEAD_DIM
RET_W = RET_HEADS * HEAD_DIM
RET_SLABS = RET_W // LANES
ATT_SLABS = ATT_Q_W // LANES
POOL_TAIL = 16

C_QA = 0
C_KA = C_QA + ATT_Q_W
C_VA = C_KA + ATT_KV_W
C_QR = C_VA + ATT_KV_W
C_KR = C_QR + RET_W
C_VR = C_KR + RET_W
C_GR = C_VR + RET_W
C_UP = C_GR + RET_W
Z_WIDTH = C_UP + POOL_WIDTH

Y_A = 0
Y_R = ATT_Q_W
Y_P = ATT_Q_W + RET_W

MIXER_TILE = 1024
MIXER_SUB = 512
PROJ_CHUNK = 256
OUT_CHUNK = 256
W_SCORES, W_SOFTMAX, W_VALUES, W_RETENTION = 0.5, 1.0, 1.0, 1.5
W_POOL_NARROW, W_POOL_WIDE, W_POOL_MAPS = 1.0, 3.0, 0.5
MLP_TILE = 1024
FF_CHUNK = 1024
VMEM_LIMIT_BYTES = 56 * 1024 * 1024


def _rms_norm(x, g):
    ms = jnp.mean(x * x, axis=-1, keepdims=True)
    return x * lax.rsqrt(ms + RMS_EPS) * g


def _t5_bucket(dist):
    max_exact = N_BUCKETS // 2
    n = np.maximum(dist, 0)
    large = max_exact + (np.log(np.maximum(n, 1) / max_exact)
                         / np.log(MAX_DISTANCE / max_exact)
                         * (N_BUCKETS - max_exact)).astype(np.int64)
    large = np.minimum(large, N_BUCKETS - 1)
    return np.where(n < max_exact, n, large).astype(np.int32)


def _spread(first, second):
    keyed = []
    for order, tasks in enumerate((first, second)):
        total = sum(weight for weight, _ in tasks)
        done = 0.0
        for weight, task in tasks:
            keyed.append(((done + 0.5 * weight) / total, order, (weight, task)))
            done += weight
    return [item for _, _, item in sorted(keyed, key=lambda entry: entry[:2])]


def _mixer_kernel(sinks_ref, x_ref, xn_ref, g_ref, win_ref, bias_ref, cos_ref, sin_ref, dm_ref,
                  zeta_ref, xi_ref, gtab_ref, wpool_ref, pscale_ref, wout_ref, *rest, tile, sub, layer, cast_mlp):
    if cast_mlp:
        wup_f32_ref, wdown_f32_ref, o_ref, wup_bf16_ref, wdown_bf16_ref, *scratch = rest
    else:
        o_ref, *scratch = rest
    z0_ref, z_ref, y_ref, kbuf, vbuf, ubuf, r_ref = scratch
    t = pl.program_id(1)
    npiece = tile // sub
    assert npiece >= 2, "piece 0 of the next tile is projected while the last piece is mixed"
    f32, bf16 = jnp.float32, jnp.bfloat16

    lane = lax.broadcasted_iota(jnp.int32, (BLOCK, LANES), 1)
    lo = lane < HALF

    def zrows(r0):
        return (z0_ref, r0) if r0 < sub else (z_ref, r0 - sub)

    def project_tasks(read_x, zdst, zrow, brow):
        cache = []

        def normed():
            if not cache:
                cache.append(_rms_norm(read_x(), g_ref[...]).astype(bf16))
            return cache[0]

        side = ((C_KA, ATT_KV_W, kbuf, BLOCK), (C_VA, ATT_KV_W, vbuf, BLOCK), (C_UP, POOL_WIDTH, ubuf, POOL_TAIL))
        chunks = [(c0, min(c0 + PROJ_CHUNK, Z_WIDTH)) for c0 in range(0, Z_WIDTH, PROJ_CHUNK)]

        def make(c0, c1):
            def task():
                z = jnp.dot(normed(), win_ref[:, c0:c1], preferred_element_type=f32)
                zdst[zrow:zrow + sub, c0:c1] = z
                for cs, width, buf, head in side:
                    a, b = max(c0, cs), min(c1, cs + width)
                    if a < b:
                        buf[head + brow:head + brow + sub, a - cs:b - cs] = z[:, a - c0:b - c0].astype(buf.dtype)
            return task

        return [((c1 - c0) / PROJ_CHUNK, make(c0, c1)) for c0, c1 in chunks]

    @pl.when(t == 0)
    def _():
        kbuf[0:BLOCK, :] = jnp.zeros((BLOCK, ATT_KV_W), bf16)
        vbuf[0:BLOCK, :] = jnp.zeros((BLOCK, ATT_KV_W), bf16)
        ubuf[0:POOL_TAIL, :] = jnp.zeros((POOL_TAIL, POOL_WIDTH), f32)
        r_ref[...] = jnp.zeros(r_ref.shape, f32)
        for _, task in project_tasks(lambda: x_ref[0, 0:sub, :], z0_ref, 0, 0):
            task()

    def attention_tasks(r0):
        live = {"ps": [], "denoms": []}

        def swap_halves(v):
            return pltpu.roll(v, HALF, 1)

        def scores():
            zr, zo = zrows(r0)
            qs = [zr[zo:zo + BLOCK, C_QA + j * LANES:C_QA + (j + 1) * LANES] for j in range(ATT_SLABS)]
            swapped = {}
            parts = []
            for hd in range(ATT_Q_HEADS):
                slab, half, group = hd // 2, hd % 2, hd // ATT_GROUP
                q = qs[slab]
                if half != group:
                    q = swapped.setdefault(slab, swap_halves(q))
                parts.append(jnp.where(lo, q, 0.0) if group == 0 else jnp.where(lo, 0.0, q))
            qstack = jnp.concatenate(parts, axis=0).astype(bf16)
            kk = kbuf[r0:r0 + 2 * BLOCK, :]
            s = lax.dot_general(qstack, kk, (((1,), (1,)), ((), ())), preferred_element_type=f32)
            live["s"] = s + (bias_ref[jnp.where(t == 0, 1, 0)] if r0 == 0 else bias_ref[0])

        def softmax(heads):
            for hd in heads:
                sh = live["s"][hd * BLOCK:(hd + 1) * BLOCK]
                sink = sinks_ref[layer, hd]
                m = jnp.maximum(jnp.max(sh, axis=-1, keepdims=True), sink)
                p = jnp.exp(sh - m)
                live["denoms"].append(jnp.sum(p, axis=-1, keepdims=True) + jnp.exp(sink - m))
                live["ps"].append(p.astype(bf16))

        def weighted_values():
            denoms = live["denoms"]
            pstack = jnp.concatenate(live["ps"], axis=0)
            o = jnp.dot(pstack, vbuf[r0:r0 + 2 * BLOCK, :], preferred_element_type=f32)
            halves = []
            for hd in range(ATT_Q_HEADS):
                oh = o[hd * BLOCK:(hd + 1) * BLOCK] / denoms[hd]
                halves.append(oh if hd % 2 == hd // ATT_GROUP else swap_halves(oh))
            for j in range(ATT_SLABS):
                y_ref[r0:r0 + BLOCK, Y_A + j * LANES:Y_A + (j + 1) * LANES] = (
                    jnp.where(lo, halves[2 * j], halves[2 * j + 1]).astype(bf16))

        return ([(W_SCORES, scores)] + [(W_SOFTMAX, functools.partial(softmax, (hd,))) for hd in range(ATT_Q_HEADS)]
                + [(W_VALUES, weighted_values)])

    even = lane % 2 == 0

    def rotary(v, cos, sin):
        partner = jnp.where(even, pltpu.roll(v, LANES - 1, 1), pltpu.roll(v, 1, 1))
        return v * cos + partner * sin

    def retention(r0, j):
        zr, zo = zrows(r0)
        cos = cos_ref[r0:r0 + BLOCK, :]
        sin = sin_ref[r0:r0 + BLOCK, :]
        c0 = j * LANES
        q = rotary(zr[zo:zo + BLOCK, C_QR + c0:C_QR + c0 + LANES], cos, sin)
        k = rotary(zr[zo:zo + BLOCK, C_KR + c0:C_KR + c0 + LANES], cos, sin)
        v = zr[zo:zo + BLOCK, C_VR + c0:C_VR + c0 + LANES]
        gate = zr[zo:zo + BLOCK, C_GR + c0:C_GR + c0 + LANES]
        qb = q.astype(bf16)
        kb = k.astype(bf16)
        ksplit = jnp.concatenate([jnp.where(lo, k, 0.0), jnp.where(lo, 0.0, k)], axis=0).astype(bf16)
        sc = lax.dot_general(qb, ksplit, (((1,), (1,)), ((), ())), preferred_element_type=f32)
        acat = (sc * dm_ref[j]).astype(bf16)
        vsplit = jnp.concatenate([jnp.where(lo, v, 0.0), jnp.where(lo, 0.0, v)], axis=0).astype(bf16)
        inner = jnp.dot(acat, vsplit, preferred_element_type=f32)
        r_prev = r_ref[j]
        cross = jnp.dot(qb, r_prev.astype(bf16), preferred_element_type=f32) * xi_ref[j]
        o = inner + cross
        vz = (v * zeta_ref[j]).astype(bf16)
        u = lax.dot_general(kb, vz, (((0,), (0,)), ((), ())), preferred_element_type=f32)
        gt = gtab_ref[j]
        r_ref[j] = r_prev * gt + jnp.where(gt > 0.0, u, 0.0)
        o2 = o * o
        ms_lo = jnp.sum(jnp.where(lo, o2, 0.0), axis=-1, keepdims=True)
        ms_hi = jnp.sum(jnp.where(lo, 0.0, o2), axis=-1, keepdims=True)
        ms = jnp.where(lo, ms_lo, ms_hi) * (1.0 / HEAD_DIM)
        on = o * lax.rsqrt(ms + RMS_EPS)
        y_ref[r0:r0 + BLOCK, Y_R + c0:Y_R + c0 + LANES] = (jax.nn.silu(gate) * on).astype(bf16)

    def pool_tasks(r0):
        pooled = {}

        def window(sl):
            lo_t = lax.broadcasted_iota(jnp.int32, (sub, LANES), 1) < HALF
            pos1 = lax.broadcasted_iota(jnp.int32, (sub, LANES), 0) + (t * tile + r0 + 1)
            w_small, w_big = POOL_WINDOWS[2 * sl], POOL_WINDOWS[2 * sl + 1]

            def shifted(d):
                return ubuf[POOL_TAIL + r0 - d:POOL_TAIL + r0 - d + sub, sl * LANES:(sl + 1) * LANES]

            u0 = shifted(0)
            acc = u0
            for d in range(1, w_small):
                acc = acc + shifted(d)
            small = acc
            for d in range(w_small, w_big):
                acc = acc + shifted(d)
            win = jnp.where(lo_t, small, acc)
            cnt = jnp.minimum(pos1, jnp.where(lo_t, w_small, w_big)).astype(f32)
            pooled[sl] = (win / cnt - u0).astype(bf16)

        def group_maps():
            both = jnp.concatenate([pooled[0], pooled[1]], axis=1)
            mixed = jnp.dot(both, wpool_ref[...], preferred_element_type=f32) * pscale_ref[...]
            y_ref[r0:r0 + sub, Y_P:Y_P + POOL_WIDTH] = mixed.astype(bf16)

        return [(W_POOL_NARROW, functools.partial(window, 0)), (W_POOL_WIDE, functools.partial(window, 1)),
                (W_POOL_MAPS, group_maps)]

    def output_tasks(r0):
        def make(c0):
            def task():
                o_ref[0, r0:r0 + sub, c0:c0 + OUT_CHUNK] = x_ref[0, r0:r0 + sub, c0:c0 + OUT_CHUNK] + jnp.dot(
                    y_ref[r0:r0 + sub, :], wout_ref[:, c0:c0 + OUT_CHUNK], preferred_element_type=f32)
            return task
        return [(1.0, make(c0)) for c0 in range(0, D_MODEL, OUT_CHUNK)]

    def mix_tasks(r0):
        tasks = []
        for b0 in range(r0, r0 + sub, BLOCK):
            tasks += attention_tasks(b0)
            tasks += [(W_RETENTION, functools.partial(retention, b0, j)) for j in range(RET_SLABS)]
        return tasks + pool_tasks(r0)

    for p in range(npiece):
        r0 = p * sub
        if p + 1 < npiece:
            matmuls = project_tasks(lambda r=r0 + sub: x_ref[0, r:r + sub, :], z_ref, r0, r0 + sub)
        else:
            matmuls = project_tasks(lambda: xn_ref[0], z0_ref, 0, tile)
        if p >= 1:
            matmuls = _spread(matmuls, output_tasks(r0 - sub))
        for _, task in _spread(matmuls, mix_tasks(r0)):
            task()
    for _, task in output_tasks(tile - sub):
        task()

    kbuf[0:BLOCK + sub, :] = kbuf[tile:tile + BLOCK + sub, :]
    vbuf[0:BLOCK + sub, :] = vbuf[tile:tile + BLOCK + sub, :]
    ubuf[0:POOL_TAIL + sub, :] = ubuf[tile:tile + POOL_TAIL + sub, :]

    if cast_mlp:
        wup_bf16_ref[...] = wup_f32_ref[...].astype(bf16)
        wdown_bf16_ref[...] = wdown_f32_ref[...].astype(bf16)


_RESIDENT = pl.Buffered(1)


def _const_spec(shape):
    nd = len(shape)
    return pl.BlockSpec(shape, lambda b, t, _nd=nd: (0,) * _nd, pipeline_mode=_RESIDENT)


def _layer_spec(shape, layer):
    nd = len(shape)
    return pl.BlockSpec((None,) + tuple(shape[1:]), lambda b, t, _nd=nd: (layer,) + (0,) * (_nd - 1),
                        pipeline_mode=_RESIDENT)


def _mixer_call(layer, x, sinks, g, win, bias, cos, sin, dm, zeta, xi, gtab, wpool, pscale, wout, mlp_weights=None):
    batch, seq, _ = x.shape
    tile, sub = MIXER_TILE, MIXER_SUB
    ntiles = seq // tile
    grid = (batch, ntiles)
    last_piece = seq // sub - 1
    f32, bf16 = jnp.float32, jnp.bfloat16
    rot_spec = pl.BlockSpec((tile, LANES), lambda b, t: (t, 0))
    x_spec = pl.BlockSpec((1, tile, D_MODEL), lambda b, t: (b, t, 0))
    in_specs = [
        pl.BlockSpec(memory_space=pltpu.SMEM),
        x_spec,
        pl.BlockSpec((1, sub, D_MODEL), lambda b, t: (b, jnp.minimum((t + 1) * (tile // sub), last_piece), 0)),
        _layer_spec(g.shape, layer), _const_spec(win.shape), _const_spec(bias.shape),
        rot_spec, rot_spec,
        _const_spec(dm.shape), _const_spec(zeta.shape), _const_spec(xi.shape), _const_spec(gtab.shape),
        _layer_spec(wpool.shape, layer), _layer_spec(pscale.shape, layer), _const_spec(wout.shape),
    ]
    operands = [sinks, x, x, g, win, bias, cos, sin, dm, zeta, xi, gtab, wpool, pscale, wout]
    out_specs = [x_spec]
    out_shape = [jax.ShapeDtypeStruct(x.shape, x.dtype)]
    if mlp_weights is not None:
        steps = batch * ntiles
        for w in mlp_weights:
            rows = w.shape[1] // steps
            in_specs.append(pl.BlockSpec((None, rows, w.shape[2]), lambda b, t: (layer, b * ntiles + t, 0)))
            out_specs.append(pl.BlockSpec((rows, w.shape[2]), lambda b, t: (b * ntiles + t, 0)))
            out_shape.append(jax.ShapeDtypeStruct(w.shape[1:], bf16))
            operands.append(w)
    outs = pl.pallas_call(
        functools.partial(_mixer_kernel, tile=tile, sub=sub, layer=layer, cast_mlp=mlp_weights is not None),
        grid=grid,
        in_specs=in_specs,
        out_specs=out_specs,
        out_shape=out_shape,
        scratch_shapes=[
            pltpu.VMEM((sub, Z_WIDTH), f32),
            pltpu.VMEM((tile - sub, Z_WIDTH), f32),
            pltpu.VMEM((tile, D_MODEL), bf16),
            pltpu.VMEM((BLOCK + tile + sub, ATT_KV_W), bf16),
            pltpu.VMEM((BLOCK + tile + sub, ATT_KV_W), bf16),
            pltpu.VMEM((POOL_TAIL + tile + sub, POOL_WIDTH), f32),
            pltpu.VMEM((RET_SLABS, LANES, LANES), f32),
        ],
        compiler_params=pltpu.CompilerParams(
            dimension_semantics=("arbitrary", "arbitrary"),
            vmem_limit_bytes=VMEM_LIMIT_BYTES),
        name="mixer",
    )(*operands)
    return outs[0], (tuple(outs[1:]) if mlp_weights is not None else None)


def _mlp_kernel(x_ref, g_ref, wup_ref, wdown_ref, gf_ref, *rest, final_norm, cast_next):
    f32, bf16 = jnp.float32, jnp.bfloat16
    o_ref = rest[len(rest) // 2] if cast_next else rest[0]
    x = x_ref[0]
    h = _rms_norm(x, g_ref[...]).astype(bf16)
    acc = x
    for c in range(D_FF // FF_CHUNK):
        a = jnp.dot(h, wup_ref[:, c * FF_CHUNK:(c + 1) * FF_CHUNK], preferred_element_type=f32)
        a = jnp.square(jnp.maximum(a, 0.0)).astype(bf16)
        acc = acc + jnp.dot(a, wdown_ref[c * FF_CHUNK:(c + 1) * FF_CHUNK, :], preferred_element_type=f32)
    if final_norm:
        acc = _rms_norm(acc, gf_ref[...])
    o_ref[0] = acc
    if cast_next:
        scale_ref, nwin_ref, nwout_ref, nwup_ref, nwdown_ref, _, owin_ref, owout_ref, owup_ref, owdown_ref = rest
        owin_ref[...] = (nwin_ref[...] * scale_ref[...]).astype(bf16)
        owout_ref[...] = nwout_ref[...].astype(bf16)
        owup_ref[...] = nwup_ref[...].astype(bf16)
        owdown_ref[...] = nwdown_ref[...].astype(bf16)


def _mlp_call(layer, x, g, wup, wdown, gf, final_norm, next_weights=None):
    batch, seq, _ = x.shape
    tile = MLP_TILE
    ntiles = seq // tile
    grid = (batch, ntiles)
    x_spec = pl.BlockSpec((1, tile, D_MODEL), lambda b, t: (b, t, 0))
    in_specs = [x_spec, _layer_spec(g.shape, layer), _const_spec(wup.shape), _const_spec(wdown.shape),
                _const_spec(gf.shape)]
    out_specs = [x_spec]
    out_shape = [jax.ShapeDtypeStruct(x.shape, x.dtype)]
    operands = [x, g, wup, wdown, gf]
    if next_weights is not None:
        col_scale, *stacked = next_weights
        steps = batch * ntiles
        in_specs.append(_const_spec(col_scale.shape))
        operands.append(col_scale)
        for w in stacked:
            rows = w.shape[1] // steps
            in_specs.append(pl.BlockSpec((None, rows, w.shape[2]), lambda b, t: (layer + 1, b * ntiles + t, 0)))
            out_specs.append(pl.BlockSpec((rows, w.shape[2]), lambda b, t: (b * ntiles + t, 0)))
            out_shape.append(jax.ShapeDtypeStruct(w.shape[1:], jnp.bfloat16))
            operands.append(w)
    outs = pl.pallas_call(
        functools.partial(_mlp_kernel, final_norm=final_norm, cast_next=next_weights is not None),
        grid=grid,
        in_specs=in_specs,
        out_specs=out_specs,
        out_shape=out_shape,
        compiler_params=pltpu.CompilerParams(
            dimension_semantics=("arbitrary", "arbitrary"),
            vmem_limit_bytes=VMEM_LIMIT_BYTES),
        name="mlp",
    )(*operands)
    return outs[0], (tuple(outs[1:]) if next_weights is not None else None)


def _w_in_col_scale():
    col_scale = np.ones((1, Z_WIDTH), np.float32)
    col_scale[:, C_QA:C_QA + ATT_Q_W] = HEAD_DIM ** -0.5
    return jnp.asarray(col_scale)


def _attention_bias(rel_bias):
    i = np.arange(BLOCK)[:, None]
    j = np.arange(2 * BLOCK)[None, :]
    dist = BLOCK + i - j
    band = (dist >= 0) & (dist < BLOCK)
    onehot = (_t5_bucket(dist).reshape(1, -1) == np.arange(N_BUCKETS)[:, None]).astype(np.float32)
    bias = jnp.dot(rel_bias.astype(jnp.float32).T, onehot, precision=lax.Precision.HIGHEST)
    bias = bias.reshape(ATT_Q_HEADS, BLOCK, 2 * BLOCK)
    normal = jnp.where(jnp.asarray(band)[None], bias, NEG_INF)
    first = jnp.where(jnp.asarray(band & (j >= BLOCK))[None], bias, NEG_INF)
    return jnp.stack([normal, first]).reshape(2, ATT_Q_HEADS * BLOCK, 2 * BLOCK)


def _rotary_tables(seq):
    f32 = np.float32
    inv = (f32(1.0) / np.power(f32(ROPE_BASE), np.linspace(0.0, 1.0, HEAD_DIM // 2, dtype=f32))).astype(f32)
    inv_lane = np.tile(np.repeat(inv, 2), LANES // HEAD_DIM)
    ang = (np.arange(seq, dtype=f32)[:, None] * inv_lane[None, :]).astype(f32)
    cos, sin = np.cos(ang).astype(f32), np.sin(ang).astype(f32)
    even = (np.arange(LANES) % 2 == 0)[None, :]
    return jnp.asarray(cos), jnp.asarray(np.where(even, -sin, sin))


def _retention_tables():
    c = BLOCK
    f32 = np.float32
    scale = f32(HEAD_DIM ** -0.5)
    lg = np.log(f32(1.0) - f32(2.0) ** (f32(-5.0) - np.arange(RET_HEADS, dtype=f32))).astype(f32)
    idx = np.arange(c, dtype=f32)
    diff = idx[:, None] - idx[None, :]
    dmask = np.where(diff >= 0, np.exp(lg[:, None, None] * np.maximum(diff, f32(0.0))), f32(0.0)).astype(f32)
    dm = np.stack([np.concatenate([dmask[2 * j], dmask[2 * j + 1]], axis=1)
                   for j in range(RET_SLABS)]) * scale
    zeta = (np.exp(lg[:, None] * (f32(c) - f32(1.0) - idx)[None, :]) * scale).astype(f32)
    xi = np.exp(lg[None, :] * (idx[:, None] + f32(1.0))).astype(f32)
    g_chunk = np.exp(lg * f32(c)).astype(f32)
    head_of_lane = (np.arange(LANES) >= HALF).astype(np.int32)
    zeta_t = np.stack([zeta[2 * j + head_of_lane].T for j in range(RET_SLABS)])
    xi_t = np.stack([xi[:, 2 * j + head_of_lane] for j in range(RET_SLABS)])
    same = head_of_lane[:, None] == head_of_lane[None, :]
    gtab = np.stack([np.where(same, g_chunk[2 * j + head_of_lane][:, None], f32(0.0))
                     for j in range(RET_SLABS)]).astype(f32)
    return jnp.asarray(dm), jnp.asarray(zeta_t), jnp.asarray(xi_t), jnp.asarray(gtab)


def _pool_block_diag(pool_w):
    depth, groups = pool_w.shape[0], pool_w.shape[1]
    eye = jnp.eye(groups, dtype=jnp.bfloat16)[None, :, None, :, None]
    wide = pool_w.astype(jnp.bfloat16)[:, :, :, None, :] * eye
    return wide.reshape(depth, POOL_WIDTH, POOL_WIDTH)


def kernel(x, attn_norm_g, w_in, attn_sinks, rel_bias, pool_w, pool_scale, w_out, mlp_norm_g, w_up, w_down,
           final_norm_g):
    depth = w_in.shape[0]
    seq = x.shape[1]
    bf16 = jnp.bfloat16
    col_scale = _w_in_col_scale()
    weights = ((w_in[0] * col_scale).astype(bf16), w_out[0].astype(bf16), w_up[0].astype(bf16),
               w_down[0].astype(bf16))
    w_pool = _pool_block_diag(pool_w)
    bias = _attention_bias(rel_bias)
    cos, sin = _rotary_tables(seq)
    dm, zeta_t, xi_t, gtab = _retention_tables()
    g_attn = attn_norm_g.reshape(depth, 1, D_MODEL)
    g_mlp = mlp_norm_g.reshape(depth, 1, D_MODEL)
    p_scale = pool_scale.reshape(depth, 1, POOL_WIDTH)
    gf = final_norm_g.reshape(1, D_MODEL)
    for layer in range(depth):
        last = layer == depth - 1
        w_in_b, w_out_b, w_up_b, w_down_b = weights
        x, converted = _mixer_call(layer, x, attn_sinks, g_attn, w_in_b, bias, cos, sin, dm, zeta_t, xi_t, gtab,
                                   w_pool, p_scale, w_out_b, mlp_weights=None)
        if converted is not None:
            w_up_b, w_down_b = converted
        x, weights = _mlp_call(layer, x, g_mlp, w_up_b, w_down_b, gf, final_norm=last,
                               next_weights=None if last else (col_scale, w_in, w_out, w_up, w_down))
    return x
```

```python
import functools

import jax
import jax.numpy as jnp
import numpy as np
from jax import lax
from jax.experimental import pallas as pl
from jax.experimental.pallas import tpu as pltpu

D_MODEL = 1024
HEAD_DIM = 64
ATT_Q_HEADS = 6
ATT_KV_HEADS = 2
ATT_GROUP = ATT_Q_HEADS // ATT_KV_HEADS
RET_HEADS = 6
POOL_WINDOWS = (2, 4, 8, 16)
POOL_GROUP_WIDTH = 64
POOL_WIDTH = 256
BLOCK = 128
N_BUCKETS = 32
MAX_DISTANCE = 128
D_FF = 4 * D_MODEL
RMS_EPS = 1e-6
ROPE_BASE = 10000.0
NEG_INF = -1e30

LANES = 128
HALF = LANES // 2
ATT_Q_W = ATT_Q_HEADS * HEAD_DIM
ATT_KV_W = ATT_KV_HEADS * HEAD_DIM
RET_W = RET_HEADS * HEAD_DIM
RET_SLABS = RET_W // LANES
ATT_SLABS = ATT_Q_W // LANES
POOL_TAIL = 16

C_QA = 0
C_KA = C_QA + ATT_Q_W
C_VA = C_KA + ATT_KV_W
C_QR = C_VA + ATT_KV_W
C_KR = C_QR + RET_W
C_VR = C_KR + RET_W
C_GR = C_VR + RET_W
C_UP = C_GR + RET_W
Z_WIDTH = C_UP + POOL_WIDTH

Y_A = 0
Y_R = ATT_Q_W
Y_P = ATT_Q_W + RET_W

MIXER_TILE = 1024
MIXER_SUB = 128
PROJ_CHUNK = 256
OUT_CHUNK = 256
W_SCORES, W_SOFTMAX, W_VALUES, W_RETENTION = 0.5, 1.0, 1.0, 1.5
W_POOL_NARROW, W_POOL_WIDE, W_POOL_MAPS = 1.0, 3.0, 0.5
MLP_TILE = 1024
FF_CHUNK = 1024
VMEM_LIMIT_BYTES = 56 * 1024 * 1024


def _rms_norm(x, g):
    ms = jnp.mean(x * x, axis=-1, keepdims=True)
    return x * lax.rsqrt(ms + RMS_EPS) * g


def _t5_bucket(dist):
    max_exact = N_BUCKETS // 2
    n = np.maximum(dist, 0)
    large = max_exact + (np.log(np.maximum(n, 1) / max_exact)
                         / np.log(MAX_DISTANCE / max_exact)
                         * (N_BUCKETS - max_exact)).astype(np.int64)
    large = np.minimum(large, N_BUCKETS - 1)
    return np.where(n < max_exact, n, large).astype(np.int32)


def _spread(first, second):
    keyed = []
    for order, tasks in enumerate((first, second)):
        total = sum(weight for weight, _ in tasks)
        done = 0.0
        for weight, task in tasks:
            keyed.append(((done + 0.5 * weight) / total, order, (weight, task)))
            done += weight
    return [item for _, _, item in sorted(keyed, key=lambda entry: entry[:2])]


def _mixer_kernel(sinks_ref, x_ref, xn_ref, g_ref, win_ref, bias_ref, cos_ref, sin_ref, dm_ref,
                  zeta_ref, xi_ref, gtab_ref, wpool_ref, pscale_ref, wout_ref, *rest, tile, sub, layer, cast_mlp):
    if cast_mlp:
        wup_f32_ref, wdown_f32_ref, o_ref, wup_bf16_ref, wdown_bf16_ref, *scratch = rest
    else:
        o_ref, *scratch = rest
    z0_ref, z_ref, y_ref, kbuf, vbuf, ubuf, r_ref = scratch
    t = pl.program_id(1)
    npiece = tile // sub
    assert npiece >= 2, "piece 0 of the next tile is projected while the last piece is mixed"
    f32, bf16 = jnp.float32, jnp.bfloat16

    lane = lax.broadcasted_iota(jnp.int32, (BLOCK, LANES), 1)
    lo = lane < HALF

    def zrows(r0):
        return (z0_ref, r0) if r0 < sub else (z_ref, r0 - sub)

    def project_tasks(read_x, zdst, zrow, brow):
        cache = []

        def normed():
            if not cache:
                cache.append(_rms_norm(read_x(), g_ref[...]).astype(bf16))
            return cache[0]

        side = ((C_KA, ATT_KV_W, kbuf, BLOCK), (C_VA, ATT_KV_W, vbuf, BLOCK), (C_UP, POOL_WIDTH, ubuf, POOL_TAIL))
        chunks = [(c0, min(c0 + PROJ_CHUNK, Z_WIDTH)) for c0 in range(0, Z_WIDTH, PROJ_CHUNK)]

        def make(c0, c1):
            def task():
                z = jnp.dot(normed(), win_ref[:, c0:c1], preferred_element_type=f32)
                zdst[zrow:zrow + sub, c0:c1] = z
                for cs, width, buf, head in side:
                    a, b = max(c0, cs), min(c1, cs + width)
                    if a < b:
                        buf[head + brow:head + brow + sub, a - cs:b - cs] = z[:, a - c0:b - c0].astype(buf.dtype)
            return task

        return [((c1 - c0) / PROJ_CHUNK, make(c0, c1)) for c0, c1 in chunks]

    @pl.when(t == 0)
    def _():
        kbuf[0:BLOCK, :] = jnp.zeros((BLOCK, ATT_KV_W), bf16)
        vbuf[0:BLOCK, :] = jnp.zeros((BLOCK, ATT_KV_W), bf16)
        ubuf[0:POOL_TAIL, :] = jnp.zeros((POOL_TAIL, POOL_WIDTH), f32)
        r_ref[...] = jnp.zeros(r_ref.shape, f32)
        for _, task in project_tasks(lambda: x_ref[0, 0:sub, :], z0_ref, 0, 0):
            task()

    def attention_tasks(r0):
        live = {"ps": [], "denoms": []}

        def swap_halves(v):
            return pltpu.roll(v, HALF, 1)

        def scores():
            zr, zo = zrows(r0)
            qs = [zr[zo:zo + BLOCK, C_QA + j * LANES:C_QA + (j + 1) * LANES] for j in range(ATT_SLABS)]
            swapped = {}
            parts = []
            for hd in range(ATT_Q_HEADS):
                slab, half, group = hd // 2, hd % 2, hd // ATT_GROUP
                q = qs[slab]
                if half != group:
                    q = swapped.setdefault(slab, swap_halves(q))
                parts.append(jnp.where(lo, q, 0.0) if group == 0 else jnp.where(lo, 0.0, q))
            qstack = jnp.concatenate(parts, axis=0).astype(bf16)
            kk = kbuf[r0:r0 + 2 * BLOCK, :]
            s = lax.dot_general(qstack, kk, (((1,), (1,)), ((), ())), preferred_element_type=f32)
            live["s"] = s + (bias_ref[jnp.where(t == 0, 1, 0)] if r0 == 0 else bias_ref[0])

        def softmax(heads):
            for hd in heads:
                sh = live["s"][hd * BLOCK:(hd + 1) * BLOCK]
                sink = sinks_ref[layer, hd]
                m = jnp.maximum(jnp.max(sh, axis=-1, keepdims=True), sink)
                p = jnp.exp(sh - m)
                live["denoms"].append(jnp.sum(p, axis=-1, keepdims=True) + jnp.exp(sink - m))
                live["ps"].append(p.astype(bf16))

        def weighted_values():
            denoms = live["denoms"]
            pstack = jnp.concatenate(live["ps"], axis=0)
            o = jnp.dot(pstack, vbuf[r0:r0 + 2 * BLOCK, :], preferred_element_type=f32)
            halves = []
            for hd in range(ATT_Q_HEADS):
                oh = o[hd * BLOCK:(hd + 1) * BLOCK] / denoms[hd]
                halves.append(oh if hd % 2 == hd // ATT_GROUP else swap_halves(oh))
            for j in range(ATT_SLABS):
                y_ref[r0:r0 + BLOCK, Y_A + j * LANES:Y_A + (j + 1) * LANES] = (
                    jnp.where(lo, halves[2 * j], halves[2 * j + 1]).astype(bf16))

        return ([(W_SCORES, scores)] + [(W_SOFTMAX, functools.partial(softmax, (hd,))) for hd in range(ATT_Q_HEADS)]
                + [(W_VALUES, weighted_values)])

    even = lane % 2 == 0

    def rotary(v, cos, sin):
        partner = jnp.where(even, pltpu.roll(v, LANES - 1, 1), pltpu.roll(v, 1, 1))
        return v * cos + partner * sin

    def retention(r0, j):
        zr, zo = zrows(r0)
        cos = cos_ref[r0:r0 + BLOCK, :]
        sin = sin_ref[r0:r0 + BLOCK, :]
        c0 = j * LANES
        q = rotary(zr[zo:zo + BLOCK, C_QR + c0:C_QR + c0 + LANES], cos, sin)
        k = rotary(zr[zo:zo + BLOCK, C_KR + c0:C_KR + c0 + LANES], cos, sin)
        v = zr[zo:zo + BLOCK, C_VR + c0:C_VR + c0 + LANES]
        gate = zr[zo:zo + BLOCK, C_GR + c0:C_GR + c0 + LANES]
        qb = q.astype(bf16)
        kb = k.astype(bf16)
        ksplit = jnp.concatenate([jnp.where(lo, k, 0.0), jnp.where(lo, 0.0, k)], axis=0).astype(bf16)
        sc = lax.dot_general(qb, ksplit, (((1,), (1,)), ((), ())), preferred_element_type=f32)
        acat = (sc * dm_ref[j]).astype(bf16)
        vsplit = jnp.concatenate([jnp.where(lo, v, 0.0), jnp.where(lo, 0.0, v)], axis=0).astype(bf16)
        inner = jnp.dot(acat, vsplit, preferred_element_type=f32)
        r_prev = r_ref[j]
        cross = jnp.dot(qb, r_prev.astype(bf16), preferred_element_type=f32) * xi_ref[j]
        o = inner + cross
        vz = (v * zeta_ref[j]).astype(bf16)
        u = lax.dot_general(kb, vz, (((0,), (0,)), ((), ())), preferred_element_type=f32)
        gt = gtab_ref[j]
        r_ref[j] = r_prev * gt + jnp.where(gt > 0.0, u, 0.0)
        o2 = o * o
        ms_lo = jnp.sum(jnp.where(lo, o2, 0.0), axis=-1, keepdims=True)
        ms_hi = jnp.sum(jnp.where(lo, 0.0, o2), axis=-1, keepdims=True)
        ms = jnp.where(lo, ms_lo, ms_hi) * (1.0 / HEAD_DIM)
        on = o * lax.rsqrt(ms + RMS_EPS)
        y_ref[r0:r0 + BLOCK, Y_R + c0:Y_R + c0 + LANES] = (jax.nn.silu(gate) * on).astype(bf16)

    def pool_tasks(r0):
        pooled = {}

        def window(sl):
            lo_t = lax.broadcasted_iota(jnp.int32, (sub, LANES), 1) < HALF
            pos1 = lax.broadcasted_iota(jnp.int32, (sub, LANES), 0) + (t * tile + r0 + 1)
            w_small, w_big = POOL_WINDOWS[2 * sl], POOL_WINDOWS[2 * sl + 1]

            def shifted(d):
                return ubuf[POOL_TAIL + r0 - d:POOL_TAIL + r0 - d + sub, sl * LANES:(sl + 1) * LANES]

            u0 = shifted(0)
            acc = u0
            for d in range(1, w_small):
                acc = acc + shifted(d)
            small = acc
            for d in range(w_small, w_big):
                acc = acc + shifted(d)
            win = jnp.where(lo_t, small, acc)
            cnt = jnp.minimum(pos1, jnp.where(lo_t, w_small, w_big)).astype(f32)
            pooled[sl] = (win / cnt - u0).astype(bf16)

        def group_maps():
            both = jnp.concatenate([pooled[0], pooled[1]], axis=1)
            mixed = jnp.dot(both, wpool_ref[...], preferred_element_type=f32) * pscale_ref[...]
            y_ref[r0:r0 + sub, Y_P:Y_P + POOL_WIDTH] = mixed.astype(bf16)

        return [(W_POOL_NARROW, functools.partial(window, 0)), (W_POOL_WIDE, functools.partial(window, 1)),
                (W_POOL_MAPS, group_maps)]

    def output_tasks(r0):
        def make(c0):
            def task():
                o_ref[0, r0:r0 + sub, c0:c0 + OUT_CHUNK] = x_ref[0, r0:r0 + sub, c0:c0 + OUT_CHUNK] + jnp.dot(
                    y_ref[r0:r0 + sub, :], wout_ref[:, c0:c0 + OUT_CHUNK], preferred_element_type=f32)
            return task
        return [(1.0, make(c0)) for c0 in range(0, D_MODEL, OUT_CHUNK)]

    def mix_tasks(r0):
        tasks = []
        for b0 in range(r0, r0 + sub, BLOCK):
            tasks += attention_tasks(b0)
            tasks += [(W_RETENTION, functools.partial(retention, b0, j)) for j in range(RET_SLABS)]
        return tasks + pool_tasks(r0)

    for p in range(npiece):
        r0 = p * sub
        if p + 1 < npiece:
            matmuls = project_tasks(lambda r=r0 + sub: x_ref[0, r:r + sub, :], z_ref, r0, r0 + sub)
        else:
            matmuls = project_tasks(lambda: xn_ref[0], z0_ref, 0, tile)
        if p >= 1:
            matmuls = _spread(matmuls, output_tasks(r0 - sub))
        for _, task in _spread(matmuls, mix_tasks(r0)):
            task()
    for _, task in output_tasks(tile - sub):
        task()

    kbuf[0:BLOCK + sub, :] = kbuf[tile:tile + BLOCK + sub, :]
    vbuf[0:BLOCK + sub, :] = vbuf[tile:tile + BLOCK + sub, :]
    ubuf[0:POOL_TAIL + sub, :] = ubuf[tile:tile + POOL_TAIL + sub, :]

    if cast_mlp:
        wup_bf16_ref[...] = wup_f32_ref[...].astype(bf16)
        wdown_bf16_ref[...] = wdown_f32_ref[...].astype(bf16)


_RESIDENT = pl.Buffered(1)


def _const_spec(shape):
    nd = len(shape)
    return pl.BlockSpec(shape, lambda b, t, _nd=nd: (0,) * _nd, pipeline_mode=_RESIDENT)


def _layer_spec(shape, layer):
    nd = len(shape)
    return pl.BlockSpec((None,) + tuple(shape[1:]), lambda b, t, _nd=nd: (layer,) + (0,) * (_nd - 1),
                        pipeline_mode=_RESIDENT)


def _mixer_call(layer, x, sinks, g, win, bias, cos, sin, dm, zeta, xi, gtab, wpool, pscale, wout, mlp_weights=None):
    batch, seq, _ = x.shape
    tile, sub = MIXER_TILE, MIXER_SUB
    ntiles = seq // tile
    grid = (batch, ntiles)
    last_piece = seq // sub - 1
    f32, bf16 = jnp.float32, jnp.bfloat16
    rot_spec = pl.BlockSpec((tile, LANES), lambda b, t: (t, 0))
    x_spec = pl.BlockSpec((1, tile, D_MODEL), lambda b, t: (b, t, 0))
    in_specs = [
        pl.BlockSpec(memory_space=pltpu.SMEM),
        x_spec,
        pl.BlockSpec((1, sub, D_MODEL), lambda b, t: (b, jnp.minimum((t + 1) * (tile // sub), last_piece), 0)),
        _layer_spec(g.shape, layer), _const_spec(win.shape), _const_spec(bias.shape),
        rot_spec, rot_spec,
        _const_spec(dm.shape), _const_spec(zeta.shape), _const_spec(xi.shape), _const_spec(gtab.shape),
        _layer_spec(wpool.shape, layer), _layer_spec(pscale.shape, layer), _const_spec(wout.shape),
    ]
    operands = [sinks, x, x, g, win, bias, cos, sin, dm, zeta, xi, gtab, wpool, pscale, wout]
    out_specs = [x_spec]
    out_shape = [jax.ShapeDtypeStruct(x.shape, x.dtype)]
    if mlp_weights is not None:
        steps = batch * ntiles
        for w in mlp_weights:
            rows = w.shape[1] // steps
            in_specs.append(pl.BlockSpec((None, rows, w.shape[2]), lambda b, t: (layer, b * ntiles + t, 0)))
            out_specs.append(pl.BlockSpec((rows, w.shape[2]), lambda b, t: (b * ntiles + t, 0)))
            out_shape.append(jax.ShapeDtypeStruct(w.shape[1:], bf16))
            operands.append(w)
    outs = pl.pallas_call(
        functools.partial(_mixer_kernel, tile=tile, sub=sub, layer=layer, cast_mlp=mlp_weights is not None),
        grid=grid,
        in_specs=in_specs,
        out_specs=out_specs,
        out_shape=out_shape,
        scratch_shapes=[
            pltpu.VMEM((sub, Z_WIDTH), f32),
            pltpu.VMEM((tile - sub, Z_WIDTH), f32),
            pltpu.VMEM((tile, D_MODEL), bf16),
            pltpu.VMEM((BLOCK + tile + sub, ATT_KV_W), bf16),
            pltpu.VMEM((BLOCK + tile + sub, ATT_KV_W), bf16),
            pltpu.VMEM((POOL_TAIL + tile + sub, POOL_WIDTH), f32),
            pltpu.VMEM((RET_SLABS, LANES, LANES), f32),
        ],
        compiler_params=pltpu.CompilerParams(
            dimension_semantics=("arbitrary", "arbitrary"),
            vmem_limit_bytes=VMEM_LIMIT_BYTES),
        name="mixer",
    )(*operands)
    return outs[0], (tuple(outs[1:]) if mlp_weights is not None else None)


def _mlp_kernel(x_ref, g_ref, wup_ref, wdown_ref, gf_ref, *rest, final_norm, cast_next):
    f32, bf16 = jnp.float32, jnp.bfloat16
    o_ref = rest[len(rest) // 2] if cast_next else rest[0]
    x = x_ref[0]
    h = _rms_norm(x, g_ref[...]).astype(bf16)
    acc = x
    for c in range(D_FF // FF_CHUNK):
        a = jnp.dot(h, wup_ref[:, c * FF_CHUNK:(c + 1) * FF_CHUNK], preferred_element_type=f32)
        a = jnp.square(jnp.maximum(a, 0.0)).astype(bf16)
        acc = acc + jnp.dot(a, wdown_ref[c * FF_CHUNK:(c + 1) * FF_CHUNK, :], preferred_element_type=f32)
    if final_norm:
        acc = _rms_norm(acc, gf_ref[...])
    o_ref[0] = acc
    if cast_next:
        scale_ref, nwin_ref, nwout_ref, nwup_ref, nwdown_ref, _, owin_ref, owout_ref, owup_ref, owdown_ref = rest
        owin_ref[...] = (nwin_ref[...] * scale_ref[...]).astype(bf16)
        owout_ref[...] = nwout_ref[...].astype(bf16)
        owup_ref[...] = nwup_ref[...].astype(bf16)
        owdown_ref[...] = nwdown_ref[...].astype(bf16)


def _mlp_call(layer, x, g, wup, wdown, gf, final_norm, next_weights=None):
    batch, seq, _ = x.shape
    tile = MLP_TILE
    ntiles = seq // tile
    grid = (batch, ntiles)
    x_spec = pl.BlockSpec((1, tile, D_MODEL), lambda b, t: (b, t, 0))
    in_specs = [x_spec, _layer_spec(g.shape, layer), _const_spec(wup.shape), _const_spec(wdown.shape),
                _const_spec(gf.shape)]
    out_specs = [x_spec]
    out_shape = [jax.ShapeDtypeStruct(x.shape, x.dtype)]
    operands = [x, g, wup, wdown, gf]
    if next_weights is not None:
        col_scale, *stacked = next_weights
        steps = batch * ntiles
        in_specs.append(_const_spec(col_scale.shape))
        operands.append(col_scale)
        for w in stacked:
            rows = w.shape[1] // steps
            in_specs.append(pl.BlockSpec((None, rows, w.shape[2]), lambda b, t: (layer + 1, b * ntiles + t, 0)))
            out_specs.append(pl.BlockSpec((rows, w.shape[2]), lambda b, t: (b * ntiles + t, 0)))
            out_shape.append(jax.ShapeDtypeStruct(w.shape[1:], jnp.bfloat16))
            operands.append(w)
    outs = pl.pallas_call(
        functools.partial(_mlp_kernel, final_norm=final_norm, cast_next=next_weights is not None),
        grid=grid,
        in_specs=in_specs,
        out_specs=out_specs,
        out_shape=out_shape,
        compiler_params=pltpu.CompilerParams(
            dimension_semantics=("arbitrary", "arbitrary"),
            vmem_limit_bytes=VMEM_LIMIT_BYTES),
        name="mlp",
    )(*operands)
    return outs[0], (tuple(outs[1:]) if next_weights is not None else None)


def _w_in_col_scale():
    col_scale = np.ones((1, Z_WIDTH), np.float32)
    col_scale[:, C_QA:C_QA + ATT_Q_W] = HEAD_DIM ** -0.5
    return jnp.asarray(col_scale)


def _attention_bias(rel_bias):
    i = np.arange(BLOCK)[:, None]
    j = np.arange(2 * BLOCK)[None, :]
    dist = BLOCK + i - j
    band = (dist >= 0) & (dist < BLOCK)
    onehot = (_t5_bucket(dist).reshape(1, -1) == np.arange(N_BUCKETS)[:, None]).astype(np.float32)
    bias = jnp.dot(rel_bias.astype(jnp.float32).T, onehot, precision=lax.Precision.HIGHEST)
    bias = bias.reshape(ATT_Q_HEADS, BLOCK, 2 * BLOCK)
    normal = jnp.where(jnp.asarray(band)[None], bias, NEG_INF)
    first = jnp.where(jnp.asarray(band & (j >= BLOCK))[None], bias, NEG_INF)
    return jnp.stack([normal, first]).reshape(2, ATT_Q_HEADS * BLOCK, 2 * BLOCK)


def _rotary_tables(seq):
    f32 = np.float32
    inv = (f32(1.0) / np.power(f32(ROPE_BASE), np.linspace(0.0, 1.0, HEAD_DIM // 2, dtype=f32))).astype(f32)
    inv_lane = np.tile(np.repeat(inv, 2), LANES // HEAD_DIM)
    ang = (np.arange(seq, dtype=f32)[:, None] * inv_lane[None, :]).astype(f32)
    cos, sin = np.cos(ang).astype(f32), np.sin(ang).astype(f32)
    even = (np.arange(LANES) % 2 == 0)[None, :]
    return jnp.asarray(cos), jnp.asarray(np.where(even, -sin, sin))


def _retention_tables():
    c = BLOCK
    f32 = np.float32
    scale = f32(HEAD_DIM ** -0.5)
    lg = np.log(f32(1.0) - f32(2.0) ** (f32(-5.0) - np.arange(RET_HEADS, dtype=f32))).astype(f32)
    idx = np.arange(c, dtype=f32)
    diff = idx[:, None] - idx[None, :]
    dmask = np.where(diff >= 0, np.exp(lg[:, None, None] * np.maximum(diff, f32(0.0))), f32(0.0)).astype(f32)
    dm = np.stack([np.concatenate([dmask[2 * j], dmask[2 * j + 1]], axis=1)
                   for j in range(RET_SLABS)]) * scale
    zeta = (np.exp(lg[:, None] * (f32(c) - f32(1.0) - idx)[None, :]) * scale).astype(f32)
    xi = np.exp(lg[None, :] * (idx[:, None] + f32(1.0))).astype(f32)
    g_chunk = np.exp(lg * f32(c)).astype(f32)
    head_of_lane = (np.arange(LANES) >= HALF).astype(np.int32)
    zeta_t = np.stack([zeta[2 * j + head_of_lane].T for j in range(RET_SLABS)])
    xi_t = np.stack([xi[:, 2 * j + head_of_lane] for j in range(RET_SLABS)])
    same = head_of_lane[:, None] == head_of_lane[None, :]
    gtab = np.stack([np.where(same, g_chunk[2 * j + head_of_lane][:, None], f32(0.0))
                     for j in range(RET_SLABS)]).astype(f32)
    return jnp.asarray(dm), jnp.asarray(zeta_t), jnp.asarray(xi_t), jnp.asarray(gtab)


def _pool_block_diag(pool_w):
    depth, groups = pool_w.shape[0], pool_w.shape[1]
    eye = jnp.eye(groups, dtype=jnp.bfloat16)[None, :, None, :, None]
    wide = pool_w.astype(jnp.bfloat16)[:, :, :, None, :] * eye
    return wide.reshape(depth, POOL_WIDTH, POOL_WIDTH)


def kernel(x, attn_norm_g, w_in, attn_sinks, rel_bias, pool_w, pool_scale, w_out, mlp_norm_g, w_up, w_down,
           final_norm_g):
    depth = w_in.shape[0]
    seq = x.shape[1]
    bf16 = jnp.bfloat16
    col_scale = _w_in_col_scale()
    weights = ((w_in[0] * col_scale).astype(bf16), w_out[0].astype(bf16), None, None)
    w_pool = _pool_block_diag(pool_w)
    bias = _attention_bias(rel_bias)
    cos, sin = _rotary_tables(seq)
    dm, zeta_t, xi_t, gtab = _retention_tables()
    g_attn = attn_norm_g.reshape(depth, 1, D_MODEL)
    g_mlp = mlp_norm_g.reshape(depth, 1, D_MODEL)
    p_scale = pool_scale.reshape(depth, 1, POOL_WIDTH)
    gf = final_norm_g.reshape(1, D_MODEL)
    for layer in range(depth):
        last = layer == depth - 1
        w_in_b, w_out_b, w_up_b, w_down_b = weights
        x, converted = _mixer_call(layer, x, attn_sinks, g_attn, w_in_b, bias, cos, sin, dm, zeta_t, xi_t, gtab,
                                   w_pool, p_scale, w_out_b, mlp_weights=(w_up, w_down) if layer == 0 else None)
        if converted is not None:
            w_up_b, w_down_b = converted
        x, weights = _mlp_call(layer, x, g_mlp, w_up_b, w_down_b, gf, final_norm=last,
                               next_weights=None if last else (col_scale, w_in, w_out, w_up, w_down))
    return x
```

```python
import functools

import jax
import jax.numpy as jnp
import numpy as np
from jax import lax
from jax.experimental import pallas as pl
from jax.experimental.pallas import tpu as pltpu

D_MODEL = 1024
HEAD_DIM = 64
ATT_Q_HEADS = 6
ATT_KV_HEADS = 2
ATT_GROUP = ATT_Q_HEADS // ATT_KV_HEADS
RET_HEADS = 6
POOL_WINDOWS = (2, 4, 8, 16)
POOL_GROUP_WIDTH = 64
POOL_WIDTH = 256
BLOCK = 128
N_BUCKETS = 32
MAX_DISTANCE = 128
D_FF = 4 * D_MODEL
RMS_EPS = 1e-6
ROPE_BASE = 10000.0
NEG_INF = -1e30

LANES = 128
HALF = LANES // 2
ATT_Q_W = ATT_Q_HEADS * HEAD_DIM
ATT_KV_W = ATT_KV_HEADS * HEAD_DIM
RET_W = RET_HEADS * HEAD_DIM
RET_SLABS = RET_W // LANES
ATT_SLABS = ATT_Q_W // LANES
POOL_TAIL = 16

C_QA = 0
C_KA = C_QA + ATT_Q_W
C_VA = C_KA + ATT_KV_W
C_QR = C_VA + ATT_KV_W
C_KR = C_QR + RET_W
C_VR = C_KR + RET_W
C_GR = C_VR + RET_W
C_UP = C_GR + RET_W
Z_WIDTH = C_UP + POOL_WIDTH

Y_A = 0
Y_R = ATT_Q_W
Y_P = ATT_Q_W + RET_W

MIXER_TILE = 1024
MIXER_SUB = 128
PROJ_CHUNK = 256
OUT_CHUNK = 256
W_SCORES, W_SOFTMAX, W_VALUES, W_RETENTION = 0.5, 1.0, 1.0, 1.5
W_POOL_NARROW, W_POOL_WIDE, W_POOL_MAPS = 1.0, 3.0, 0.5
MLP_TILE = 1024
FF_CHUNK = 1024
VMEM_LIMIT_BYTES = 56 * 1024 * 1024


def _rms_norm(x, g):
    ms = jnp.mean(x * x, axis=-1, keepdims=True)
    return x * lax.rsqrt(ms + RMS_EPS) * g


def _t5_bucket(dist):
    max_exact = N_BUCKETS // 2
    n = np.maximum(dist, 0)
    large = max_exact + (np.log(np.maximum(n, 1) / max_exact)
                         / np.log(MAX_DISTANCE / max_exact)
                         * (N_BUCKETS - max_exact)).astype(np.int64)
    large = np.minimum(large, N_BUCKETS - 1)
    return np.where(n < max_exact, n, large).astype(np.int32)


def _spread(first, second):
    keyed = []
    for order, tasks in enumerate((first, second)):
        total = sum(weight for weight, _ in tasks)
        done = 0.0
        for weight, task in tasks:
            keyed.append(((done + 0.5 * weight) / total, order, (weight, task)))
            done += weight
    return [item for _, _, item in sorted(keyed, key=lambda entry: entry[:2])]


def _mixer_kernel(sinks_ref, x_ref, xn_ref, g_ref, win_ref, bias_ref, cos_ref, sin_ref, dm_ref,
                  zeta_ref, xi_ref, gtab_ref, wpool_ref, pscale_ref, wout_ref, *rest, tile, sub, layer, cast_mlp):
    if cast_mlp:
        wup_f32_ref, wdown_f32_ref, o_ref, wup_bf16_ref, wdown_bf16_ref, *scratch = rest
    else:
        o_ref, *scratch = rest
    z0_ref, z_ref, y_ref, kbuf, vbuf, ubuf, r_ref = scratch
    t = pl.program_id(1)
    npiece = tile // sub
    assert npiece >= 2, "piece 0 of the next tile is projected while the last piece is mixed"
    f32, bf16 = jnp.float32, jnp.bfloat16

    lane = lax.broadcasted_iota(jnp.int32, (BLOCK, LANES), 1)
    lo = lane < HALF

    def zrows(r0):
        return (z0_ref, r0) if r0 < sub else (z_ref, r0 - sub)

    def project_tasks(read_x, zdst, zrow, brow):
        cache = []

        def normed():
            if not cache:
                cache.append(_rms_norm(read_x(), g_ref[...]).astype(bf16))
            return cache[0]

        side = ((C_KA, ATT_KV_W, kbuf, BLOCK), (C_VA, ATT_KV_W, vbuf, BLOCK), (C_UP, POOL_WIDTH, ubuf, POOL_TAIL))
        chunks = [(c0, min(c0 + PROJ_CHUNK, Z_WIDTH)) for c0 in range(0, Z_WIDTH, PROJ_CHUNK)]

        def make(c0, c1):
            def task():
                z = jnp.dot(normed(), win_ref[:, c0:c1], preferred_element_type=f32)
                zdst[zrow:zrow + sub, c0:c1] = z
                for cs, width, buf, head in side:
                    a, b = max(c0, cs), min(c1, cs + width)
                    if a < b:
                        buf[head + brow:head + brow + sub, a - cs:b - cs] = z[:, a - c0:b - c0].astype(buf.dtype)
            return task

        return [((c1 - c0) / PROJ_CHUNK, make(c0, c1)) for c0, c1 in chunks]

    @pl.when(t == 0)
    def _():
        kbuf[0:BLOCK, :] = jnp.zeros((BLOCK, ATT_KV_W), bf16)
        vbuf[0:BLOCK, :] = jnp.zeros((BLOCK, ATT_KV_W), bf16)
        ubuf[0:POOL_TAIL, :] = jnp.zeros((POOL_TAIL, POOL_WIDTH), f32)
        r_ref[...] = jnp.zeros(r_ref.shape, f32)
        for _, task in project_tasks(lambda: x_ref[0, 0:sub, :], z0_ref, 0, 0):
            task()

    def attention_tasks(r0):
        live = {"ps": [], "denoms": []}

        def swap_halves(v):
            return pltpu.roll(v, HALF, 1)

        def scores():
            zr, zo = zrows(r0)
            qs = [zr[zo:zo + BLOCK, C_QA + j * LANES:C_QA + (j + 1) * LANES] for j in range(ATT_SLABS)]
            swapped = {}
            parts = []
            for hd in range(ATT_Q_HEADS):
                slab, half, group = hd // 2, hd % 2, hd // ATT_GROUP
                q = qs[slab]
                if half != group:
                    q = swapped.setdefault(slab, swap_halves(q))
                parts.append(jnp.where(lo, q, 0.0) if group == 0 else jnp.where(lo, 0.0, q))
            qstack = jnp.concatenate(parts, axis=0).astype(bf16)
            kk = kbuf[r0:r0 + 2 * BLOCK, :]
            s = lax.dot_general(kk, qstack, (((1,), (1,)), ((), ())), preferred_element_type=f32)
            live["s"] = s + (bias_ref[jnp.where(t == 0, 1, 0)] if r0 == 0 else bias_ref[0])

        def softmax(heads):
            for hd in heads:
                sh = live["s"][:, hd * BLOCK:(hd + 1) * BLOCK]
                sink = sinks_ref[layer, hd]
                m = jnp.maximum(jnp.max(sh, axis=0, keepdims=True), sink)
                p = jnp.exp(sh - m)
                live["denoms"].append(jnp.sum(p, axis=0, keepdims=True) + jnp.exp(sink - m))
                live["ps"].append(p.astype(bf16))

        def weighted_values():
            denoms = live["denoms"]
            probs = jnp.concatenate(live["ps"], axis=1)
            vv = vbuf[r0:r0 + 2 * BLOCK, :]
            o = lax.dot_general(vv, probs, (((0,), (0,)), ((), ())), preferred_element_type=f32)
            for j in range(ATT_SLABS):
                rows = []
                for hd in (2 * j, 2 * j + 1):
                    g = hd // ATT_GROUP
                    rows.append(o[g * HALF:(g + 1) * HALF, hd * BLOCK:(hd + 1) * BLOCK] / denoms[hd])
                slab_t = jnp.concatenate(rows, axis=0)
                y_ref[r0:r0 + BLOCK, Y_A + j * LANES:Y_A + (j + 1) * LANES] = slab_t.T.astype(bf16)

        return ([(W_SCORES, scores)] + [(W_SOFTMAX, functools.partial(softmax, (hd,))) for hd in range(ATT_Q_HEADS)]
                + [(W_VALUES, weighted_values)])

    even = lane % 2 == 0

    def rotary(v, cos, sin):
        partner = jnp.where(even, pltpu.roll(v, LANES - 1, 1), pltpu.roll(v, 1, 1))
        return v * cos + partner * sin

    def retention(r0, j):
        zr, zo = zrows(r0)
        cos = cos_ref[r0:r0 + BLOCK, :]
        sin = sin_ref[r0:r0 + BLOCK, :]
        c0 = j * LANES
        q = rotary(zr[zo:zo + BLOCK, C_QR + c0:C_QR + c0 + LANES], cos, sin)
        k = rotary(zr[zo:zo + BLOCK, C_KR + c0:C_KR + c0 + LANES], cos, sin)
        v = zr[zo:zo + BLOCK, C_VR + c0:C_VR + c0 + LANES]
        gate = zr[zo:zo + BLOCK, C_GR + c0:C_GR + c0 + LANES]
        qb = q.astype(bf16)
        kb = k.astype(bf16)
        ksplit = jnp.concatenate([jnp.where(lo, k, 0.0), jnp.where(lo, 0.0, k)], axis=0).astype(bf16)
        sc = lax.dot_general(qb, ksplit, (((1,), (1,)), ((), ())), preferred_element_type=f32)
        acat = (sc * dm_ref[j]).astype(bf16)
        vsplit = jnp.concatenate([jnp.where(lo, v, 0.0), jnp.where(lo, 0.0, v)], axis=0).astype(bf16)
        inner = jnp.dot(acat, vsplit, preferred_element_type=f32)
        r_prev = r_ref[j]
        cross = jnp.dot(qb, r_prev.astype(bf16), preferred_element_type=f32) * xi_ref[j]
        o = inner + cross
        vz = (v * zeta_ref[j]).astype(bf16)
        u = lax.dot_general(kb, vz, (((0,), (0,)), ((), ())), preferred_element_type=f32)
        gt = gtab_ref[j]
        r_ref[j] = r_prev * gt + jnp.where(gt > 0.0, u, 0.0)
        o2 = o * o
        ms_lo = jnp.sum(jnp.where(lo, o2, 0.0), axis=-1, keepdims=True)
        ms_hi = jnp.sum(jnp.where(lo, 0.0, o2), axis=-1, keepdims=True)
        ms = jnp.where(lo, ms_lo, ms_hi) * (1.0 / HEAD_DIM)
        on = o * lax.rsqrt(ms + RMS_EPS)
        y_ref[r0:r0 + BLOCK, Y_R + c0:Y_R + c0 + LANES] = (jax.nn.silu(gate) * on).astype(bf16)

    def pool_tasks(r0):
        pooled = {}

        def window(sl):
            lo_t = lax.broadcasted_iota(jnp.int32, (sub, LANES), 1) < HALF
            pos1 = lax.broadcasted_iota(jnp.int32, (sub, LANES), 0) + (t * tile + r0 + 1)
            w_small, w_big = POOL_WINDOWS[2 * sl], POOL_WINDOWS[2 * sl + 1]

            def shifted(d):
                return ubuf[POOL_TAIL + r0 - d:POOL_TAIL + r0 - d + sub, sl * LANES:(sl + 1) * LANES]

            u0 = shifted(0)
            acc = u0
            for d in range(1, w_small):
                acc = acc + shifted(d)
            small = acc
            for d in range(w_small, w_big):
                acc = acc + shifted(d)
            win = jnp.where(lo_t, small, acc)
            cnt = jnp.minimum(pos1, jnp.where(lo_t, w_small, w_big)).astype(f32)
            pooled[sl] = (win / cnt - u0).astype(bf16)

        def group_maps():
            both = jnp.concatenate([pooled[0], pooled[1]], axis=1)
            mixed = jnp.dot(both, wpool_ref[...], preferred_element_type=f32) * pscale_ref[...]
            y_ref[r0:r0 + sub, Y_P:Y_P + POOL_WIDTH] = mixed.astype(bf16)

        return [(W_POOL_NARROW, functools.partial(window, 0)), (W_POOL_WIDE, functools.partial(window, 1)),
                (W_POOL_MAPS, group_maps)]

    def output_tasks(r0):
        def make(c0):
            def task():
                o_ref[0, r0:r0 + sub, c0:c0 + OUT_CHUNK] = x_ref[0, r0:r0 + sub, c0:c0 + OUT_CHUNK] + jnp.dot(
                    y_ref[r0:r0 + sub, :], wout_ref[:, c0:c0 + OUT_CHUNK], preferred_element_type=f32)
            return task
        return [(1.0, make(c0)) for c0 in range(0, D_MODEL, OUT_CHUNK)]

    def mix_tasks(r0):
        tasks = []
        for b0 in range(r0, r0 + sub, BLOCK):
            tasks += attention_tasks(b0)
            tasks += [(W_RETENTION, functools.partial(retention, b0, j)) for j in range(RET_SLABS)]
        return tasks + pool_tasks(r0)

    for p in range(npiece):
        r0 = p * sub
        if p + 1 < npiece:
            matmuls = project_tasks(lambda r=r0 + sub: x_ref[0, r:r + sub, :], z_ref, r0, r0 + sub)
        else:
            matmuls = project_tasks(lambda: xn_ref[0], z0_ref, 0, tile)
        if p >= 1:
            matmuls = _spread(matmuls, output_tasks(r0 - sub))
        for _, task in _spread(matmuls, mix_tasks(r0)):
            task()
    for _, task in output_tasks(tile - sub):
        task()

    kbuf[0:BLOCK + sub, :] = kbuf[tile:tile + BLOCK + sub, :]
    vbuf[0:BLOCK + sub, :] = vbuf[tile:tile + BLOCK + sub, :]
    ubuf[0:POOL_TAIL + sub, :] = ubuf[tile:tile + POOL_TAIL + sub, :]

    if cast_mlp:
        wup_bf16_ref[...] = wup_f32_ref[...].astype(bf16)
        wdown_bf16_ref[...] = wdown_f32_ref[...].astype(bf16)


_RESIDENT = pl.Buffered(1)


def _const_spec(shape):
    nd = len(shape)
    return pl.BlockSpec(shape, lambda b, t, _nd=nd: (0,) * _nd, pipeline_mode=_RESIDENT)


def _layer_spec(shape, layer):
    nd = len(shape)
    return pl.BlockSpec((None,) + tuple(shape[1:]), lambda b, t, _nd=nd: (layer,) + (0,) * (_nd - 1),
                        pipeline_mode=_RESIDENT)


def _mixer_call(layer, x, sinks, g, win, bias, cos, sin, dm, zeta, xi, gtab, wpool, pscale, wout, mlp_weights=None):
    batch, seq, _ = x.shape
    tile, sub = MIXER_TILE, MIXER_SUB
    ntiles = seq // tile
    grid = (batch, ntiles)
    last_piece = seq // sub - 1
    f32, bf16 = jnp.float32, jnp.bfloat16
    rot_spec = pl.BlockSpec((tile, LANES), lambda b, t: (t, 0))
    x_spec = pl.BlockSpec((1, tile, D_MODEL), lambda b, t: (b, t, 0))
    in_specs = [
        pl.BlockSpec(memory_space=pltpu.SMEM),
        x_spec,
        pl.BlockSpec((1, sub, D_MODEL), lambda b, t: (b, jnp.minimum((t + 1) * (tile // sub), last_piece), 0)),
        _layer_spec(g.shape, layer), _const_spec(win.shape), _const_spec(bias.shape),
        rot_spec, rot_spec,
        _const_spec(dm.shape), _const_spec(zeta.shape), _const_spec(xi.shape), _const_spec(gtab.shape),
        _layer_spec(wpool.shape, layer), _layer_spec(pscale.shape, layer), _const_spec(wout.shape),
    ]
    operands = [sinks, x, x, g, win, bias, cos, sin, dm, zeta, xi, gtab, wpool, pscale, wout]
    out_specs = [x_spec]
    out_shape = [jax.ShapeDtypeStruct(x.shape, x.dtype)]
    if mlp_weights is not None:
        steps = batch * ntiles
        for w in mlp_weights:
            rows = w.shape[1] // steps
            in_specs.append(pl.BlockSpec((None, rows, w.shape[2]), lambda b, t: (layer, b * ntiles + t, 0)))
            out_specs.append(pl.BlockSpec((rows, w.shape[2]), lambda b, t: (b * ntiles + t, 0)))
            out_shape.append(jax.ShapeDtypeStruct(w.shape[1:], bf16))
            operands.append(w)
    outs = pl.pallas_call(
        functools.partial(_mixer_kernel, tile=tile, sub=sub, layer=layer, cast_mlp=mlp_weights is not None),
        grid=grid,
        in_specs=in_specs,
        out_specs=out_specs,
        out_shape=out_shape,
        scratch_shapes=[
            pltpu.VMEM((sub, Z_WIDTH), f32),
            pltpu.VMEM((tile - sub, Z_WIDTH), f32),
            pltpu.VMEM((tile, D_MODEL), bf16),
            pltpu.VMEM((BLOCK + tile + sub, ATT_KV_W), bf16),
            pltpu.VMEM((BLOCK + tile + sub, ATT_KV_W), bf16),
            pltpu.VMEM((POOL_TAIL + tile + sub, POOL_WIDTH), f32),
            pltpu.VMEM((RET_SLABS, LANES, LANES), f32),
        ],
        compiler_params=pltpu.CompilerParams(
            dimension_semantics=("arbitrary", "arbitrary"),
            vmem_limit_bytes=VMEM_LIMIT_BYTES),
        name="mixer",
    )(*operands)
    return outs[0], (tuple(outs[1:]) if mlp_weights is not None else None)


def _mlp_kernel(x_ref, g_ref, wup_ref, wdown_ref, gf_ref, *rest, final_norm, cast_next):
    f32, bf16 = jnp.float32, jnp.bfloat16
    o_ref = rest[len(rest) // 2] if cast_next else rest[0]
    x = x_ref[0]
    h = _rms_norm(x, g_ref[...]).astype(bf16)
    acc = x
    for c in range(D_FF // FF_CHUNK):
        a = jnp.dot(h, wup_ref[:, c * FF_CHUNK:(c + 1) * FF_CHUNK], preferred_element_type=f32)
        a = jnp.square(jnp.maximum(a, 0.0)).astype(bf16)
        acc = acc + jnp.dot(a, wdown_ref[c * FF_CHUNK:(c + 1) * FF_CHUNK, :], preferred_element_type=f32)
    if final_norm:
        acc = _rms_norm(acc, gf_ref[...])
    o_ref[0] = acc
    if cast_next:
        scale_ref, nwin_ref, nwout_ref, nwup_ref, nwdown_ref, _, owin_ref, owout_ref, owup_ref, owdown_ref = rest
        owin_ref[...] = (nwin_ref[...] * scale_ref[...]).astype(bf16)
        owout_ref[...] = nwout_ref[...].astype(bf16)
        owup_ref[...] = nwup_ref[...].astype(bf16)
        owdown_ref[...] = nwdown_ref[...].astype(bf16)


def _mlp_call(layer, x, g, wup, wdown, gf, final_norm, next_weights=None):
    batch, seq, _ = x.shape
    tile = MLP_TILE
    ntiles = seq // tile
    grid = (batch, ntiles)
    x_spec = pl.BlockSpec((1, tile, D_MODEL), lambda b, t: (b, t, 0))
    in_specs = [x_spec, _layer_spec(g.shape, layer), _const_spec(wup.shape), _const_spec(wdown.shape),
                _const_spec(gf.shape)]
    out_specs = [x_spec]
    out_shape = [jax.ShapeDtypeStruct(x.shape, x.dtype)]
    operands = [x, g, wup, wdown, gf]
    if next_weights is not None:
        col_scale, *stacked = next_weights
        steps = batch * ntiles
        in_specs.append(_const_spec(col_scale.shape))
        operands.append(col_scale)
        for w in stacked:
            rows = w.shape[1] // steps
            in_specs.append(pl.BlockSpec((None, rows, w.shape[2]), lambda b, t: (layer + 1, b * ntiles + t, 0)))
            out_specs.append(pl.BlockSpec((rows, w.shape[2]), lambda b, t: (b * ntiles + t, 0)))
            out_shape.append(jax.ShapeDtypeStruct(w.shape[1:], jnp.bfloat16))
            operands.append(w)
    outs = pl.pallas_call(
        functools.partial(_mlp_kernel, final_norm=final_norm, cast_next=next_weights is not None),
        grid=grid,
        in_specs=in_specs,
        out_specs=out_specs,
        out_shape=out_shape,
        compiler_params=pltpu.CompilerParams(
            dimension_semantics=("arbitrary", "arbitrary"),
            vmem_limit_bytes=VMEM_LIMIT_BYTES),
        name="mlp",
    )(*operands)
    return outs[0], (tuple(outs[1:]) if next_weights is not None else None)


def _w_in_col_scale():
    col_scale = np.ones((1, Z_WIDTH), np.float32)
    col_scale[:, C_QA:C_QA + ATT_Q_W] = HEAD_DIM ** -0.5
    return jnp.asarray(col_scale)


def _attention_bias(rel_bias):
    i = np.arange(BLOCK)[:, None]
    j = np.arange(2 * BLOCK)[None, :]
    dist = BLOCK + i - j
    band = (dist >= 0) & (dist < BLOCK)
    onehot = (_t5_bucket(dist).reshape(1, -1) == np.arange(N_BUCKETS)[:, None]).astype(np.float32)
    bias = jnp.dot(rel_bias.astype(jnp.float32).T, onehot, precision=lax.Precision.HIGHEST)
    bias = bias.reshape(ATT_Q_HEADS, BLOCK, 2 * BLOCK)
    normal = jnp.where(jnp.asarray(band)[None], bias, NEG_INF)
    first = jnp.where(jnp.asarray(band & (j >= BLOCK))[None], bias, NEG_INF)
    stacked = jnp.stack([normal, first]).reshape(2, ATT_Q_HEADS * BLOCK, 2 * BLOCK)
    return jnp.transpose(stacked, (0, 2, 1))


def _rotary_tables(seq):
    f32 = np.float32
    inv = (f32(1.0) / np.power(f32(ROPE_BASE), np.linspace(0.0, 1.0, HEAD_DIM // 2, dtype=f32))).astype(f32)
    inv_lane = np.tile(np.repeat(inv, 2), LANES // HEAD_DIM)
    ang = (np.arange(seq, dtype=f32)[:, None] * inv_lane[None, :]).astype(f32)
    cos, sin = np.cos(ang).astype(f32), np.sin(ang).astype(f32)
    even = (np.arange(LANES) % 2 == 0)[None, :]
    return jnp.asarray(cos), jnp.asarray(np.where(even, -sin, sin))


def _retention_tables():
    c = BLOCK
    f32 = np.float32
    scale = f32(HEAD_DIM ** -0.5)
    lg = np.log(f32(1.0) - f32(2.0) ** (f32(-5.0) - np.arange(RET_HEADS, dtype=f32))).astype(f32)
    idx = np.arange(c, dtype=f32)
    diff = idx[:, None] - idx[None, :]
    dmask = np.where(diff >= 0, np.exp(lg[:, None, None] * np.maximum(diff, f32(0.0))), f32(0.0)).astype(f32)
    dm = np.stack([np.concatenate([dmask[2 * j], dmask[2 * j + 1]], axis=1)
                   for j in range(RET_SLABS)]) * scale
    zeta = (np.exp(lg[:, None] * (f32(c) - f32(1.0) - idx)[None, :]) * scale).astype(f32)
    xi = np.exp(lg[None, :] * (idx[:, None] + f32(1.0))).astype(f32)
    g_chunk = np.exp(lg * f32(c)).astype(f32)
    head_of_lane = (np.arange(LANES) >= HALF).astype(np.int32)
    zeta_t = np.stack([zeta[2 * j + head_of_lane].T for j in range(RET_SLABS)])
    xi_t = np.stack([xi[:, 2 * j + head_of_lane] for j in range(RET_SLABS)])
    same = head_of_lane[:, None] == head_of_lane[None, :]
    gtab = np.stack([np.where(same, g_chunk[2 * j + head_of_lane][:, None], f32(0.0))
                     for j in range(RET_SLABS)]).astype(f32)
    return jnp.asarray(dm), jnp.asarray(zeta_t), jnp.asarray(xi_t), jnp.asarray(gtab)


def _pool_block_diag(pool_w):
    depth, groups = pool_w.shape[0], pool_w.shape[1]
    eye = jnp.eye(groups, dtype=jnp.bfloat16)[None, :, None, :, None]
    wide = pool_w.astype(jnp.bfloat16)[:, :, :, None, :] * eye
    return wide.reshape(depth, POOL_WIDTH, POOL_WIDTH)


def kernel(x, attn_norm_g, w_in, attn_sinks, rel_bias, pool_w, pool_scale, w_out, mlp_norm_g, w_up, w_down,
           final_norm_g):
    depth = w_in.shape[0]
    seq = x.shape[1]
    bf16 = jnp.bfloat16
    col_scale = _w_in_col_scale()
    weights = ((w_in[0] * col_scale).astype(bf16), w_out[0].astype(bf16), None, None)
    w_pool = _pool_block_diag(pool_w)
    bias = _attention_bias(rel_bias)
    cos, sin = _rotary_tables(seq)
    dm, zeta_t, xi_t, gtab = _retention_tables()
    g_attn = attn_norm_g.reshape(depth, 1, D_MODEL)
    g_mlp = mlp_norm_g.reshape(depth, 1, D_MODEL)
    p_scale = pool_scale.reshape(depth, 1, POOL_WIDTH)
    gf = final_norm_g.reshape(1, D_MODEL)
    for layer in range(depth):
        last = layer == depth - 1
        w_in_b, w_out_b, w_up_b, w_down_b = weights
        x, converted = _mixer_call(layer, x, attn_sinks, g_attn, w_in_b, bias, cos, sin, dm, zeta_t, xi_t, gtab,
                                   w_pool, p_scale, w_out_b, mlp_weights=(w_up, w_down) if layer == 0 else None)
        if converted is not None:
            w_up_b, w_down_b = converted
        x, weights = _mlp_call(layer, x, g_mlp, w_up_b, w_down_b, gf, final_norm=last,
                               next_weights=None if last else (col_scale, w_in, w_out, w_up, w_down))
    return x
```

```python
import functools

import jax
import jax.numpy as jnp
import numpy as np
from jax import lax
from jax.experimental import pallas as pl
from jax.experimental.pallas import tpu as pltpu

D_MODEL = 1024
HEAD_DIM = 64
ATT_Q_HEADS = 6
ATT_KV_HEADS = 2
ATT_GROUP = ATT_Q_HEADS // ATT_KV_HEADS
RET_HEADS = 6
POOL_WINDOWS = (2, 4, 8, 16)
POOL_GROUP_WIDTH = 64
POOL_WIDTH = 256
BLOCK = 128
N_BUCKETS = 32
MAX_DISTANCE = 128
D_FF = 4 * D_MODEL
RMS_EPS = 1e-6
ROPE_BASE = 10000.0
NEG_INF = -1e30

LANES = 128
HALF = LANES // 2
ATT_Q_W = ATT_Q_HEADS * HEAD_DIM
ATT_KV_W = ATT_KV_HEADS * HEAD_DIM
RET_W = RET_HEADS * HEAD_DIM
RET_SLABS = RET_W // LANES
ATT_SLABS = ATT_Q_W // LANES
POOL_TAIL = 16

C_QA = 0
C_KA = C_QA + ATT_Q_W
C_VA = C_KA + ATT_KV_W
C_QR = C_VA + ATT_KV_W
C_KR = C_QR + RET_W
C_VR = C_KR + RET_W
C_GR = C_VR + RET_W
C_UP = C_GR + RET_W
Z_WIDTH = C_UP + POOL_WIDTH

Y_A = 0
Y_R = ATT_Q_W
Y_P = ATT_Q_W + RET_W

MIXER_TILE = 1024
MIXER_SUB = 128
PROJ_CHUNK = 256
OUT_CHUNK = 256
W_SCORES, W_SOFTMAX, W_VALUES, W_RETENTION = 0.5, 1.0, 1.0, 1.5
W_POOL_NARROW, W_POOL_WIDE, W_POOL_MAPS = 1.0, 3.0, 0.5
MLP_TILE = 1024
FF_CHUNK = 1024
VMEM_LIMIT_BYTES = 56 * 1024 * 1024


def _rms_norm(x, g):
    ms = jnp.mean(x * x, axis=-1, keepdims=True)
    return x * lax.rsqrt(ms + RMS_EPS) * g


def _t5_bucket(dist):
    max_exact = N_BUCKETS // 2
    n = np.maximum(dist, 0)
    large = max_exact + (np.log(np.maximum(n, 1) / max_exact)
                         / np.log(MAX_DISTANCE / max_exact)
                         * (N_BUCKETS - max_exact)).astype(np.int64)
    large = np.minimum(large, N_BUCKETS - 1)
    return np.where(n < max_exact, n, large).astype(np.int32)


def _spread(first, second):
    keyed = []
    for order, tasks in enumerate((first, second)):
        total = sum(weight for weight, _ in tasks)
        done = 0.0
        for weight, task in tasks:
            keyed.append(((done + 0.5 * weight) / total, order, (weight, task)))
            done += weight
    return [item for _, _, item in sorted(keyed, key=lambda entry: entry[:2])]


def _mixer_kernel(sinks_ref, x_ref, xn_ref, g_ref, win_ref, bias_ref, cos_ref, sin_ref, dm_ref,
                  zeta_ref, xi_ref, gtab_ref, wpool_ref, pscale_ref, wout_ref, *rest, tile, sub, layer, cast_mlp):
    if cast_mlp:
        wup_f32_ref, wdown_f32_ref, o_ref, wup_bf16_ref, wdown_bf16_ref, *scratch = rest
    else:
        o_ref, *scratch = rest
    z0_ref, z_ref, y_ref, kbuf, vbuf, ubuf, r_ref = scratch
    t = pl.program_id(1)
    npiece = tile // sub
    assert npiece >= 2, "piece 0 of the next tile is projected while the last piece is mixed"
    f32, bf16 = jnp.float32, jnp.bfloat16

    lane = lax.broadcasted_iota(jnp.int32, (BLOCK, LANES), 1)
    lo = lane < HALF
    top = lax.broadcasted_iota(jnp.int32, (LANES, BLOCK), 0) < HALF

    def zrows(r0):
        return (z0_ref, r0) if r0 < sub else (z_ref, r0 - sub)

    def project_tasks(read_x, zdst, zrow, brow):
        cache = []

        def normed():
            if not cache:
                cache.append(_rms_norm(read_x(), g_ref[...]).astype(bf16))
            return cache[0]

        side = ((C_KA, ATT_KV_W, kbuf, BLOCK), (C_VA, ATT_KV_W, vbuf, BLOCK), (C_UP, POOL_WIDTH, ubuf, POOL_TAIL))
        chunks = [(c0, min(c0 + PROJ_CHUNK, Z_WIDTH)) for c0 in range(0, Z_WIDTH, PROJ_CHUNK)]

        def make(c0, c1):
            def task():
                z = jnp.dot(normed(), win_ref[:, c0:c1], preferred_element_type=f32)
                zdst[zrow:zrow + sub, c0:c1] = z
                for cs, width, buf, head in side:
                    a, b = max(c0, cs), min(c1, cs + width)
                    if a < b:
                        buf[head + brow:head + brow + sub, a - cs:b - cs] = z[:, a - c0:b - c0].astype(buf.dtype)
            return task

        return [((c1 - c0) / PROJ_CHUNK, make(c0, c1)) for c0, c1 in chunks]

    @pl.when(t == 0)
    def _():
        kbuf[0:BLOCK, :] = jnp.zeros((BLOCK, ATT_KV_W), bf16)
        vbuf[0:BLOCK, :] = jnp.zeros((BLOCK, ATT_KV_W), bf16)
        ubuf[0:POOL_TAIL, :] = jnp.zeros((POOL_TAIL, POOL_WIDTH), f32)
        r_ref[...] = jnp.zeros(r_ref.shape, f32)
        for _, task in project_tasks(lambda: x_ref[0, 0:sub, :], z0_ref, 0, 0):
            task()

    def attention_tasks(r0):
        live = {"ps": [], "denoms": []}

        def swap_halves(v):
            return pltpu.roll(v, HALF, 1)

        def scores():
            zr, zo = zrows(r0)
            qs = [zr[zo:zo + BLOCK, C_QA + j * LANES:C_QA + (j + 1) * LANES] for j in range(ATT_SLABS)]
            swapped = {}
            parts = []
            for hd in range(ATT_Q_HEADS):
                slab, half, group = hd // 2, hd % 2, hd // ATT_GROUP
                q = qs[slab]
                if half != group:
                    q = swapped.setdefault(slab, swap_halves(q))
                parts.append(jnp.where(lo, q, 0.0) if group == 0 else jnp.where(lo, 0.0, q))
            qstack = jnp.concatenate(parts, axis=0).astype(bf16)
            kk = kbuf[r0:r0 + 2 * BLOCK, :]
            s = lax.dot_general(kk, qstack, (((1,), (1,)), ((), ())), preferred_element_type=f32)
            live["s"] = s + (bias_ref[jnp.where(t == 0, 1, 0)] if r0 == 0 else bias_ref[0])

        def softmax(heads):
            for hd in heads:
                sh = live["s"][:, hd * BLOCK:(hd + 1) * BLOCK]
                sink = sinks_ref[layer, hd]
                m = jnp.maximum(jnp.max(sh, axis=0, keepdims=True), sink)
                p = jnp.exp(sh - m)
                live["denoms"].append(jnp.sum(p, axis=0, keepdims=True) + jnp.exp(sink - m))
                live["ps"].append(p.astype(bf16))

        def weighted_values():
            denoms = live["denoms"]
            probs = jnp.concatenate(live["ps"], axis=1)
            vv = vbuf[r0:r0 + 2 * BLOCK, :]
            o = lax.dot_general(vv, probs, (((0,), (0,)), ((), ())), preferred_element_type=f32)
            for j in range(ATT_SLABS):
                rows = []
                for hd in (2 * j, 2 * j + 1):
                    g = hd // ATT_GROUP
                    rows.append(o[g * HALF:(g + 1) * HALF, hd * BLOCK:(hd + 1) * BLOCK] / denoms[hd])
                slab_t = jnp.concatenate(rows, axis=0)
                y_ref[r0:r0 + BLOCK, Y_A + j * LANES:Y_A + (j + 1) * LANES] = slab_t.T.astype(bf16)

        return ([(W_SCORES, scores)] + [(W_SOFTMAX, functools.partial(softmax, (hd,))) for hd in range(ATT_Q_HEADS)]
                + [(W_VALUES, weighted_values)])

    even = lane % 2 == 0

    def rotary(v, cos, sin):
        partner = jnp.where(even, pltpu.roll(v, LANES - 1, 1), pltpu.roll(v, 1, 1))
        return v * cos + partner * sin

    def retention(r0, j):
        zr, zo = zrows(r0)
        cos = cos_ref[r0:r0 + BLOCK, :]
        sin = sin_ref[r0:r0 + BLOCK, :]
        c0 = j * LANES
        q = rotary(zr[zo:zo + BLOCK, C_QR + c0:C_QR + c0 + LANES], cos, sin)
        k = rotary(zr[zo:zo + BLOCK, C_KR + c0:C_KR + c0 + LANES], cos, sin)
        v = zr[zo:zo + BLOCK, C_VR + c0:C_VR + c0 + LANES]
        gate = zr[zo:zo + BLOCK, C_GR + c0:C_GR + c0 + LANES]
        qb = q.astype(bf16)
        kt = k.T
        ksplit = jnp.concatenate([jnp.where(top, kt, 0.0), jnp.where(top, 0.0, kt)], axis=1).astype(bf16)
        sc = jnp.dot(qb, ksplit, preferred_element_type=f32)
        acat = (sc * dm_ref[j]).astype(bf16)
        vsplit = jnp.concatenate([jnp.where(lo, v, 0.0), jnp.where(lo, 0.0, v)], axis=0).astype(bf16)
        inner = jnp.dot(acat, vsplit, preferred_element_type=f32)
        r_prev = r_ref[j]
        cross = jnp.dot(qb, r_prev.astype(bf16), preferred_element_type=f32) * xi_ref[j]
        o = inner + cross
        vz = (v * zeta_ref[j]).astype(bf16)
        u = jnp.dot(kt.astype(bf16), vz, preferred_element_type=f32)
        gt = gtab_ref[j]
        r_ref[j] = r_prev * gt + jnp.where(gt > 0.0, u, 0.0)
        o2 = o * o
        ms_lo = jnp.sum(jnp.where(lo, o2, 0.0), axis=-1, keepdims=True)
        ms_hi = jnp.sum(jnp.where(lo, 0.0, o2), axis=-1, keepdims=True)
        ms = jnp.where(lo, ms_lo, ms_hi) * (1.0 / HEAD_DIM)
        on = o * lax.rsqrt(ms + RMS_EPS)
        y_ref[r0:r0 + BLOCK, Y_R + c0:Y_R + c0 + LANES] = (jax.nn.silu(gate) * on).astype(bf16)

    def pool_tasks(r0):
        pooled = {}

        def window(sl):
            lo_t = lax.broadcasted_iota(jnp.int32, (sub, LANES), 1) < HALF
            pos1 = lax.broadcasted_iota(jnp.int32, (sub, LANES), 0) + (t * tile + r0 + 1)
            w_small, w_big = POOL_WINDOWS[2 * sl], POOL_WINDOWS[2 * sl + 1]

            def shifted(d):
                return ubuf[POOL_TAIL + r0 - d:POOL_TAIL + r0 - d + sub, sl * LANES:(sl + 1) * LANES]

            u0 = shifted(0)
            acc = u0
            for d in range(1, w_small):
                acc = acc + shifted(d)
            small = acc
            for d in range(w_small, w_big):
                acc = acc + shifted(d)
            win = jnp.where(lo_t, small, acc)
            cnt = jnp.minimum(pos1, jnp.where(lo_t, w_small, w_big)).astype(f32)
            pooled[sl] = (win / cnt - u0).astype(bf16)

        def group_maps():
            both = jnp.concatenate([pooled[0], pooled[1]], axis=1)
            mixed = jnp.dot(both, wpool_ref[...], preferred_element_type=f32) * pscale_ref[...]
            y_ref[r0:r0 + sub, Y_P:Y_P + POOL_WIDTH] = mixed.astype(bf16)

        return [(W_POOL_NARROW, functools.partial(window, 0)), (W_POOL_WIDE, functools.partial(window, 1)),
                (W_POOL_MAPS, group_maps)]

    def output_tasks(r0):
        def make(c0):
            def task():
                o_ref[0, r0:r0 + sub, c0:c0 + OUT_CHUNK] = x_ref[0, r0:r0 + sub, c0:c0 + OUT_CHUNK] + jnp.dot(
                    y_ref[r0:r0 + sub, :], wout_ref[:, c0:c0 + OUT_CHUNK], preferred_element_type=f32)
            return task
        return [(1.0, make(c0)) for c0 in range(0, D_MODEL, OUT_CHUNK)]

    def mix_tasks(r0):
        tasks = []
        for b0 in range(r0, r0 + sub, BLOCK):
            tasks += attention_tasks(b0)
            tasks += [(W_RETENTION, functools.partial(retention, b0, j)) for j in range(RET_SLABS)]
        return tasks + pool_tasks(r0)

    for p in range(npiece):
        r0 = p * sub
        if p + 1 < npiece:
            matmuls = project_tasks(lambda r=r0 + sub: x_ref[0, r:r + sub, :], z_ref, r0, r0 + sub)
        else:
            matmuls = project_tasks(lambda: xn_ref[0], z0_ref, 0, tile)
        if p >= 1:
            matmuls = _spread(matmuls, output_tasks(r0 - sub))
        for _, task in _spread(matmuls, mix_tasks(r0)):
            task()
    for _, task in output_tasks(tile - sub):
        task()

    kbuf[0:BLOCK + sub, :] = kbuf[tile:tile + BLOCK + sub, :]
    vbuf[0:BLOCK + sub, :] = vbuf[tile:tile + BLOCK + sub, :]
    ubuf[0:POOL_TAIL + sub, :] = ubuf[tile:tile + POOL_TAIL + sub, :]

    if cast_mlp:
        wup_bf16_ref[...] = wup_f32_ref[...].astype(bf16)
        wdown_bf16_ref[...] = wdown_f32_ref[...].astype(bf16)


_RESIDENT = pl.Buffered(1)


def _const_spec(shape):
    nd = len(shape)
    return pl.BlockSpec(shape, lambda b, t, _nd=nd: (0,) * _nd, pipeline_mode=_RESIDENT)


def _layer_spec(shape, layer):
    nd = len(shape)
    return pl.BlockSpec((None,) + tuple(shape[1:]), lambda b, t, _nd=nd: (layer,) + (0,) * (_nd - 1),
                        pipeline_mode=_RESIDENT)


def _mixer_call(layer, x, sinks, g, win, bias, cos, sin, dm, zeta, xi, gtab, wpool, pscale, wout, mlp_weights=None):
    batch, seq, _ = x.shape
    tile, sub = MIXER_TILE, MIXER_SUB
    ntiles = seq // tile
    grid = (batch, ntiles)
    last_piece = seq // sub - 1
    f32, bf16 = jnp.float32, jnp.bfloat16
    rot_spec = pl.BlockSpec((tile, LANES), lambda b, t: (t, 0))
    x_spec = pl.BlockSpec((1, tile, D_MODEL), lambda b, t: (b, t, 0))
    in_specs = [
        pl.BlockSpec(memory_space=pltpu.SMEM),
        x_spec,
        pl.BlockSpec((1, sub, D_MODEL), lambda b, t: (b, jnp.minimum((t + 1) * (tile // sub), last_piece), 0)),
        _layer_spec(g.shape, layer), _const_spec(win.shape), _const_spec(bias.shape),
        rot_spec, rot_spec,
        _const_spec(dm.shape), _const_spec(zeta.shape), _const_spec(xi.shape), _const_spec(gtab.shape),
        _layer_spec(wpool.shape, layer), _layer_spec(pscale.shape, layer), _const_spec(wout.shape),
    ]
    operands = [sinks, x, x, g, win, bias, cos, sin, dm, zeta, xi, gtab, wpool, pscale, wout]
    out_specs = [x_spec]
    out_shape = [jax.ShapeDtypeStruct(x.shape, x.dtype)]
    if mlp_weights is not None:
        steps = batch * ntiles
        for w in mlp_weights:
            rows = w.shape[1] // steps
            in_specs.append(pl.BlockSpec((None, rows, w.shape[2]), lambda b, t: (layer, b * ntiles + t, 0)))
            out_specs.append(pl.BlockSpec((rows, w.shape[2]), lambda b, t: (b * ntiles + t, 0)))
            out_shape.append(jax.ShapeDtypeStruct(w.shape[1:], bf16))
            operands.append(w)
    outs = pl.pallas_call(
        functools.partial(_mixer_kernel, tile=tile, sub=sub, layer=layer, cast_mlp=mlp_weights is not None),
        grid=grid,
        in_specs=in_specs,
        out_specs=out_specs,
        out_shape=out_shape,
        scratch_shapes=[
            pltpu.VMEM((sub, Z_WIDTH), f32),
            pltpu.VMEM((tile - sub, Z_WIDTH), f32),
            pltpu.VMEM((tile, D_MODEL), bf16),
            pltpu.VMEM((BLOCK + tile + sub, ATT_KV_W), bf16),
            pltpu.VMEM((BLOCK + tile + sub, ATT_KV_W), bf16),
            pltpu.VMEM((POOL_TAIL + tile + sub, POOL_WIDTH), f32),
            pltpu.VMEM((RET_SLABS, LANES, LANES), f32),
        ],
        compiler_params=pltpu.CompilerParams(
            dimension_semantics=("arbitrary", "arbitrary"),
            vmem_limit_bytes=VMEM_LIMIT_BYTES),
        name="mixer",
    )(*operands)
    return outs[0], (tuple(outs[1:]) if mlp_weights is not None else None)


def _mlp_kernel(x_ref, g_ref, wup_ref, wdown_ref, gf_ref, *rest, final_norm, cast_next):
    f32, bf16 = jnp.float32, jnp.bfloat16
    o_ref = rest[len(rest) // 2] if cast_next else rest[0]
    x = x_ref[0]
    h = _rms_norm(x, g_ref[...]).astype(bf16)
    acc = x
    for c in range(D_FF // FF_CHUNK):
        a = jnp.dot(h, wup_ref[:, c * FF_CHUNK:(c + 1) * FF_CHUNK], preferred_element_type=f32)
        a = jnp.square(jnp.maximum(a, 0.0)).astype(bf16)
        acc = acc + jnp.dot(a, wdown_ref[c * FF_CHUNK:(c + 1) * FF_CHUNK, :], preferred_element_type=f32)
    if final_norm:
        acc = _rms_norm(acc, gf_ref[...])
    o_ref[0] = acc
    if cast_next:
        scale_ref, nwin_ref, nwout_ref, nwup_ref, nwdown_ref, _, owin_ref, owout_ref, owup_ref, owdown_ref = rest
        owin_ref[...] = (nwin_ref[...] * scale_ref[...]).astype(bf16)
        owout_ref[...] = nwout_ref[...].astype(bf16)
        owup_ref[...] = nwup_ref[...].astype(bf16)
        owdown_ref[...] = nwdown_ref[...].astype(bf16)


def _mlp_call(layer, x, g, wup, wdown, gf, final_norm, next_weights=None):
    batch, seq, _ = x.shape
    tile = MLP_TILE
    ntiles = seq // tile
    grid = (batch, ntiles)
    x_spec = pl.BlockSpec((1, tile, D_MODEL), lambda b, t: (b, t, 0))
    in_specs = [x_spec, _layer_spec(g.shape, layer), _const_spec(wup.shape), _const_spec(wdown.shape),
                _const_spec(gf.shape)]
    out_specs = [x_spec]
    out_shape = [jax.ShapeDtypeStruct(x.shape, x.dtype)]
    operands = [x, g, wup, wdown, gf]
    if next_weights is not None:
        col_scale, *stacked = next_weights
        steps = batch * ntiles
        in_specs.append(_const_spec(col_scale.shape))
        operands.append(col_scale)
        for w in stacked:
            rows = w.shape[1] // steps
            in_specs.append(pl.BlockSpec((None, rows, w.shape[2]), lambda b, t: (layer + 1, b * ntiles + t, 0)))
            out_specs.append(pl.BlockSpec((rows, w.shape[2]), lambda b, t: (b * ntiles + t, 0)))
            out_shape.append(jax.ShapeDtypeStruct(w.shape[1:], jnp.bfloat16))
            operands.append(w)
    outs = pl.pallas_call(
        functools.partial(_mlp_kernel, final_norm=final_norm, cast_next=next_weights is not None),
        grid=grid,
        in_specs=in_specs,
        out_specs=out_specs,
        out_shape=out_shape,
        compiler_params=pltpu.CompilerParams(
            dimension_semantics=("arbitrary", "arbitrary"),
            vmem_limit_bytes=VMEM_LIMIT_BYTES),
        name="mlp",
    )(*operands)
    return outs[0], (tuple(outs[1:]) if next_weights is not None else None)


def _w_in_col_scale():
    col_scale = np.ones((1, Z_WIDTH), np.float32)
    col_scale[:, C_QA:C_QA + ATT_Q_W] = HEAD_DIM ** -0.5
    return jnp.asarray(col_scale)


def _attention_bias(rel_bias):
    i = np.arange(BLOCK)[:, None]
    j = np.arange(2 * BLOCK)[None, :]
    dist = BLOCK + i - j
    band = (dist >= 0) & (dist < BLOCK)
    onehot = (_t5_bucket(dist).reshape(1, -1) == np.arange(N_BUCKETS)[:, None]).astype(np.float32)
    bias = jnp.dot(rel_bias.astype(jnp.float32).T, onehot, precision=lax.Precision.HIGHEST)
    bias = bias.reshape(ATT_Q_HEADS, BLOCK, 2 * BLOCK)
    normal = jnp.where(jnp.asarray(band)[None], bias, NEG_INF)
    first = jnp.where(jnp.asarray(band & (j >= BLOCK))[None], bias, NEG_INF)
    stacked = jnp.stack([normal, first]).reshape(2, ATT_Q_HEADS * BLOCK, 2 * BLOCK)
    return jnp.transpose(stacked, (0, 2, 1))


def _rotary_tables(seq):
    f32 = np.float32
    inv = (f32(1.0) / np.power(f32(ROPE_BASE), np.linspace(0.0, 1.0, HEAD_DIM // 2, dtype=f32))).astype(f32)
    inv_lane = np.tile(np.repeat(inv, 2), LANES // HEAD_DIM)
    ang = (np.arange(seq, dtype=f32)[:, None] * inv_lane[None, :]).astype(f32)
    cos, sin = np.cos(ang).astype(f32), np.sin(ang).astype(f32)
    even = (np.arange(LANES) % 2 == 0)[None, :]
    return jnp.asarray(cos), jnp.asarray(np.where(even, -sin, sin))


def _retention_tables():
    c = BLOCK
    f32 = np.float32
    scale = f32(HEAD_DIM ** -0.5)
    lg = np.log(f32(1.0) - f32(2.0) ** (f32(-5.0) - np.arange(RET_HEADS, dtype=f32))).astype(f32)
    idx = np.arange(c, dtype=f32)
    diff = idx[:, None] - idx[None, :]
    dmask = np.where(diff >= 0, np.exp(lg[:, None, None] * np.maximum(diff, f32(0.0))), f32(0.0)).astype(f32)
    dm = np.stack([np.concatenate([dmask[2 * j], dmask[2 * j + 1]], axis=1)
                   for j in range(RET_SLABS)]) * scale
    zeta = (np.exp(lg[:, None] * (f32(c) - f32(1.0) - idx)[None, :]) * scale).astype(f32)
    xi = np.exp(lg[None, :] * (idx[:, None] + f32(1.0))).astype(f32)
    g_chunk = np.exp(lg * f32(c)).astype(f32)
    head_of_lane = (np.arange(LANES) >= HALF).astype(np.int32)
    zeta_t = np.stack([zeta[2 * j + head_of_lane].T for j in range(RET_SLABS)])
    xi_t = np.stack([xi[:, 2 * j + head_of_lane] for j in range(RET_SLABS)])
    same = head_of_lane[:, None] == head_of_lane[None, :]
    gtab = np.stack([np.where(same, g_chunk[2 * j + head_of_lane][:, None], f32(0.0))
                     for j in range(RET_SLABS)]).astype(f32)
    return jnp.asarray(dm), jnp.asarray(zeta_t), jnp.asarray(xi_t), jnp.asarray(gtab)


def _pool_block_diag(pool_w):
    depth, groups = pool_w.shape[0], pool_w.shape[1]
    eye = jnp.eye(groups, dtype=jnp.bfloat16)[None, :, None, :, None]
    wide = pool_w.astype(jnp.bfloat16)[:, :, :, None, :] * eye
    return wide.reshape(depth, POOL_WIDTH, POOL_WIDTH)


def kernel(x, attn_norm_g, w_in, attn_sinks, rel_bias, pool_w, pool_scale, w_out, mlp_norm_g, w_up, w_down,
           final_norm_g):
    depth = w_in.shape[0]
    seq = x.shape[1]
    bf16 = jnp.bfloat16
    col_scale = _w_in_col_scale()
    weights = ((w_in[0] * col_scale).astype(bf16), w_out[0].astype(bf16), None, None)
    w_pool = _pool_block_diag(pool_w)
    bias = _attention_bias(rel_bias)
    cos, sin = _rotary_tables(seq)
    dm, zeta_t, xi_t, gtab = _retention_tables()
    g_attn = attn_norm_g.reshape(depth, 1, D_MODEL)
    g_mlp = mlp_norm_g.reshape(depth, 1, D_MODEL)
    p_scale = pool_scale.reshape(depth, 1, POOL_WIDTH)
    gf = final_norm_g.reshape(1, D_MODEL)
    for layer in range(depth):
        last = layer == depth - 1
        w_in_b, w_out_b, w_up_b, w_down_b = weights
        x, converted = _mixer_call(layer, x, attn_sinks, g_attn, w_in_b, bias, cos, sin, dm, zeta_t, xi_t, gtab,
                                   w_pool, p_scale, w_out_b, mlp_weights=(w_up, w_down) if layer == 0 else None)
        if converted is not None:
            w_up_b, w_down_b = converted
        x, weights = _mlp_call(layer, x, g_mlp, w_up_b, w_down_b, gf, final_norm=last,
                               next_weights=None if last else (col_scale, w_in, w_out, w_up, w_down))
    return x
```

```python
import functools

import jax
import jax.numpy as jnp
import numpy as np
from jax import lax
from jax.experimental import pallas as pl
from jax.experimental.pallas import tpu as pltpu

D_MODEL = 1024
HEAD_DIM = 64
ATT_Q_HEADS = 6
ATT_KV_HEADS = 2
ATT_GROUP = ATT_Q_HEADS // ATT_KV_HEADS
RET_HEADS = 6
POOL_WINDOWS = (2, 4, 8, 16)
POOL_GROUP_WIDTH = 64
POOL_WIDTH = 256
BLOCK = 128
N_BUCKETS = 32
MAX_DISTANCE = 128
D_FF = 4 * D_MODEL
RMS_EPS = 1e-6
ROPE_BASE = 10000.0
NEG_INF = -1e30

LANES = 128
HALF = LANES // 2
ATT_Q_W = ATT_Q_HEADS * HEAD_DIM
ATT_KV_W = ATT_KV_HEADS * HEAD_DIM
RET_W = RET_HEADS * HEAD_DIM
RET_SLABS = RET_W // LANES
ATT_SLABS = ATT_Q_W // LANES
POOL_TAIL = 16

C_QA = 0
C_KA = C_QA + ATT_Q_W
C_VA = C_KA + ATT_KV_W
C_QR = C_VA + ATT_KV_W
C_KR = C_QR + RET_W
C_VR = C_KR + RET_W
C_GR = C_VR + RET_W
C_UP = C_GR + RET_W
Z_WIDTH = C_UP + POOL_WIDTH

Y_A = 0
Y_R = ATT_Q_W
Y_P = ATT_Q_W + RET_W

MIXER_TILE = 1024
MIXER_SUB = 128
PROJ_CHUNK = 256
OUT_CHUNK = 256
W_SCORES, W_SOFTMAX, W_VALUES, W_RETENTION = 0.5, 1.0, 1.0, 1.5
W_POOL_NARROW, W_POOL_WIDE, W_POOL_MAPS = 1.0, 3.0, 0.5
MLP_TILE = 1024
FF_CHUNK = 1024
VMEM_LIMIT_BYTES = 56 * 1024 * 1024


def _rms_norm(x, g):
    ms = jnp.mean(x * x, axis=-1, keepdims=True)
    return x * lax.rsqrt(ms + RMS_EPS) * g


def _t5_bucket(dist):
    max_exact = N_BUCKETS // 2
    n = np.maximum(dist, 0)
    large = max_exact + (np.log(np.maximum(n, 1) / max_exact)
                         / np.log(MAX_DISTANCE / max_exact)
                         * (N_BUCKETS - max_exact)).astype(np.int64)
    large = np.minimum(large, N_BUCKETS - 1)
    return np.where(n < max_exact, n, large).astype(np.int32)


def _spread(first, second):
    keyed = []
    for order, tasks in enumerate((first, second)):
        total = sum(weight for weight, _ in tasks)
        done = 0.0
        for weight, task in tasks:
            keyed.append(((done + 0.5 * weight) / total, order, (weight, task)))
            done += weight
    return [item for _, _, item in sorted(keyed, key=lambda entry: entry[:2])]


def _mixer_kernel(sinks_ref, x_ref, xn_ref, g_ref, win_ref, bias_ref, cos_ref, sin_ref, dm_ref,
                  zeta_ref, xi_ref, gtab_ref, wpool_ref, pscale_ref, wout_ref, *rest, tile, sub, layer, cast_mlp):
    if cast_mlp:
        wup_f32_ref, wdown_f32_ref, o_ref, wup_bf16_ref, wdown_bf16_ref, *scratch = rest
    else:
        o_ref, *scratch = rest
    z0_ref, z_ref, y_ref, kbuf, vbuf, ubuf, r_ref = scratch
    t = pl.program_id(1)
    npiece = tile // sub
    assert npiece >= 2, "piece 0 of the next tile is projected while the last piece is mixed"
    f32, bf16 = jnp.float32, jnp.bfloat16

    lane = lax.broadcasted_iota(jnp.int32, (BLOCK, LANES), 1)
    lo = lane < HALF

    def zrows(r0):
        return (z0_ref, r0) if r0 < sub else (z_ref, r0 - sub)

    def project_tasks(read_x, zdst, zrow, brow):
        cache = []

        def normed():
            if not cache:
                cache.append(_rms_norm(read_x(), g_ref[...]).astype(bf16))
            return cache[0]

        side = ((C_KA, ATT_KV_W, kbuf, BLOCK), (C_VA, ATT_KV_W, vbuf, BLOCK), (C_UP, POOL_WIDTH, ubuf, POOL_TAIL))
        chunks = [(c0, min(c0 + PROJ_CHUNK, Z_WIDTH)) for c0 in range(0, Z_WIDTH, PROJ_CHUNK)]

        def make(c0, c1):
            def task():
                z = jnp.dot(normed(), win_ref[:, c0:c1], preferred_element_type=f32)
                zdst[zrow:zrow + sub, c0:c1] = z
                for cs, width, buf, head in side:
                    a, b = max(c0, cs), min(c1, cs + width)
                    if a < b:
                        buf[head + brow:head + brow + sub, a - cs:b - cs] = z[:, a - c0:b - c0].astype(buf.dtype)
            return task

        return [((c1 - c0) / PROJ_CHUNK, make(c0, c1)) for c0, c1 in chunks]

    @pl.when(t == 0)
    def _():
        kbuf[0:BLOCK, :] = jnp.zeros((BLOCK, ATT_KV_W), bf16)
        vbuf[0:BLOCK, :] = jnp.zeros((BLOCK, ATT_KV_W), bf16)
        ubuf[0:POOL_TAIL, :] = jnp.zeros((POOL_TAIL, POOL_WIDTH), f32)
        r_ref[...] = jnp.zeros(r_ref.shape, f32)
        for _, task in project_tasks(lambda: x_ref[0, 0:sub, :], z0_ref, 0, 0):
            task()

    def attention_tasks(r0):
        live = {"ps": [], "denoms": []}

        def swap_halves(v):
            return pltpu.roll(v, HALF, 1)

        def scores():
            zr, zo = zrows(r0)
            qs = [zr[zo:zo + BLOCK, C_QA + j * LANES:C_QA + (j + 1) * LANES] for j in range(ATT_SLABS)]
            swapped = {}
            parts = []
            for hd in range(ATT_Q_HEADS):
                slab, half, group = hd // 2, hd % 2, hd // ATT_GROUP
                q = qs[slab]
                if half != group:
                    q = swapped.setdefault(slab, swap_halves(q))
                parts.append(jnp.where(lo, q, 0.0) if group == 0 else jnp.where(lo, 0.0, q))
            qstack = jnp.concatenate(parts, axis=0).astype(bf16)
            kk = kbuf[r0:r0 + 2 * BLOCK, :]
            s = lax.dot_general(kk, qstack, (((1,), (1,)), ((), ())), preferred_element_type=f32)
            live["s"] = s + (bias_ref[jnp.where(t == 0, 1, 0)] if r0 == 0 else bias_ref[0])

        def softmax(heads):
            for hd in heads:
                sh = live["s"][:, hd * BLOCK:(hd + 1) * BLOCK]
                sink = sinks_ref[layer, hd]
                m = jnp.maximum(jnp.max(sh, axis=0, keepdims=True), sink)
                p = jnp.exp(sh - m)
                live["denoms"].append(jnp.sum(p, axis=0, keepdims=True) + jnp.exp(sink - m))
                live["ps"].append(p.astype(bf16))

        def weighted_values():
            denoms = live["denoms"]
            probs = jnp.concatenate(live["ps"], axis=1)
            vv = vbuf[r0:r0 + 2 * BLOCK, :]
            o = lax.dot_general(vv, probs, (((0,), (0,)), ((), ())), preferred_element_type=f32)
            for j in range(ATT_SLABS):
                rows = []
                for hd in (2 * j, 2 * j + 1):
                    g = hd // ATT_GROUP
                    rows.append(o[g * HALF:(g + 1) * HALF, hd * BLOCK:(hd + 1) * BLOCK] / denoms[hd])
                slab_t = jnp.concatenate(rows, axis=0)
                y_ref[r0:r0 + BLOCK, Y_A + j * LANES:Y_A + (j + 1) * LANES] = slab_t.T.astype(bf16)

        return ([(W_SCORES, scores)] + [(W_SOFTMAX, functools.partial(softmax, (hd,))) for hd in range(ATT_Q_HEADS)]
                + [(W_VALUES, weighted_values)])

    even = lane % 2 == 0

    def rotary(v, cos, sin):
        partner = jnp.where(even, pltpu.roll(v, LANES - 1, 1), pltpu.roll(v, 1, 1))
        return v * cos + partner * sin

    def retention(r0, j):
        zr, zo = zrows(r0)
        cos = cos_ref[r0:r0 + BLOCK, :]
        sin = sin_ref[r0:r0 + BLOCK, :]
        c0 = j * LANES
        q = rotary(zr[zo:zo + BLOCK, C_QR + c0:C_QR + c0 + LANES], cos, sin)
        k = rotary(zr[zo:zo + BLOCK, C_KR + c0:C_KR + c0 + LANES], cos, sin)
        v = zr[zo:zo + BLOCK, C_VR + c0:C_VR + c0 + LANES]
        gate = zr[zo:zo + BLOCK, C_GR + c0:C_GR + c0 + LANES]
        qb = q.astype(bf16)
        kb = k.astype(bf16)
        ksplit = jnp.concatenate([jnp.where(lo, k, 0.0), jnp.where(lo, 0.0, k)], axis=0).astype(bf16)
        rt_prev = r_ref[j]
        wide = lax.dot_general(qb, jnp.concatenate([ksplit, rt_prev.astype(bf16)], axis=0),
                               (((1,), (1,)), ((), ())), preferred_element_type=f32)
        acat = (wide[:, 0:2 * BLOCK] * dm_ref[j]).astype(bf16)
        vsplit = jnp.concatenate([jnp.where(lo, v, 0.0), jnp.where(lo, 0.0, v)], axis=0).astype(bf16)
        inner = jnp.dot(acat, vsplit, preferred_element_type=f32)
        o = inner + wide[:, 2 * BLOCK:] * xi_ref[j]
        vz = (v * zeta_ref[j]).astype(bf16)
        ut = lax.dot_general(vz, kb, (((0,), (0,)), ((), ())), preferred_element_type=f32)
        gt = gtab_ref[j]
        r_ref[j] = rt_prev * gt + jnp.where(gt > 0.0, ut, 0.0)
        o2 = o * o
        ms_lo = jnp.sum(jnp.where(lo, o2, 0.0), axis=-1, keepdims=True)
        ms_hi = jnp.sum(jnp.where(lo, 0.0, o2), axis=-1, keepdims=True)
        ms = jnp.where(lo, ms_lo, ms_hi) * (1.0 / HEAD_DIM)
        on = o * lax.rsqrt(ms + RMS_EPS)
        y_ref[r0:r0 + BLOCK, Y_R + c0:Y_R + c0 + LANES] = (jax.nn.silu(gate) * on).astype(bf16)

    def pool_tasks(r0):
        pooled = {}

        def window(sl):
            lo_t = lax.broadcasted_iota(jnp.int32, (sub, LANES), 1) < HALF
            pos1 = lax.broadcasted_iota(jnp.int32, (sub, LANES), 0) + (t * tile + r0 + 1)
            w_small, w_big = POOL_WINDOWS[2 * sl], POOL_WINDOWS[2 * sl + 1]

            def shifted(d):
                return ubuf[POOL_TAIL + r0 - d:POOL_TAIL + r0 - d + sub, sl * LANES:(sl + 1) * LANES]

            u0 = shifted(0)
            acc = u0
            for d in range(1, w_small):
                acc = acc + shifted(d)
            small = acc
            for d in range(w_small, w_big):
                acc = acc + shifted(d)
            win = jnp.where(lo_t, small, acc)
            cnt = jnp.minimum(pos1, jnp.where(lo_t, w_small, w_big)).astype(f32)
            pooled[sl] = (win / cnt - u0).astype(bf16)

        def group_maps():
            both = jnp.concatenate([pooled[0], pooled[1]], axis=1)
            mixed = jnp.dot(both, wpool_ref[...], preferred_element_type=f32) * pscale_ref[...]
            y_ref[r0:r0 + sub, Y_P:Y_P + POOL_WIDTH] = mixed.astype(bf16)

        return [(W_POOL_NARROW, functools.partial(window, 0)), (W_POOL_WIDE, functools.partial(window, 1)),
                (W_POOL_MAPS, group_maps)]

    def output_tasks(r0):
        def make(c0):
            def task():
                o_ref[0, r0:r0 + sub, c0:c0 + OUT_CHUNK] = x_ref[0, r0:r0 + sub, c0:c0 + OUT_CHUNK] + jnp.dot(
                    y_ref[r0:r0 + sub, :], wout_ref[:, c0:c0 + OUT_CHUNK], preferred_element_type=f32)
            return task
        return [(1.0, make(c0)) for c0 in range(0, D_MODEL, OUT_CHUNK)]

    def mix_tasks(r0):
        tasks = []
        for b0 in range(r0, r0 + sub, BLOCK):
            tasks += attention_tasks(b0)
            tasks += [(W_RETENTION, functools.partial(retention, b0, j)) for j in range(RET_SLABS)]
        return tasks + pool_tasks(r0)

    for p in range(npiece):
        r0 = p * sub
        if p + 1 < npiece:
            matmuls = project_tasks(lambda r=r0 + sub: x_ref[0, r:r + sub, :], z_ref, r0, r0 + sub)
        else:
            matmuls = project_tasks(lambda: xn_ref[0], z0_ref, 0, tile)
        if p >= 1:
            matmuls = _spread(matmuls, output_tasks(r0 - sub))
        for _, task in _spread(matmuls, mix_tasks(r0)):
            task()
    for _, task in output_tasks(tile - sub):
        task()

    kbuf[0:BLOCK + sub, :] = kbuf[tile:tile + BLOCK + sub, :]
    vbuf[0:BLOCK + sub, :] = vbuf[tile:tile + BLOCK + sub, :]
    ubuf[0:POOL_TAIL + sub, :] = ubuf[tile:tile + POOL_TAIL + sub, :]

    if cast_mlp:
        wup_bf16_ref[...] = wup_f32_ref[...].astype(bf16)
        wdown_bf16_ref[...] = wdown_f32_ref[...].astype(bf16)


_RESIDENT = pl.Buffered(1)


def _const_spec(shape):
    nd = len(shape)
    return pl.BlockSpec(shape, lambda b, t, _nd=nd: (0,) * _nd, pipeline_mode=_RESIDENT)


def _layer_spec(shape, layer):
    nd = len(shape)
    return pl.BlockSpec((None,) + tuple(shape[1:]), lambda b, t, _nd=nd: (layer,) + (0,) * (_nd - 1),
                        pipeline_mode=_RESIDENT)


def _mixer_call(layer, x, sinks, g, win, bias, cos, sin, dm, zeta, xi, gtab, wpool, pscale, wout, mlp_weights=None):
    batch, seq, _ = x.shape
    tile, sub = MIXER_TILE, MIXER_SUB
    ntiles = seq // tile
    grid = (batch, ntiles)
    last_piece = seq // sub - 1
    f32, bf16 = jnp.float32, jnp.bfloat16
    rot_spec = pl.BlockSpec((tile, LANES), lambda b, t: (t, 0))
    x_spec = pl.BlockSpec((1, tile, D_MODEL), lambda b, t: (b, t, 0))
    in_specs = [
        pl.BlockSpec(memory_space=pltpu.SMEM),
        x_spec,
        pl.BlockSpec((1, sub, D_MODEL), lambda b, t: (b, jnp.minimum((t + 1) * (tile // sub), last_piece), 0)),
        _layer_spec(g.shape, layer), _const_spec(win.shape), _const_spec(bias.shape),
        rot_spec, rot_spec,
        _const_spec(dm.shape), _const_spec(zeta.shape), _const_spec(xi.shape), _const_spec(gtab.shape),
        _layer_spec(wpool.shape, layer), _layer_spec(pscale.shape, layer), _const_spec(wout.shape),
    ]
    operands = [sinks, x, x, g, win, bias, cos, sin, dm, zeta, xi, gtab, wpool, pscale, wout]
    out_specs = [x_spec]
    out_shape = [jax.ShapeDtypeStruct(x.shape, x.dtype)]
    if mlp_weights is not None:
        steps = batch * ntiles
        for w in mlp_weights:
            rows = w.shape[1] // steps
            in_specs.append(pl.BlockSpec((None, rows, w.shape[2]), lambda b, t: (layer, b * ntiles + t, 0)))
            out_specs.append(pl.BlockSpec((rows, w.shape[2]), lambda b, t: (b * ntiles + t, 0)))
            out_shape.append(jax.ShapeDtypeStruct(w.shape[1:], bf16))
            operands.append(w)
    outs = pl.pallas_call(
        functools.partial(_mixer_kernel, tile=tile, sub=sub, layer=layer, cast_mlp=mlp_weights is not None),
        grid=grid,
        in_specs=in_specs,
        out_specs=out_specs,
        out_shape=out_shape,
        scratch_shapes=[
            pltpu.VMEM((sub, Z_WIDTH), f32),
            pltpu.VMEM((tile - sub, Z_WIDTH), f32),
            pltpu.VMEM((tile, D_MODEL), bf16),
            pltpu.VMEM((BLOCK + tile + sub, ATT_KV_W), bf16),
            pltpu.VMEM((BLOCK + tile + sub, ATT_KV_W), bf16),
            pltpu.VMEM((POOL_TAIL + tile + sub, POOL_WIDTH), f32),
            pltpu.VMEM((RET_SLABS, LANES, LANES), f32),
        ],
        compiler_params=pltpu.CompilerParams(
            dimension_semantics=("arbitrary", "arbitrary"),
            vmem_limit_bytes=VMEM_LIMIT_BYTES),
        name="mixer",
    )(*operands)
    return outs[0], (tuple(outs[1:]) if mlp_weights is not None else None)


def _mlp_kernel(x_ref, g_ref, wup_ref, wdown_ref, gf_ref, *rest, final_norm, cast_next):
    f32, bf16 = jnp.float32, jnp.bfloat16
    o_ref = rest[len(rest) // 2] if cast_next else rest[0]
    x = x_ref[0]
    h = _rms_norm(x, g_ref[...]).astype(bf16)
    acc = x
    for c in range(D_FF // FF_CHUNK):
        a = jnp.dot(h, wup_ref[:, c * FF_CHUNK:(c + 1) * FF_CHUNK], preferred_element_type=f32)
        a = jnp.square(jnp.maximum(a, 0.0)).astype(bf16)
        acc = acc + jnp.dot(a, wdown_ref[c * FF_CHUNK:(c + 1) * FF_CHUNK, :], preferred_element_type=f32)
    if final_norm:
        acc = _rms_norm(acc, gf_ref[...])
    o_ref[0] = acc
    if cast_next:
        scale_ref, nwin_ref, nwout_ref, nwup_ref, nwdown_ref, _, owin_ref, owout_ref, owup_ref, owdown_ref = rest
        owin_ref[...] = (nwin_ref[...] * scale_ref[...]).astype(bf16)
        owout_ref[...] = nwout_ref[...].astype(bf16)
        owup_ref[...] = nwup_ref[...].astype(bf16)
        owdown_ref[...] = nwdown_ref[...].astype(bf16)


def _mlp_call(layer, x, g, wup, wdown, gf, final_norm, next_weights=None):
    batch, seq, _ = x.shape
    tile = MLP_TILE
    ntiles = seq // tile
    grid = (batch, ntiles)
    x_spec = pl.BlockSpec((1, tile, D_MODEL), lambda b, t: (b, t, 0))
    in_specs = [x_spec, _layer_spec(g.shape, layer), _const_spec(wup.shape), _const_spec(wdown.shape),
                _const_spec(gf.shape)]
    out_specs = [x_spec]
    out_shape = [jax.ShapeDtypeStruct(x.shape, x.dtype)]
    operands = [x, g, wup, wdown, gf]
    if next_weights is not None:
        col_scale, *stacked = next_weights
        steps = batch * ntiles
        in_specs.append(_const_spec(col_scale.shape))
        operands.append(col_scale)
        for w in stacked:
            rows = w.shape[1] // steps
            in_specs.append(pl.BlockSpec((None, rows, w.shape[2]), lambda b, t: (layer + 1, b * ntiles + t, 0)))
            out_specs.append(pl.BlockSpec((rows, w.shape[2]), lambda b, t: (b * ntiles + t, 0)))
            out_shape.append(jax.ShapeDtypeStruct(w.shape[1:], jnp.bfloat16))
            operands.append(w)
    outs = pl.pallas_call(
        functools.partial(_mlp_kernel, final_norm=final_norm, cast_next=next_weights is not None),
        grid=grid,
        in_specs=in_specs,
        out_specs=out_specs,
        out_shape=out_shape,
        compiler_params=pltpu.CompilerParams(
            dimension_semantics=("arbitrary", "arbitrary"),
            vmem_limit_bytes=VMEM_LIMIT_BYTES),
        name="mlp",
    )(*operands)
    return outs[0], (tuple(outs[1:]) if next_weights is not None else None)


def _w_in_col_scale():
    col_scale = np.ones((1, Z_WIDTH), np.float32)
    col_scale[:, C_QA:C_QA + ATT_Q_W] = HEAD_DIM ** -0.5
    return jnp.asarray(col_scale)


def _attention_bias(rel_bias):
    i = np.arange(BLOCK)[:, None]
    j = np.arange(2 * BLOCK)[None, :]
    dist = BLOCK + i - j
    band = (dist >= 0) & (dist < BLOCK)
    onehot = (_t5_bucket(dist).reshape(1, -1) == np.arange(N_BUCKETS)[:, None]).astype(np.float32)
    bias = jnp.dot(rel_bias.astype(jnp.float32).T, onehot, precision=lax.Precision.HIGHEST)
    bias = bias.reshape(ATT_Q_HEADS, BLOCK, 2 * BLOCK)
    normal = jnp.where(jnp.asarray(band)[None], bias, NEG_INF)
    first = jnp.where(jnp.asarray(band & (j >= BLOCK))[None], bias, NEG_INF)
    stacked = jnp.stack([normal, first]).reshape(2, ATT_Q_HEADS * BLOCK, 2 * BLOCK)
    return jnp.transpose(stacked, (0, 2, 1))


def _rotary_tables(seq):
    f32 = np.float32
    inv = (f32(1.0) / np.power(f32(ROPE_BASE), np.linspace(0.0, 1.0, HEAD_DIM // 2, dtype=f32))).astype(f32)
    inv_lane = np.tile(np.repeat(inv, 2), LANES // HEAD_DIM)
    ang = (np.arange(seq, dtype=f32)[:, None] * inv_lane[None, :]).astype(f32)
    cos, sin = np.cos(ang).astype(f32), np.sin(ang).astype(f32)
    even = (np.arange(LANES) % 2 == 0)[None, :]
    return jnp.asarray(cos), jnp.asarray(np.where(even, -sin, sin))


def _retention_tables():
    c = BLOCK
    f32 = np.float32
    scale = f32(HEAD_DIM ** -0.5)
    lg = np.log(f32(1.0) - f32(2.0) ** (f32(-5.0) - np.arange(RET_HEADS, dtype=f32))).astype(f32)
    idx = np.arange(c, dtype=f32)
    diff = idx[:, None] - idx[None, :]
    dmask = np.where(diff >= 0, np.exp(lg[:, None, None] * np.maximum(diff, f32(0.0))), f32(0.0)).astype(f32)
    dm = np.stack([np.concatenate([dmask[2 * j], dmask[2 * j + 1]], axis=1)
                   for j in range(RET_SLABS)]) * scale
    zeta = (np.exp(lg[:, None] * (f32(c) - f32(1.0) - idx)[None, :]) * scale).astype(f32)
    xi = np.exp(lg[None, :] * (idx[:, None] + f32(1.0))).astype(f32)
    g_chunk = np.exp(lg * f32(c)).astype(f32)
    head_of_lane = (np.arange(LANES) >= HALF).astype(np.int32)
    zeta_t = np.stack([zeta[2 * j + head_of_lane].T for j in range(RET_SLABS)])
    xi_t = np.stack([xi[:, 2 * j + head_of_lane] for j in range(RET_SLABS)])
    same = head_of_lane[:, None] == head_of_lane[None, :]
    gtab = np.stack([np.where(same, g_chunk[2 * j + head_of_lane][:, None], f32(0.0))
                     for j in range(RET_SLABS)]).astype(f32)
    return jnp.asarray(dm), jnp.asarray(zeta_t), jnp.asarray(xi_t), jnp.asarray(gtab)


def _pool_block_diag(pool_w):
    depth, groups = pool_w.shape[0], pool_w.shape[1]
    eye = jnp.eye(groups, dtype=jnp.bfloat16)[None, :, None, :, None]
    wide = pool_w.astype(jnp.bfloat16)[:, :, :, None, :] * eye
    return wide.reshape(depth, POOL_WIDTH, POOL_WIDTH)


def kernel(x, attn_norm_g, w_in, attn_sinks, rel_bias, pool_w, pool_scale, w_out, mlp_norm_g, w_up, w_down,
           final_norm_g):
    depth = w_in.shape[0]
    seq = x.shape[1]
    bf16 = jnp.bfloat16
    col_scale = _w_in_col_scale()
    weights = ((w_in[0] * col_scale).astype(bf16), w_out[0].astype(bf16), None, None)
    w_pool = _pool_block_diag(pool_w)
    bias = _attention_bias(rel_bias)
    cos, sin = _rotary_tables(seq)
    dm, zeta_t, xi_t, gtab = _retention_tables()
    g_attn = attn_norm_g.reshape(depth, 1, D_MODEL)
    g_mlp = mlp_norm_g.reshape(depth, 1, D_MODEL)
    p_scale = pool_scale.reshape(depth, 1, POOL_WIDTH)
    gf = final_norm_g.reshape(1, D_MODEL)
    for layer in range(depth):
        last = layer == depth - 1
        w_in_b, w_out_b, w_up_b, w_down_b = weights
        x, converted = _mixer_call(layer, x, attn_sinks, g_attn, w_in_b, bias, cos, sin, dm, zeta_t, xi_t, gtab,
                                   w_pool, p_scale, w_out_b, mlp_weights=(w_up, w_down) if layer == 0 else None)
        if converted is not None:
            w_up_b, w_down_b = converted
        x, weights = _mlp_call(layer, x, g_mlp, w_up_b, w_down_b, gf, final_norm=last,
                               next_weights=None if last else (col_scale, w_in, w_out, w_up, w_down))
    return x
```

```python
import functools

import jax
import jax.numpy as jnp
import numpy as np
from jax import lax
from jax.experimental import pallas as pl
from jax.experimental.pallas import tpu as pltpu

D_MODEL = 1024
HEAD_DIM = 64
ATT_Q_HEADS = 6
ATT_KV_HEADS = 2
ATT_GROUP = ATT_Q_HEADS // ATT_KV_HEADS
RET_HEADS = 6
POOL_WINDOWS = (2, 4, 8, 16)
POOL_GROUP_WIDTH = 64
POOL_WIDTH = 256
BLOCK = 128
N_BUCKETS = 32
MAX_DISTANCE = 128
D_FF = 4 * D_MODEL
RMS_EPS = 1e-6
ROPE_BASE = 10000.0
NEG_INF = -1e30

LANES = 128
HALF = LANES // 2
ATT_Q_W = ATT_Q_HEADS * HEAD_DIM
ATT_KV_W = ATT_KV_HEADS * HEAD_DIM
RET_W = RET_HEADS * HEAD_DIM
RET_SLABS = RET_W // LANES
ATT_SLABS = ATT_Q_W // LANES
POOL_TAIL = 16

C_QA = 0
C_KA = C_QA + ATT_Q_W
C_VA = C_KA + ATT_KV_W
C_QR = C_VA + ATT_KV_W
C_KR = C_QR + RET_W
C_VR = C_KR + RET_W
C_GR = C_VR + RET_W
C_UP = C_GR + RET_W
Z_WIDTH = C_UP + POOL_WIDTH

Y_A = 0
Y_R = ATT_Q_W
Y_P = ATT_Q_W + RET_W

MIXER_TILE = 1024
MIXER_SUB = 128
PROJ_CHUNK = 256
OUT_CHUNK = 256
W_SCORES, W_SOFTMAX, W_VALUES, W_RETENTION = 0.5, 1.0, 1.0, 1.5
W_POOL_NARROW, W_POOL_WIDE, W_POOL_MAPS = 1.0, 3.0, 0.5
MLP_TILE = 1024
FF_CHUNK = 1024
VMEM_LIMIT_BYTES = 56 * 1024 * 1024


def _rms_norm(x, g):
    ms = jnp.mean(x * x, axis=-1, keepdims=True)
    return x * lax.rsqrt(ms + RMS_EPS) * g


def _t5_bucket(dist):
    max_exact = N_BUCKETS // 2
    n = np.maximum(dist, 0)
    large = max_exact + (np.log(np.maximum(n, 1) / max_exact)
                         / np.log(MAX_DISTANCE / max_exact)
                         * (N_BUCKETS - max_exact)).astype(np.int64)
    large = np.minimum(large, N_BUCKETS - 1)
    return np.where(n < max_exact, n, large).astype(np.int32)


def _spread(first, second):
    keyed = []
    for order, tasks in enumerate((first, second)):
        total = sum(weight for weight, _ in tasks)
        done = 0.0
        for weight, task in tasks:
            keyed.append(((done + 0.5 * weight) / total, order, (weight, task)))
            done += weight
    return [item for _, _, item in sorted(keyed, key=lambda entry: entry[:2])]


def _mixer_kernel(sinks_ref, x_ref, xn_ref, g_ref, win_ref, bias_ref, cos_ref, sin_ref, dm_ref,
                  zeta_ref, xi_ref, gtab_ref, wpool_ref, pscale_ref, wout_ref, *rest, tile, sub, layer, cast_mlp):
    if cast_mlp:
        wup_f32_ref, wdown_f32_ref, o_ref, wup_bf16_ref, wdown_bf16_ref, *scratch = rest
    else:
        o_ref, *scratch = rest
    z0_ref, z_ref, y_ref, kbuf, vbuf, ubuf, r_ref = scratch
    t = pl.program_id(1)
    npiece = tile // sub
    assert npiece >= 2, "piece 0 of the next tile is projected while the last piece is mixed"
    f32, bf16 = jnp.float32, jnp.bfloat16

    lane = lax.broadcasted_iota(jnp.int32, (BLOCK, LANES), 1)
    lo = lane < HALF

    def zrows(r0):
        return (z0_ref, r0) if r0 < sub else (z_ref, r0 - sub)

    def project_tasks(read_x, zdst, zrow, brow):
        cache = []

        def normed():
            if not cache:
                cache.append(_rms_norm(read_x(), g_ref[...]).astype(bf16))
            return cache[0]

        side = ((C_KA, ATT_KV_W, kbuf, BLOCK), (C_VA, ATT_KV_W, vbuf, BLOCK), (C_UP, POOL_WIDTH, ubuf, POOL_TAIL))
        chunks = [(c0, min(c0 + PROJ_CHUNK, Z_WIDTH)) for c0 in range(0, Z_WIDTH, PROJ_CHUNK)]

        def make(c0, c1):
            def task():
                z = jnp.dot(normed(), win_ref[:, c0:c1], preferred_element_type=f32)
                zdst[zrow:zrow + sub, c0:c1] = z
                for cs, width, buf, head in side:
                    a, b = max(c0, cs), min(c1, cs + width)
                    if a < b:
                        buf[head + brow:head + brow + sub, a - cs:b - cs] = z[:, a - c0:b - c0].astype(buf.dtype)
            return task

        return [((c1 - c0) / PROJ_CHUNK, make(c0, c1)) for c0, c1 in chunks]

    @pl.when(t == 0)
    def _():
        kbuf[0:BLOCK, :] = jnp.zeros((BLOCK, ATT_KV_W), bf16)
        vbuf[0:BLOCK, :] = jnp.zeros((BLOCK, ATT_KV_W), bf16)
        ubuf[0:POOL_TAIL, :] = jnp.zeros((POOL_TAIL, POOL_WIDTH), f32)
        r_ref[...] = jnp.zeros(r_ref.shape, f32)
        for _, task in project_tasks(lambda: x_ref[0, 0:sub, :], z0_ref, 0, 0):
            task()

    def attention_tasks(r0):
        live = {"ps": [], "denoms": []}

        def swap_halves(v):
            return pltpu.roll(v, HALF, 1)

        def scores():
            zr, zo = zrows(r0)
            qs = [zr[zo:zo + BLOCK, C_QA + j * LANES:C_QA + (j + 1) * LANES] for j in range(ATT_SLABS)]
            swapped = {}
            parts = []
            for hd in range(ATT_Q_HEADS):
                slab, half, group = hd // 2, hd % 2, hd // ATT_GROUP
                q = qs[slab]
                if half != group:
                    q = swapped.setdefault(slab, swap_halves(q))
                parts.append(jnp.where(lo, q, 0.0) if group == 0 else jnp.where(lo, 0.0, q))
            qstack = jnp.concatenate(parts, axis=0).astype(bf16)
            kk = kbuf[r0:r0 + 2 * BLOCK, :]
            s = lax.dot_general(kk, qstack, (((1,), (1,)), ((), ())), preferred_element_type=f32)
            live["s"] = s + (bias_ref[jnp.where(t == 0, 1, 0)] if r0 == 0 else bias_ref[0])

        def softmax(heads):
            for hd in heads:
                sh = live["s"][:, hd * BLOCK:(hd + 1) * BLOCK]
                sink = sinks_ref[layer, hd]
                m = jnp.maximum(jnp.max(sh, axis=0, keepdims=True), sink)
                p = jnp.exp(sh - m)
                live["denoms"].append(jnp.sum(p, axis=0, keepdims=True) + jnp.exp(sink - m))
                live["ps"].append(p.astype(bf16))

        def weighted_values():
            denoms = live["denoms"]
            probs = jnp.concatenate(live["ps"], axis=1)
            vv = vbuf[r0:r0 + 2 * BLOCK, :]
            o = lax.dot_general(vv, probs, (((0,), (0,)), ((), ())), preferred_element_type=f32)
            for j in range(ATT_SLABS):
                rows = []
                for hd in (2 * j, 2 * j + 1):
                    g = hd // ATT_GROUP
                    rows.append(o[g * HALF:(g + 1) * HALF, hd * BLOCK:(hd + 1) * BLOCK] / denoms[hd])
                slab_t = jnp.concatenate(rows, axis=0)
                y_ref[r0:r0 + BLOCK, Y_A + j * LANES:Y_A + (j + 1) * LANES] = slab_t.T.astype(bf16)

        return ([(W_SCORES, scores)] + [(W_SOFTMAX, functools.partial(softmax, (hd,))) for hd in range(ATT_Q_HEADS)]
                + [(W_VALUES, weighted_values)])

    even = lane % 2 == 0

    def rotary(v, cos, sin):
        partner = jnp.where(even, pltpu.roll(v, LANES - 1, 1), pltpu.roll(v, 1, 1))
        return v * cos + partner * sin

    def retention(r0, j, live, stage):
        zr, zo = zrows(r0)
        c0 = j * LANES
        if stage == 0:
            cos = cos_ref[r0:r0 + BLOCK, :]
            sin = sin_ref[r0:r0 + BLOCK, :]
            q = rotary(zr[zo:zo + BLOCK, C_QR + c0:C_QR + c0 + LANES], cos, sin)
            k = rotary(zr[zo:zo + BLOCK, C_KR + c0:C_KR + c0 + LANES], cos, sin)
            qb = q.astype(bf16)
            kb = k.astype(bf16)
            ksplit = jnp.concatenate([jnp.where(lo, k, 0.0), jnp.where(lo, 0.0, k)], axis=0).astype(bf16)
            sc = lax.dot_general(qb, ksplit, (((1,), (1,)), ((), ())), preferred_element_type=f32)
            live.update(qb=qb, kb=kb, acat=(sc * dm_ref[j]).astype(bf16))
            return
        qb, kb, acat = live["qb"], live["kb"], live["acat"]
        v = zr[zo:zo + BLOCK, C_VR + c0:C_VR + c0 + LANES]
        gate = zr[zo:zo + BLOCK, C_GR + c0:C_GR + c0 + LANES]
        vsplit = jnp.concatenate([jnp.where(lo, v, 0.0), jnp.where(lo, 0.0, v)], axis=0).astype(bf16)
        inner = jnp.dot(acat, vsplit, preferred_element_type=f32)
        r_prev = r_ref[j]
        cross = jnp.dot(qb, r_prev.astype(bf16), preferred_element_type=f32) * xi_ref[j]
        o = inner + cross
        vz = (v * zeta_ref[j]).astype(bf16)
        u = lax.dot_general(kb, vz, (((0,), (0,)), ((), ())), preferred_element_type=f32)
        gt = gtab_ref[j]
        r_ref[j] = r_prev * gt + jnp.where(gt > 0.0, u, 0.0)
        o2 = o * o
        ms_lo = jnp.sum(jnp.where(lo, o2, 0.0), axis=-1, keepdims=True)
        ms_hi = jnp.sum(jnp.where(lo, 0.0, o2), axis=-1, keepdims=True)
        ms = jnp.where(lo, ms_lo, ms_hi) * (1.0 / HEAD_DIM)
        on = o * lax.rsqrt(ms + RMS_EPS)
        y_ref[r0:r0 + BLOCK, Y_R + c0:Y_R + c0 + LANES] = (jax.nn.silu(gate) * on).astype(bf16)

    def pool_tasks(r0):
        pooled = {}

        def window(sl):
            lo_t = lax.broadcasted_iota(jnp.int32, (sub, LANES), 1) < HALF
            pos1 = lax.broadcasted_iota(jnp.int32, (sub, LANES), 0) + (t * tile + r0 + 1)
            w_small, w_big = POOL_WINDOWS[2 * sl], POOL_WINDOWS[2 * sl + 1]

            def shifted(d):
                return ubuf[POOL_TAIL + r0 - d:POOL_TAIL + r0 - d + sub, sl * LANES:(sl + 1) * LANES]

            u0 = shifted(0)
            acc = u0
            for d in range(1, w_small):
                acc = acc + shifted(d)
            small = acc
            for d in range(w_small, w_big):
                acc = acc + shifted(d)
            win = jnp.where(lo_t, small, acc)
            cnt = jnp.minimum(pos1, jnp.where(lo_t, w_small, w_big)).astype(f32)
            pooled[sl] = (win / cnt - u0).astype(bf16)

        def group_maps():
            both = jnp.concatenate([pooled[0], pooled[1]], axis=1)
            mixed = jnp.dot(both, wpool_ref[...], preferred_element_type=f32) * pscale_ref[...]
            y_ref[r0:r0 + sub, Y_P:Y_P + POOL_WIDTH] = mixed.astype(bf16)

        return [(W_POOL_NARROW, functools.partial(window, 0)), (W_POOL_WIDE, functools.partial(window, 1)),
                (W_POOL_MAPS, group_maps)]

    def output_tasks(r0):
        def make(c0):
            def task():
                o_ref[0, r0:r0 + sub, c0:c0 + OUT_CHUNK] = x_ref[0, r0:r0 + sub, c0:c0 + OUT_CHUNK] + jnp.dot(
                    y_ref[r0:r0 + sub, :], wout_ref[:, c0:c0 + OUT_CHUNK], preferred_element_type=f32)
            return task
        return [(1.0, make(c0)) for c0 in range(0, D_MODEL, OUT_CHUNK)]

    def mix_tasks(r0):
        tasks = []
        for b0 in range(r0, r0 + sub, BLOCK):
            tasks += attention_tasks(b0)
            for j in range(RET_SLABS):
                live = {}
                tasks += [(W_RETENTION / 2, functools.partial(retention, b0, j, live, 0)),
                          (W_RETENTION / 2, functools.partial(retention, b0, j, live, 1))]
        return tasks + pool_tasks(r0)

    for p in range(npiece):
        r0 = p * sub
        if p + 1 < npiece:
            matmuls = project_tasks(lambda r=r0 + sub: x_ref[0, r:r + sub, :], z_ref, r0, r0 + sub)
        else:
            matmuls = project_tasks(lambda: xn_ref[0], z0_ref, 0, tile)
        if p >= 1:
            matmuls = _spread(matmuls, output_tasks(r0 - sub))
        for _, task in _spread(matmuls, mix_tasks(r0)):
            task()
    for _, task in output_tasks(tile - sub):
        task()

    kbuf[0:BLOCK + sub, :] = kbuf[tile:tile + BLOCK + sub, :]
    vbuf[0:BLOCK + sub, :] = vbuf[tile:tile + BLOCK + sub, :]
    ubuf[0:POOL_TAIL + sub, :] = ubuf[tile:tile + POOL_TAIL + sub, :]

    if cast_mlp:
        wup_bf16_ref[...] = wup_f32_ref[...].astype(bf16)
        wdown_bf16_ref[...] = wdown_f32_ref[...].astype(bf16)


_RESIDENT = pl.Buffered(1)


def _const_spec(shape):
    nd = len(shape)
    return pl.BlockSpec(shape, lambda b, t, _nd=nd: (0,) * _nd, pipeline_mode=_RESIDENT)


def _layer_spec(shape, layer):
    nd = len(shape)
    return pl.BlockSpec((None,) + tuple(shape[1:]), lambda b, t, _nd=nd: (layer,) + (0,) * (_nd - 1),
                        pipeline_mode=_RESIDENT)


def _mixer_call(layer, x, sinks, g, win, bias, cos, sin, dm, zeta, xi, gtab, wpool, pscale, wout, mlp_weights=None):
    batch, seq, _ = x.shape
    tile, sub = MIXER_TILE, MIXER_SUB
    ntiles = seq // tile
    grid = (batch, ntiles)
    last_piece = seq // sub - 1
    f32, bf16 = jnp.float32, jnp.bfloat16
    rot_spec = pl.BlockSpec((tile, LANES), lambda b, t: (t, 0))
    x_spec = pl.BlockSpec((1, tile, D_MODEL), lambda b, t: (b, t, 0))
    in_specs = [
        pl.BlockSpec(memory_space=pltpu.SMEM),
        x_spec,
        pl.BlockSpec((1, sub, D_MODEL), lambda b, t: (b, jnp.minimum((t + 1) * (tile // sub), last_piece), 0)),
        _layer_spec(g.shape, layer), _const_spec(win.shape), _const_spec(bias.shape),
        rot_spec, rot_spec,
        _const_spec(dm.shape), _const_spec(zeta.shape), _const_spec(xi.shape), _const_spec(gtab.shape),
        _layer_spec(wpool.shape, layer), _layer_spec(pscale.shape, layer), _const_spec(wout.shape),
    ]
    operands = [sinks, x, x, g, win, bias, cos, sin, dm, zeta, xi, gtab, wpool, pscale, wout]
    out_specs = [x_spec]
    out_shape = [jax.ShapeDtypeStruct(x.shape, x.dtype)]
    if mlp_weights is not None:
        steps = batch * ntiles
        for w in mlp_weights:
            rows = w.shape[1] // steps
            in_specs.append(pl.BlockSpec((None, rows, w.shape[2]), lambda b, t: (layer, b * ntiles + t, 0)))
            out_specs.append(pl.BlockSpec((rows, w.shape[2]), lambda b, t: (b * ntiles + t, 0)))
            out_shape.append(jax.ShapeDtypeStruct(w.shape[1:], bf16))
            operands.append(w)
    outs = pl.pallas_call(
        functools.partial(_mixer_kernel, tile=tile, sub=sub, layer=layer, cast_mlp=mlp_weights is not None),
        grid=grid,
        in_specs=in_specs,
        out_specs=out_specs,
        out_shape=out_shape,
        scratch_shapes=[
            pltpu.VMEM((sub, Z_WIDTH), f32),
            pltpu.VMEM((tile - sub, Z_WIDTH), f32),
            pltpu.VMEM((tile, D_MODEL), bf16),
            pltpu.VMEM((BLOCK + tile + sub, ATT_KV_W), bf16),
            pltpu.VMEM((BLOCK + tile + sub, ATT_KV_W), bf16),
            pltpu.VMEM((POOL_TAIL + tile + sub, POOL_WIDTH), f32),
            pltpu.VMEM((RET_SLABS, LANES, LANES), f32),
        ],
        compiler_params=pltpu.CompilerParams(
            dimension_semantics=("arbitrary", "arbitrary"),
            vmem_limit_bytes=VMEM_LIMIT_BYTES),
        name="mixer",
    )(*operands)
    return outs[0], (tuple(outs[1:]) if mlp_weights is not None else None)


def _mlp_kernel(x_ref, g_ref, wup_ref, wdown_ref, gf_ref, *rest, final_norm, cast_next):
    f32, bf16 = jnp.float32, jnp.bfloat16
    o_ref = rest[len(rest) // 2] if cast_next else rest[0]
    x = x_ref[0]
    h = _rms_norm(x, g_ref[...]).astype(bf16)
    acc = x
    for c in range(D_FF // FF_CHUNK):
        a = jnp.dot(h, wup_ref[:, c * FF_CHUNK:(c + 1) * FF_CHUNK], preferred_element_type=f32)
        a = jnp.square(jnp.maximum(a, 0.0)).astype(bf16)
        acc = acc + jnp.dot(a, wdown_ref[c * FF_CHUNK:(c + 1) * FF_CHUNK, :], preferred_element_type=f32)
    if final_norm:
        acc = _rms_norm(acc, gf_ref[...])
    o_ref[0] = acc
    if cast_next:
        scale_ref, nwin_ref, nwout_ref, nwup_ref, nwdown_ref, _, owin_ref, owout_ref, owup_ref, owdown_ref = rest
        owin_ref[...] = (nwin_ref[...] * scale_ref[...]).astype(bf16)
        owout_ref[...] = nwout_ref[...].astype(bf16)
        owup_ref[...] = nwup_ref[...].astype(bf16)
        owdown_ref[...] = nwdown_ref[...].astype(bf16)


def _mlp_call(layer, x, g, wup, wdown, gf, final_norm, next_weights=None):
    batch, seq, _ = x.shape
    tile = MLP_TILE
    ntiles = seq // tile
    grid = (batch, ntiles)
    x_spec = pl.BlockSpec((1, tile, D_MODEL), lambda b, t: (b, t, 0))
    in_specs = [x_spec, _layer_spec(g.shape, layer), _const_spec(wup.shape), _const_spec(wdown.shape),
                _const_spec(gf.shape)]
    out_specs = [x_spec]
    out_shape = [jax.ShapeDtypeStruct(x.shape, x.dtype)]
    operands = [x, g, wup, wdown, gf]
    if next_weights is not None:
        col_scale, *stacked = next_weights
        steps = batch * ntiles
        in_specs.append(_const_spec(col_scale.shape))
        operands.append(col_scale)
        for w in stacked:
            rows = w.shape[1] // steps
            in_specs.append(pl.BlockSpec((None, rows, w.shape[2]), lambda b, t: (layer + 1, b * ntiles + t, 0)))
            out_specs.append(pl.BlockSpec((rows, w.shape[2]), lambda b, t: (b * ntiles + t, 0)))
            out_shape.append(jax.ShapeDtypeStruct(w.shape[1:], jnp.bfloat16))
            operands.append(w)
    outs = pl.pallas_call(
        functools.partial(_mlp_kernel, final_norm=final_norm, cast_next=next_weights is not None),
        grid=grid,
        in_specs=in_specs,
        out_specs=out_specs,
        out_shape=out_shape,
        compiler_params=pltpu.CompilerParams(
            dimension_semantics=("arbitrary", "arbitrary"),
            vmem_limit_bytes=VMEM_LIMIT_BYTES),
        name="mlp",
    )(*operands)
    return outs[0], (tuple(outs[1:]) if next_weights is not None else None)


def _w_in_col_scale():
    col_scale = np.ones((1, Z_WIDTH), np.float32)
    col_scale[:, C_QA:C_QA + ATT_Q_W] = HEAD_DIM ** -0.5
    return jnp.asarray(col_scale)


def _attention_bias(rel_bias):
    i = np.arange(BLOCK)[:, None]
    j = np.arange(2 * BLOCK)[None, :]
    dist = BLOCK + i - j
    band = (dist >= 0) & (dist < BLOCK)
    onehot = (_t5_bucket(dist).reshape(1, -1) == np.arange(N_BUCKETS)[:, None]).astype(np.float32)
    bias = jnp.dot(rel_bias.astype(jnp.float32).T, onehot, precision=lax.Precision.HIGHEST)
    bias = bias.reshape(ATT_Q_HEADS, BLOCK, 2 * BLOCK)
    normal = jnp.where(jnp.asarray(band)[None], bias, NEG_INF)
    first = jnp.where(jnp.asarray(band & (j >= BLOCK))[None], bias, NEG_INF)
    stacked = jnp.stack([normal, first]).reshape(2, ATT_Q_HEADS * BLOCK, 2 * BLOCK)
    return jnp.transpose(stacked, (0, 2, 1))


def _rotary_tables(seq):
    f32 = np.float32
    inv = (f32(1.0) / np.power(f32(ROPE_BASE), np.linspace(0.0, 1.0, HEAD_DIM // 2, dtype=f32))).astype(f32)
    inv_lane = np.tile(np.repeat(inv, 2), LANES // HEAD_DIM)
    ang = (np.arange(seq, dtype=f32)[:, None] * inv_lane[None, :]).astype(f32)
    cos, sin = np.cos(ang).astype(f32), np.sin(ang).astype(f32)
    even = (np.arange(LANES) % 2 == 0)[None, :]
    return jnp.asarray(cos), jnp.asarray(np.where(even, -sin, sin))


def _retention_tables():
    c = BLOCK
    f32 = np.float32
    scale = f32(HEAD_DIM ** -0.5)
    lg = np.log(f32(1.0) - f32(2.0) ** (f32(-5.0) - np.arange(RET_HEADS, dtype=f32))).astype(f32)
    idx = np.arange(c, dtype=f32)
    diff = idx[:, None] - idx[None, :]
    dmask = np.where(diff >= 0, np.exp(lg[:, None, None] * np.maximum(diff, f32(0.0))), f32(0.0)).astype(f32)
    dm = np.stack([np.concatenate([dmask[2 * j], dmask[2 * j + 1]], axis=1)
                   for j in range(RET_SLABS)]) * scale
    zeta = (np.exp(lg[:, None] * (f32(c) - f32(1.0) - idx)[None, :]) * scale).astype(f32)
    xi = np.exp(lg[None, :] * (idx[:, None] + f32(1.0))).astype(f32)
    g_chunk = np.exp(lg * f32(c)).astype(f32)
    head_of_lane = (np.arange(LANES) >= HALF).astype(np.int32)
    zeta_t = np.stack([zeta[2 * j + head_of_lane].T for j in range(RET_SLABS)])
    xi_t = np.stack([xi[:, 2 * j + head_of_lane] for j in range(RET_SLABS)])
    same = head_of_lane[:, None] == head_of_lane[None, :]
    gtab = np.stack([np.where(same, g_chunk[2 * j + head_of_lane][:, None], f32(0.0))
                     for j in range(RET_SLABS)]).astype(f32)
    return jnp.asarray(dm), jnp.asarray(zeta_t), jnp.asarray(xi_t), jnp.asarray(gtab)


def _pool_block_diag(pool_w):
    depth, groups = pool_w.shape[0], pool_w.shape[1]
    eye = jnp.eye(groups, dtype=jnp.bfloat16)[None, :, None, :, None]
    wide = pool_w.astype(jnp.bfloat16)[:, :, :, None, :] * eye
    return wide.reshape(depth, POOL_WIDTH, POOL_WIDTH)


def kernel(x, attn_norm_g, w_in, attn_sinks, rel_bias, pool_w, pool_scale, w_out, mlp_norm_g, w_up, w_down,
           final_norm_g):
    depth = w_in.shape[0]
    seq = x.shape[1]
    bf16 = jnp.bfloat16
    col_scale = _w_in_col_scale()
    weights = ((w_in[0] * col_scale).astype(bf16), w_out[0].astype(bf16), None, None)
    w_pool = _pool_block_diag(pool_w)
    bias = _attention_bias(rel_bias)
    cos, sin = _rotary_tables(seq)
    dm, zeta_t, xi_t, gtab = _retention_tables()
    g_attn = attn_norm_g.reshape(depth, 1, D_MODEL)
    g_mlp = mlp_norm_g.reshape(depth, 1, D_MODEL)
    p_scale = pool_scale.reshape(depth, 1, POOL_WIDTH)
    gf = final_norm_g.reshape(1, D_MODEL)
    for layer in range(depth):
        last = layer == depth - 1
        w_in_b, w_out_b, w_up_b, w_down_b = weights
        x, converted = _mixer_call(layer, x, attn_sinks, g_attn, w_in_b, bias, cos, sin, dm, zeta_t, xi_t, gtab,
                                   w_pool, p_scale, w_out_b, mlp_weights=(w_up, w_down) if layer == 0 else None)
        if converted is not None:
            w_up_b, w_down_b = converted
        x, weights = _mlp_call(layer, x, g_mlp, w_up_b, w_down_b, gf, final_norm=last,
                               next_weights=None if last else (col_scale, w_in, w_out, w_up, w_down))
    return x
```
